```python
import math
import jax, jax.numpy as jnp
from jax import lax
import numpy as np

D_MODEL = 2048
BATCH = 4
SEQ = 4096
DEPTH = 4

PLE_DIM = 256
HEAD_DIM = 128
NSA_HEADS = 8
NSA_KV_GROUPS = 2
NSA_CMP_BLOCK = 32
NSA_CMP_STRIDE = 16
NSA_CMP_HIDDEN = 256
NSA_SLC_BLOCK = 64
NSA_SLC_TOPN = 16
NSA_WINDOW = 512
SB_HEADS = 4
DSA_HEADS = 4
DSA_KV_RANK = 256
DSA_IDX_HEADS = 8
DSA_IDX_DIM = 64
DSA_TOPK_MAX = 256
D_FF = 5632
CONV_WIDTH = 3
REL_BUCKETS = 32
REL_MAX_DIST = 128
Q_BLOCK = 128
SLC_Q_BLOCK = 64

IN_SIZES = (
    3 * D_MODEL,
    NSA_HEADS * HEAD_DIM,
    NSA_KV_GROUPS * HEAD_DIM,
    NSA_KV_GROUPS * HEAD_DIM,
    NSA_KV_GROUPS * HEAD_DIM,
    NSA_KV_GROUPS * HEAD_DIM,
    NSA_KV_GROUPS * HEAD_DIM,
    NSA_KV_GROUPS * HEAD_DIM,
    3 * NSA_HEADS,
    SB_HEADS * HEAD_DIM,
    SB_HEADS * HEAD_DIM,
    SB_HEADS * HEAD_DIM,
    DSA_HEADS * HEAD_DIM,
    DSA_KV_RANK,
    DSA_IDX_HEADS * DSA_IDX_DIM,
    DSA_IDX_DIM,
    DSA_IDX_HEADS,
)
IN_COLS = 3 * D_MODEL + NSA_HEADS * HEAD_DIM + 6 * NSA_KV_GROUPS * HEAD_DIM + 3 * NSA_HEADS + 3 * SB_HEADS * HEAD_DIM + DSA_HEADS * HEAD_DIM + DSA_KV_RANK + DSA_IDX_HEADS * DSA_IDX_DIM + DSA_IDX_DIM + DSA_IDX_HEADS

kernel_name = "hybrid_nsa_stickbreak_dsa_convffn"


def _split_cols(u, sizes):
    outs, off = [], 0
    for s in sizes:
        outs.append(u[..., off:off + s])
        off += s
    return outs


def rmsnorm(x, g, eps=1e-6):
    xf = x.astype(jnp.float32)
    y = xf * lax.rsqrt(jnp.mean(xf * xf, axis=-1, keepdims=True) + eps)
    return (y * g.astype(jnp.float32)).astype(x.dtype)


def masked_softmax(logits, mask):
    l = jnp.where(mask, logits.astype(jnp.float32), -jnp.inf)
    m = jnp.max(l, axis=-1, keepdims=True)
    m = jnp.where(jnp.isfinite(m), m, 0.0)
    e = jnp.where(mask, jnp.exp(l - m), 0.0)
    return e / jnp.maximum(jnp.sum(e, axis=-1, keepdims=True), 1e-30)


def t5_bucket(dist):
    n = jnp.maximum(dist, 0)
    max_exact = REL_BUCKETS // 2
    nf = jnp.maximum(n, 1).astype(jnp.float32)
    large = max_exact + (jnp.log(nf / max_exact) / math.log(REL_MAX_DIST / max_exact)
                         * (REL_BUCKETS - max_exact)).astype(jnp.int32)
    large = jnp.minimum(large, REL_BUCKETS - 1)
    return jnp.where(n < max_exact, n, large)


def _unblock(o, axis):
    o = jnp.moveaxis(o, 0, axis)
    sh = o.shape
    return o.reshape(sh[:axis] + (sh[axis] * sh[axis + 1],) + sh[axis + 2:])


def nsa_mixer(q, k_cmp, v_cmp, k_slc, v_slc, k_win, v_win, g_branch,
              ck_w1, ck_w2, ck_pe, cv_w1, cv_w2, cv_pe, tab):
    B, S, _ = q.shape
    G, R, d = NSA_KV_GROUPS, NSA_HEADS // NSA_KV_GROUPS, HEAD_DIM
    scale = d ** -0.5
    q = q.reshape(B, S, G, R, d).transpose(0, 2, 3, 1, 4)

    def heads_kv(a):
        return a.reshape(B, S, G, d).transpose(0, 2, 1, 3)

    k_cmp, v_cmp, k_slc, v_slc, k_win, v_win = map(heads_kv, (k_cmp, v_cmp, k_slc, v_slc, k_win, v_win))
    t = jnp.arange(S)
    tab_gr = tab.reshape(REL_BUCKETS, G, R)

    nc = (S - NSA_CMP_BLOCK) // NSA_CMP_STRIDE + 1
    starts = jnp.arange(nc) * NSA_CMP_STRIDE
    blk_idx = starts[:, None] + jnp.arange(NSA_CMP_BLOCK)[None, :]

    def compress(a, w1, w2, pe):
        blk = a[:, :, blk_idx] + pe
        return jax.nn.gelu(blk.reshape(B, G, nc, NSA_CMP_BLOCK * d) @ w1) @ w2

    kc = compress(k_cmp, ck_w1, ck_w2, ck_pe)
    vc = compress(v_cmp, cv_w1, cv_w2, cv_pe)
    ends = starts + NSA_CMP_BLOCK - 1
    dist_c = t[:, None] - ends[None, :]
    bias_c = jnp.transpose(tab_gr[t5_bucket(dist_c)], (2, 3, 0, 1))
    logits_c = jnp.einsum('bgrsd,bgcd->bgrsc', q, kc).astype(jnp.float32) * scale + bias_c.astype(jnp.float32)
    p_cmp = masked_softmax(logits_c, dist_c >= 0)
    o_cmp = jnp.einsum('bgrsc,bgcd->bgrsd', p_cmp.astype(vc.dtype), vc)

    nsel = S // NSA_SLC_BLOCK
    n_top = min(NSA_SLC_TOPN, nsel)
    jstart = jnp.arange(nsel) * NSA_SLC_BLOCK
    overlap = ((starts[:, None] < jstart[None, :] + NSA_SLC_BLOCK)
               & (starts[:, None] + NSA_CMP_BLOCK > jstart[None, :])).astype(jnp.float32)
    imp = jnp.einsum('bgrsc,cj->bgsj', p_cmp, overlap)
    jj = jnp.arange(nsel)
    cur = t // NSA_SLC_BLOCK
    valid_blk = jstart[None, :] <= t[:, None]
    forced = (jj[None, :] == 0) | (jj[None, :] == cur[:, None]) | (jj[None, :] == cur[:, None] - 1)
    score = jnp.where(valid_blk, jnp.where(forced, 1e9, imp), -jnp.inf)
    _, sel_idx = lax.top_k(score, n_top)

    k_blocks = k_slc.reshape(B, G, nsel, NSA_SLC_BLOCK, d)
    v_blocks = v_slc.reshape(B, G, nsel, NSA_SLC_BLOCK, d)
    gather_bg = jax.vmap(jax.vmap(lambda kb, ix: kb[ix]))
    g_ar = jnp.arange(G)[None, :, None, None, None]

    def slc_block(i):
        s0 = i * SLC_Q_BLOCK
        qb = lax.dynamic_slice_in_dim(q, s0, SLC_Q_BLOCK, axis=3)
        ib = lax.dynamic_slice_in_dim(sel_idx, s0, SLC_Q_BLOCK, axis=2)
        kg = gather_bg(k_blocks, ib)
        vg = gather_bg(v_blocks, ib)
        tq = s0 + jnp.arange(SLC_Q_BLOCK)
        kpos = ib[..., None] * NSA_SLC_BLOCK + jnp.arange(NSA_SLC_BLOCK)
        dist = tq[None, None, :, None, None] - kpos
        bias = jnp.moveaxis(tab_gr[t5_bucket(dist), g_ar], -1, 2)
        logits = jnp.einsum('bgrqd,bgqnkd->bgrqnk', qb, kg).astype(jnp.float32) * scale + bias.astype(jnp.float32)
        m_ = n_top * NSA_SLC_BLOCK
        p = masked_softmax(logits.reshape(B, G, R, SLC_Q_BLOCK, m_),
                           (dist >= 0).reshape(B, G, 1, SLC_Q_BLOCK, m_))
        return jnp.einsum('bgrqm,bgqmd->bgrqd', p.astype(vg.dtype), vg.reshape(B, G, SLC_Q_BLOCK, m_, d))

    o_slc = _unblock(lax.map(slc_block, jnp.arange(S // SLC_Q_BLOCK)), 3)

    span = NSA_WINDOW + Q_BLOCK
    kw = jnp.pad(k_win, ((0, 0), (0, 0), (NSA_WINDOW, 0), (0, 0)))
    vw = jnp.pad(v_win, ((0, 0), (0, 0), (NSA_WINDOW, 0), (0, 0)))

    def win_block(i):
        s0 = i * Q_BLOCK
        qb = lax.dynamic_slice_in_dim(q, s0, Q_BLOCK, axis=3)
        kb = lax.dynamic_slice_in_dim(kw, s0, span, axis=2)
        vb = lax.dynamic_slice_in_dim(vw, s0, span, axis=2)
        kpos = s0 - NSA_WINDOW + jnp.arange(span)
        tq = s0 + jnp.arange(Q_BLOCK)
        dist = tq[:, None] - kpos[None, :]
        mask = (dist >= 0) & (dist < NSA_WINDOW) & (kpos[None, :] >= 0)
        bias = jnp.transpose(tab_gr[t5_bucket(dist)], (2, 3, 0, 1))
        logits = jnp.einsum('bgrqd,bgkd->bgrqk', qb, kb).astype(jnp.float32) * scale + bias.astype(jnp.float32)
        p = masked_softmax(logits, mask)
        return jnp.einsum('bgrqk,bgkd->bgrqd', p.astype(vb.dtype), vb)

    o_win = _unblock(lax.map(win_block, jnp.arange(S // Q_BLOCK)), 3)

    g = jax.nn.sigmoid(g_branch.reshape(B, S, G, R, 3).transpose(0, 2, 3, 1, 4))
    o = g[..., 0:1] * o_cmp + g[..., 1:2] * o_slc + g[..., 2:3] * o_win
    return o.transpose(0, 3, 1, 2, 4).reshape(B, S, NSA_HEADS * d)


def stick_breaking(q, k, v):
    B, S, _ = q.shape
    H, d = SB_HEADS, HEAD_DIM
    scale = d ** -0.5
    q, k, v = (a.reshape(B, S, H, d).transpose(0, 2, 1, 3) for a in (q, k, v))
    ks = jnp.arange(S)

    def blk(i):
        s0 = i * Q_BLOCK
        qb = lax.dynamic_slice_in_dim(q, s0, Q_BLOCK, axis=2)
        z = jnp.einsum('bhqd,bhkd->bhqk', qb, k).astype(jnp.float32) * scale
        tq = s0 + jnp.arange(Q_BLOCK)
        mask = ks[None, :] < tq[:, None]
        log_keep = jnp.where(mask, jax.nn.log_sigmoid(-z), 0.0)
        after = lax.cumsum(log_keep, axis=3, reverse=True) - log_keep
        w = jnp.where(mask, jnp.exp(jax.nn.log_sigmoid(z) + after), 0.0)
        return jnp.einsum('bhqk,bhkd->bhqd', w.astype(v.dtype), v)

    o = _unblock(lax.map(blk, jnp.arange(S // Q_BLOCK)), 2)
    return o.transpose(0, 2, 1, 3).reshape(B, S, H * d)


def dsa_mixer(q, c_kv, q_idx, k_idx, w_idx, kv_norm, w_uk, w_uv, tab):
    B, S, _ = q.shape
    H, d = DSA_HEADS, HEAD_DIM
    scale = d ** -0.5
    n_keep = min(DSA_TOPK_MAX, S // 4)
    q = q.reshape(B, S, H, d)
    ckv = rmsnorm(c_kv, kv_norm)
    k = ckv @ w_uk
    v = ckv @ w_uv
    qi = q_idx.reshape(B, S, DSA_IDX_HEADS, DSA_IDX_DIM)
    wi = w_idx.astype(jnp.float32) * (DSA_IDX_HEADS ** -0.5) * (DSA_IDX_DIM ** -0.5)
    ks = jnp.arange(S)
    gather_b = jax.vmap(lambda a, ix: a[ix])

    def blk(i):
        s0 = i * Q_BLOCK
        qb = lax.dynamic_slice_in_dim(q, s0, Q_BLOCK, axis=1)
        qib = lax.dynamic_slice_in_dim(qi, s0, Q_BLOCK, axis=1)
        wib = lax.dynamic_slice_in_dim(wi, s0, Q_BLOCK, axis=1)
        tq = s0 + jnp.arange(Q_BLOCK)
        dots = jnp.einsum('bqhe,bse->bqhs', qib, k_idx).astype(jnp.float32)
        score = jnp.einsum('bqh,bqhs->bqs', wib, jax.nn.relu(dots))
        score = jnp.where(ks[None, None, :] <= tq[None, :, None], score, -jnp.inf)
        _, idx = lax.top_k(score, n_keep)
        kg = gather_b(k, idx)
        vg = gather_b(v, idx)
        dist = tq[None, :, None] - idx
        bias = jnp.moveaxis(tab[t5_bucket(dist)], -1, 2)
        logits = jnp.einsum('bqhd,bqnd->bqhn', qb, kg).astype(jnp.float32) * scale + bias.astype(jnp.float32)
        p = masked_softmax(logits, (dist >= 0)[:, :, None, :])
        return jnp.einsum('bqhn,bqnd->bqhd', p.astype(vg.dtype), vg)

    o = _unblock(lax.map(blk, jnp.arange(S // Q_BLOCK)), 1)
    return o.reshape(B, S, H * d)


def token_mixer(h, w_in, w_a, w_b, w_c, w_out, ck_w1, ck_w2, ck_pe, cv_w1, cv_w2, cv_pe,
                kv_norm, w_uk, w_uv, rel_tab):
    B, S, _ = h.shape
    u = h @ w_in
    (g_merge, nsa_q, k_cmp, v_cmp, k_slc, v_slc, k_win, v_win, nsa_g,
     sb_q, sb_k, sb_v, dsa_q, dsa_ckv, idx_q, idx_k, idx_w) = _split_cols(u, IN_SIZES)
    o_a = nsa_mixer(nsa_q, k_cmp, v_cmp, k_slc, v_slc, k_win, v_win, nsa_g,
                    ck_w1, ck_w2, ck_pe, cv_w1, cv_w2, cv_pe, rel_tab[:, :NSA_HEADS])
    o_b = stick_breaking(sb_q, sb_k, sb_v)
    o_c = dsa_mixer(dsa_q, dsa_ckv, idx_q, idx_k, idx_w, kv_norm, w_uk, w_uv, rel_tab[:, NSA_HEADS:])
    g = jax.nn.sigmoid(g_merge.reshape(B, S, 3, D_MODEL))
    y = g[:, :, 0] * (o_a @ w_a) + g[:, :, 1] * (o_b @ w_b) + g[:, :, 2] * (o_c @ w_c)
    return y @ w_out


def conv_ffn(h, w_gate, w_up, w_down, conv_w, conv_b):
    a = h @ w_gate
    a = lax.conv_general_dilated(a, conv_w[:, None, :], window_strides=(1,),
                                 padding=((CONV_WIDTH - 1, 0),),
                                 dimension_numbers=('NWC', 'WIO', 'NWC'),
                                 feature_group_count=D_FF) + conv_b
    return (jax.nn.gelu(a) * (h @ w_up)) @ w_down


def setup_inputs(seed: int = 0) -> dict:
    key = jax.random.key(seed)
    ks = iter(jax.random.split(key, 32))

    def nrm(shape, scale):
        return jax.random.normal(next(ks), shape, jnp.float32) * scale

    L = DEPTH
    return {
        'x': nrm((BATCH, SEQ, D_MODEL), 1.0),
        'p': nrm((DEPTH, BATCH, SEQ, PLE_DIM), 1.0),
        'w_in': nrm((L, D_MODEL, IN_COLS), D_MODEL ** -0.5),
        'norm_mix': 1.0 + nrm((L, D_MODEL), 0.01),
        'norm_ffn': 1.0 + nrm((L, D_MODEL), 0.01),
        'norm_ple': 1.0 + nrm((L, D_MODEL), 0.01),
        'norm_final': 1.0 + nrm((D_MODEL,), 0.01),
        'w_proj_a': nrm((L, NSA_HEADS * HEAD_DIM, D_MODEL), (NSA_HEADS * HEAD_DIM) ** -0.5),
        'w_proj_b': nrm((L, SB_HEADS * HEAD_DIM, D_MODEL), (SB_HEADS * HEAD_DIM) ** -0.5),
        'w_proj_c': nrm((L, DSA_HEADS * HEAD_DIM, D_MODEL), (DSA_HEADS * HEAD_DIM) ** -0.5),
        'w_out': nrm((L, D_MODEL, D_MODEL), D_MODEL ** -0.5),
        'cmp_k_w1': nrm((L, NSA_CMP_BLOCK * HEAD_DIM, NSA_CMP_HIDDEN), (NSA_CMP_BLOCK * HEAD_DIM) ** -0.5),
        'cmp_k_w2': nrm((L, NSA_CMP_HIDDEN, HEAD_DIM), NSA_CMP_HIDDEN ** -0.5),
        'cmp_k_pe': nrm((L, NSA_CMP_BLOCK, HEAD_DIM), 0.1),
        'cmp_v_w1': nrm((L, NSA_CMP_BLOCK * HEAD_DIM, NSA_CMP_HIDDEN), (NSA_CMP_BLOCK * HEAD_DIM) ** -0.5),
        'cmp_v_w2': nrm((L, NSA_CMP_HIDDEN, HEAD_DIM), NSA_CMP_HIDDEN ** -0.5),
        'cmp_v_pe': nrm((L, NSA_CMP_BLOCK, HEAD_DIM), 0.1),
        'dsa_kv_norm': 1.0 + nrm((L, DSA_KV_RANK), 0.01),
        'dsa_w_uk': nrm((L, DSA_KV_RANK, HEAD_DIM), DSA_KV_RANK ** -0.5),
        'dsa_w_uv': nrm((L, DSA_KV_RANK, HEAD_DIM), DSA_KV_RANK ** -0.5),
        'rel_bias_table': nrm((REL_BUCKETS, NSA_HEADS + DSA_HEADS), 0.5),
        'ffn_w_gate': nrm((L, D_MODEL, D_FF), D_MODEL ** -0.5),
        'ffn_w_up': nrm((L, D_MODEL, D_FF), D_MODEL ** -0.5),
        'ffn_w_down': nrm((L, D_FF, D_MODEL), D_FF ** -0.5),
        'ffn_conv_w': nrm((L, CONV_WIDTH, D_FF), CONV_WIDTH ** -0.5),
        'ffn_conv_b': nrm((L, D_FF), 0.01),
        'ple_w_gate': nrm((L, D_MODEL, D_MODEL), D_MODEL ** -0.5),
        'ple_w_proj': nrm((L, PLE_DIM, D_MODEL), PLE_DIM ** -0.5),
    }


def reference(x, p, w_in, norm_mix, norm_ffn, norm_ple, norm_final, w_proj_a, w_proj_b, w_proj_c,
              w_out, cmp_k_w1, cmp_k_w2, cmp_k_pe, cmp_v_w1, cmp_v_w2, cmp_v_pe, dsa_kv_norm,
              dsa_w_uk, dsa_w_uv, rel_bias_table, ffn_w_gate, ffn_w_up, ffn_w_down, ffn_conv_w,
              ffn_conv_b, ple_w_gate, ple_w_proj):
    for i in range(DEPTH):
        h = rmsnorm(x, norm_mix[i])
        x = x + token_mixer(h, w_in[i], w_proj_a[i], w_proj_b[i], w_proj_c[i], w_out[i],
                            cmp_k_w1[i], cmp_k_w2[i], cmp_k_pe[i], cmp_v_w1[i], cmp_v_w2[i], cmp_v_pe[i],
                            dsa_kv_norm[i], dsa_w_uk[i], dsa_w_uv[i], rel_bias_table)
        h = rmsnorm(x, norm_ffn[i])
        x = x + conv_ffn(h, ffn_w_gate[i], ffn_w_up[i], ffn_w_down[i], ffn_conv_w[i], ffn_conv_b[i])
        x = x + jax.nn.sigmoid(rmsnorm(x, norm_ple[i]) @ ple_w_gate[i]) * (p[i] @ ple_w_proj[i])
    return rmsnorm(x, norm_final)
```

```python
import functools
import math

import numpy as np
import jax
import jax.numpy as jnp
from jax import lax
from jax.experimental import pallas as pl
from jax.experimental.pallas import tpu as pltpu

F32 = jnp.float32
MXU_DTYPE = jnp.bfloat16

HEAD_DIM = 128
NSA_HEADS = 8
NSA_GROUPS = 2
NSA_REP = NSA_HEADS // NSA_GROUPS
CMP_BLOCK = 32
CMP_STRIDE = 16
SLC_BLOCK = 64
SLC_TOPN = 16
NSA_WINDOW = 512
SB_HEADS = 4
DSA_HEADS = 4
DSA_KV_RANK = 256
IDX_HEADS = 8
IDX_DIM = 64
DSA_TOPK = 256
REL_BUCKETS = 32
REL_MAX_DIST = 128
CONV_WIDTH = 3
EPS = 1e-6

MASKED = -1e30
UNSELECTED = -1e9
INT_MIN = -2 ** 31
VMEM_LIMIT = 56 * 1024 * 1024

TQ = 128
KT = 512
SB_T = 256
TMPL_C0 = 896
TMPL_W = TMPL_C0 + KT

HM_NSA_Q, HM_K_SLC, HM_V_SLC, HM_K_WIN, HM_V_WIN = 0, 8, 10, 12, 14
HM_SB_Q, HM_SB_K, HM_SB_V, HM_DSA_Q, HM_HEADS = 16, 20, 24, 28, 32
MISC_IDXQ, MISC_CKV, MISC_IDXK, MISC_GATE, MISC_COLS = 0, 4, 6, 7, 9 * 128


def _cparams(sem):
    return pltpu.CompilerParams(dimension_semantics=sem, vmem_limit_bytes=VMEM_LIMIT)


def _dot(a, b):
    return jnp.dot(a, b, preferred_element_type=F32)


def _dot_nt(a, b):
    return lax.dot_general(a, b, (((1,), (1,)), ((), ())), preferred_element_type=F32)


def _rmsnorm_kernel(x_ref, g_ref, o_ref):
    x = x_ref[...]
    y = x * lax.rsqrt(jnp.mean(x * x, axis=-1, keepdims=True) + EPS)
    o_ref[...] = (y * g_ref[...]).astype(o_ref.dtype)


def _rmsnorm(x, g, out_dtype, tm=512):
    m, d = x.shape
    return pl.pallas_call(
        _rmsnorm_kernel,
        grid=(m // tm,),
        in_specs=[pl.BlockSpec((tm, d), lambda i: (i, 0)), pl.BlockSpec((1, d), lambda i: (0, 0))],
        out_specs=pl.BlockSpec((tm, d), lambda i: (i, 0)),
        out_shape=jax.ShapeDtypeStruct((m, d), out_dtype),
        compiler_params=_cparams(("parallel",)),
    )(x, g.reshape(1, d))


def _mm_kernel(a_ref, w_ref, o_ref):
    o_ref[...] = _dot(a_ref[...], w_ref[...]).astype(o_ref.dtype)


def _matmul(a, w, out_dtype, tm, tn):
    m, k = a.shape
    n = w.shape[1]
    return pl.pallas_call(
        _mm_kernel,
        grid=(m // tm, n // tn),
        in_specs=[pl.BlockSpec((tm, k), lambda i, j: (i, 0)), pl.BlockSpec((k, tn), lambda i, j: (0, j))],
        out_specs=pl.BlockSpec((tm, tn), lambda i, j: (i, j)),
        out_shape=jax.ShapeDtypeStruct((m, n), out_dtype),
        compiler_params=_cparams(("parallel", "parallel")),
    )(a, w)


def _mm_heads_kernel(a_ref, w_ref, o_ref, *, hb):
    r = _dot(a_ref[...], w_ref[...])
    for j in range(hb):
        o_ref[0, j] = r[:, j * HEAD_DIM:(j + 1) * HEAD_DIM].astype(o_ref.dtype)


def _matmul_heads(a, w, batch, out_dtype, tm=512, hb=4):
    m, k = a.shape
    s = m // batch
    nh = w.shape[1] // HEAD_DIM
    spb = s // tm
    return pl.pallas_call(
        functools.partial(_mm_heads_kernel, hb=hb),
        grid=(batch, spb, nh // hb),
        in_specs=[pl.BlockSpec((tm, k), lambda b, i, j: (b * spb + i, 0)),
                  pl.BlockSpec((k, hb * HEAD_DIM), lambda b, i, j: (0, j))],
        out_specs=pl.BlockSpec((1, hb, tm, HEAD_DIM), lambda b, i, j: (b, j, i, 0)),
        out_shape=jax.ShapeDtypeStruct((batch, nh, s, HEAD_DIM), out_dtype),
        compiler_params=_cparams(("parallel", "parallel", "parallel")),
    )(a, w)


def _mm_res_kernel(a_ref, w_ref, x_ref, o_ref):
    o_ref[...] = x_ref[...] + _dot(a_ref[...], w_ref[...])


def _matmul_residual(a, w, x, tm=512, tn=512):
    m, k = a.shape
    n = w.shape[1]
    return pl.pallas_call(
        _mm_res_kernel,
        grid=(m // tm, n // tn),
        in_specs=[pl.BlockSpec((tm, k), lambda i, j: (i, 0)), pl.BlockSpec((k, tn), lambda i, j: (0, j)),
                  pl.BlockSpec((tm, tn), lambda i, j: (i, j))],
        out_specs=pl.BlockSpec((tm, tn), lambda i, j: (i, j)),
        out_shape=jax.ShapeDtypeStruct((m, n), F32),
        compiler_params=_cparams(("parallel", "parallel")),
    )(a, w, x)


def _merge_kernel(oa_ref, ob_ref, oc_ref, wa_ref, wb_ref, wc_ref, ga_ref, gb_ref, gc_ref, y_ref):
    y = jax.nn.sigmoid(ga_ref[...]) * _dot(oa_ref[...], wa_ref[...])
    y += jax.nn.sigmoid(gb_ref[...]) * _dot(ob_ref[...], wb_ref[...])
    y += jax.nn.sigmoid(gc_ref[...]) * _dot(oc_ref[...], wc_ref[...])
    y_ref[...] = y.astype(y_ref.dtype)


def _merge(o_a, o_b, o_c, w_a, w_b, w_c, gates, tm=512, tn=512):
    m = o_a.shape[0]
    d = w_a.shape[1]
    nb = d // tn
    row = lambda i, j: (i, 0)
    col = lambda i, j: (0, j)
    return pl.pallas_call(
        _merge_kernel,
        grid=(m // tm, nb),
        in_specs=[pl.BlockSpec((tm, o_a.shape[1]), row), pl.BlockSpec((tm, o_b.shape[1]), row),
                  pl.BlockSpec((tm, o_c.shape[1]), row),
                  pl.BlockSpec((w_a.shape[0], tn), col), pl.BlockSpec((w_b.shape[0], tn), col),
                  pl.BlockSpec((w_c.shape[0], tn), col),
                  pl.BlockSpec((tm, tn), lambda i, j: (i, j)),
                  pl.BlockSpec((tm, tn), lambda i, j: (i, nb + j)),
                  pl.BlockSpec((tm, tn), lambda i, j: (i, 2 * nb + j))],
        out_specs=pl.BlockSpec((tm, tn), lambda i, j: (i, j)),
        out_shape=jax.ShapeDtypeStruct((m, d), MXU_DTYPE),
        compiler_params=_cparams(("parallel", "parallel")),
    )(o_a, o_b, o_c, w_a, w_b, w_c, gates, gates, gates)


FFN_HALO = 16


def _ffn_kernel(h_ref, hp_ref, x_ref, wg_ref, wu_ref, wd_ref, cw_ref, cb_ref, o_ref, hext_ref, *, tiles_per_seq):
    i = pl.program_id(0)
    f = pl.program_id(1)
    tm = h_ref.shape[0]

    @pl.when(f == 0)
    def _():
        first = (i % tiles_per_seq) == 0
        hext_ref[0:FFN_HALO, :] = jnp.where(first, jnp.zeros_like(hp_ref[...]), hp_ref[...])
        hext_ref[FFN_HALO:, :] = h_ref[...]
        o_ref[...] = x_ref[...]

    a = _dot(hext_ref[...], wg_ref[...])
    cw = cw_ref[...]
    c = (cw[0:1] * a[FFN_HALO - 2:FFN_HALO - 2 + tm] + cw[1:2] * a[FFN_HALO - 1:FFN_HALO - 1 + tm]
         + cw[2:3] * a[FFN_HALO:FFN_HALO + tm]) + cb_ref[...]
    u = _dot(h_ref[...], wu_ref[...])
    act = (jax.nn.gelu(c) * u).astype(MXU_DTYPE)
    o_ref[...] += _dot(act, wd_ref[...])


def _conv_ffn(h, x, w_gate, w_up, w_down, conv_w, conv_b, seq, tm=512, tf=512):
    m, d = h.shape
    ff = w_gate.shape[1]
    hb = tm // FFN_HALO
    return pl.pallas_call(
        functools.partial(_ffn_kernel, tiles_per_seq=seq // tm),
        grid=(m // tm, ff // tf),
        in_specs=[pl.BlockSpec((tm, d), lambda i, f: (i, 0)),
                  pl.BlockSpec((FFN_HALO, d), lambda i, f: (jnp.maximum(i * hb - 1, 0), 0)),
                  pl.BlockSpec((tm, d), lambda i, f: (i, 0)),
                  pl.BlockSpec((d, tf), lambda i, f: (0, f)),
                  pl.BlockSpec((d, tf), lambda i, f: (0, f)),
                  pl.BlockSpec((tf, d), lambda i, f: (f, 0)),
                  pl.BlockSpec((CONV_WIDTH, tf), lambda i, f: (0, f)),
                  pl.BlockSpec((1, tf), lambda i, f: (0, f))],
        out_specs=pl.BlockSpec((tm, d), lambda i, f: (i, 0)),
        out_shape=jax.ShapeDtypeStruct((m, d), F32),
        scratch_shapes=[pltpu.VMEM((tm + FFN_HALO, d), MXU_DTYPE)],
        compiler_params=_cparams(("parallel", "arbitrary")),
    )(h, h, x, w_gate, w_up, w_down, conv_w, conv_b.reshape(1, ff))


def _ple_kernel(h_ref, p_ref, x_ref, wg_ref, wp_ref, o_ref):
    gate = jax.nn.sigmoid(_dot(h_ref[...], wg_ref[...]))
    o_ref[...] = x_ref[...] + gate * _dot(p_ref[...].astype(MXU_DTYPE), wp_ref[...])


def _ple(h, p, x, w_gate, w_proj, tm=512, tn=512):
    m, d = h.shape
    return pl.pallas_call(
        _ple_kernel,
        grid=(m // tm, d // tn),
        in_specs=[pl.BlockSpec((tm, d), lambda i, j: (i, 0)),
                  pl.BlockSpec((tm, p.shape[1]), lambda i, j: (i, 0)),
                  pl.BlockSpec((tm, tn), lambda i, j: (i, j)),
                  pl.BlockSpec((d, tn), lambda i, j: (0, j)),
                  pl.BlockSpec((p.shape[1], tn), lambda i, j: (0, j))],
        out_specs=pl.BlockSpec((tm, tn), lambda i, j: (i, j)),
        out_shape=jax.ShapeDtypeStruct((m, d), F32),
        compiler_params=_cparams(("parallel", "parallel")),
    )(h, p, x, w_gate, w_proj)


def _compress_kernel(x_ref, w1_ref, w2_ref, pe_ref, o_ref):
    x = x_ref[0, 0]
    pe = pe_ref[0]
    half = x.shape[1]
    lo = _dot((x + pe[0:1]).astype(MXU_DTYPE), w1_ref[0, :half, :])
    hi = _dot((x + pe[1:2]).astype(MXU_DTYPE), w1_ref[0, half:, :])
    n = x.shape[0]
    hid = lo + pltpu.roll(hi, n - 1, 0)
    o_ref[0, 0, 0] = _dot(jax.nn.gelu(hid).astype(MXU_DTYPE), w2_ref[0]).astype(o_ref.dtype)


def _compress(kv_hm, w1, w2, pe):
    b, _, s, d = kv_hm.shape
    nrow = s // CMP_STRIDE
    x = kv_hm.reshape(b, 2 * NSA_GROUPS, nrow, CMP_STRIDE * d)
    return pl.pallas_call(
        _compress_kernel,
        grid=(b, 2, NSA_GROUPS),
        in_specs=[pl.BlockSpec((1, 1, nrow, CMP_STRIDE * d), lambda bi, kv, g: (bi, kv * NSA_GROUPS + g, 0, 0)),
                  pl.BlockSpec((1,) + w1.shape[1:], lambda bi, kv, g: (kv, 0, 0)),
                  pl.BlockSpec((1,) + w2.shape[1:], lambda bi, kv, g: (kv, 0, 0)),
                  pl.BlockSpec((1, 2, CMP_STRIDE * d), lambda bi, kv, g: (kv, 0, 0))],
        out_specs=pl.BlockSpec((1, 1, 1, nrow, d), lambda bi, kv, g: (bi, kv, g, 0, 0)),
        out_shape=jax.ShapeDtypeStruct((b, 2, NSA_GROUPS, nrow, d), MXU_DTYPE),
        compiler_params=_cparams(("parallel", "parallel", "parallel")),
    )(x, w1, w2, pe)


def _softmax_pv(lg_ref, mx_ref, acc_ref, v_ref, v_index, n_tiles):
    rows = lg_ref.shape[0]
    m = jnp.max(mx_ref[...], axis=1, keepdims=True)
    mx_ref[...] = jnp.zeros_like(mx_ref)
    acc_ref[...] = jnp.zeros_like(acc_ref)

    def body(kt, carry):
        k0 = pl.multiple_of(kt * KT, KT)
        p = jnp.exp(lg_ref[:, pl.ds(k0, KT)] - m)
        part = p[:, 0:128]
        for c in range(1, KT // 128):
            part = part + p[:, c * 128:(c + 1) * 128]
        mx_ref[...] += part
        acc_ref[...] += _dot(p.astype(MXU_DTYPE), v_ref[v_index + (pl.ds(k0, KT), slice(None))])
        return carry

    lax.fori_loop(0, n_tiles, body, 0)
    l = jnp.sum(mx_ref[...], axis=1, keepdims=True)
    return acc_ref[...] / l


def _lane_max(s):
    part = s[:, 0:128]
    for c in range(1, s.shape[1] // 128):
        part = jnp.maximum(part, s[:, c * 128:(c + 1) * 128])
    return part


def _nsa_kernel(q_ref, kc_ref, vc_ref, kslc_ref, vslc_ref, kwin_ref, vwin_ref, gate_ref,
                wslc_ref, wwin_ref, tcmp_ref, ovl_ref, o_ref,
                kslc_aug, kwin_aug, vwin_pad, qaug, lg_ref, mx_ref, acc_ref):
    i = pl.program_id(2)
    seq = kslc_ref.shape[2]
    d = HEAD_DIM
    rows = NSA_REP * TQ
    scale = d ** -0.5
    t0 = i * TQ

    @pl.when(i == 0)
    def _():
        kslc_aug[:, 0:d] = kslc_ref[0, 0]
        srow = lax.broadcasted_iota(jnp.int32, (seq, d), 0)
        lane = lax.broadcasted_iota(jnp.int32, (seq, d), 1)
        kslc_aug[:, d:2 * d] = jnp.where((srow >> 6) == lane, 1.0, 0.0).astype(kslc_aug.dtype)
        kwin_aug[0:NSA_WINDOW, 0:d] = jnp.zeros((NSA_WINDOW, d), kwin_aug.dtype)
        kwin_aug[NSA_WINDOW:, 0:d] = kwin_ref[0, 0]
        prow = lax.broadcasted_iota(jnp.int32, (seq + NSA_WINDOW, d), 0)
        plane = lax.broadcasted_iota(jnp.int32, (seq + NSA_WINDOW, d), 1)
        flag = jnp.where(prow < NSA_WINDOW, jnp.where(plane == SLC_BLOCK, UNSELECTED, 0.0), 0.0)
        kwin_aug[:, d:2 * d] = flag.astype(kwin_aug.dtype)
        vwin_pad[0:NSA_WINDOW, :] = jnp.zeros((NSA_WINDOW, d), vwin_pad.dtype)
        vwin_pad[NSA_WINDOW:, :] = vwin_ref[0, 0]

    q4 = q_ref[0].reshape(rows, d)

    ncmp = kc_ref.shape[3]
    kc = kc_ref[0, 0, 0]
    vc = vc_ref[0, 0, 0]
    trow = t0 + lax.broadcasted_iota(jnp.int32, (TQ, ncmp), 0)
    cend = lax.broadcasted_iota(jnp.int32, (TQ, ncmp), 1) * CMP_STRIDE + (CMP_BLOCK - 1)
    valid_c = cend <= trow
    sc_all = _dot_nt(q4, kc) * scale
    c0 = i * (TQ // CMP_STRIDE)
    imp = jnp.zeros((TQ, d), F32)
    o_cmp = []
    for r in range(NSA_REP):
        bias = pltpu.roll(tcmp_ref[r], c0, 1)
        l = jnp.where(valid_c, sc_all[r * TQ:(r + 1) * TQ] + bias, MASKED)
        m = jnp.max(l, axis=1, keepdims=True)
        e = jnp.where(valid_c, jnp.exp(l - m), 0.0)
        p = e / jnp.maximum(jnp.sum(e, axis=1, keepdims=True), 1e-30)
        pb = p.astype(MXU_DTYPE)
        o_cmp.append(_dot(pb, vc))
        imp = imp + _dot(pb, ovl_ref[...])

    jf = lax.broadcasted_iota(jnp.int32, (TQ, d), 1)
    tq_col = t0 + lax.broadcasted_iota(jnp.int32, (TQ, d), 0)
    cur = tq_col >> 6
    score = jnp.where(jf == 0, 1e9, jnp.where(jf == cur, 1e9, jnp.where(jf == cur - 1, 1e9, imp)))
    score = jnp.where(jf * SLC_BLOCK <= tq_col, score, -jnp.inf)
    jff = jf.astype(F32)
    sel = jnp.zeros((TQ, d), F32)
    for _ in range(SLC_TOPN):
        m = jnp.max(score, axis=1, keepdims=True)
        first = jnp.min(jnp.where(score == m, jff, float(d)), axis=1, keepdims=True)
        pick = jff == first
        sel = jnp.where(pick, 1.0, sel)
        score = jnp.where(pick, -jnp.inf, score)
    aug = jnp.where(jf < SLC_BLOCK, jnp.where(sel > 0.5, 0.0, UNSELECTED), jnp.where(jf == SLC_BLOCK, 1.0, 0.0))
    qaug[:, 0:d] = q4
    for r in range(NSA_REP):
        qaug[r * TQ:(r + 1) * TQ, d:2 * d] = aug.astype(qaug.dtype)
    qa = qaug[...]

    n_tiles = (t0 + TQ + KT - 1) // KT
    off = t0 - (n_tiles - 1) * KT
    mx_ref[...] = jnp.full(mx_ref.shape, MASKED, F32)

    def slc_tile(kt, tmpl_start):
        k0 = pl.multiple_of(kt * KT, KT)
        s = _dot_nt(qa, kslc_aug[pl.ds(k0, KT), :]) * scale
        if tmpl_start is not None:
            ts = pl.multiple_of(tmpl_start, 128)
            s = s + wslc_ref[:, :, pl.ds(ts, KT)].reshape(rows, KT)
        lg_ref[:, pl.ds(k0, KT)] = s
        mx_ref[...] = jnp.maximum(mx_ref[...], _lane_max(s))

    def far_body(kt, carry):
        slc_tile(kt, None)
        return carry

    lax.fori_loop(0, jnp.maximum(n_tiles - 2, 0), far_body, 0)

    @pl.when(n_tiles >= 2)
    def _():
        slc_tile(n_tiles - 2, TMPL_C0 - off - KT)

    slc_tile(n_tiles - 1, TMPL_C0 - off)
    o_slc = _softmax_pv(lg_ref, mx_ref, acc_ref, vslc_ref, (0, 0), n_tiles)

    span = NSA_WINDOW + TQ
    w0 = pl.multiple_of(t0, TQ)
    s = _dot_nt(qa, kwin_aug[pl.ds(w0, span), :]) * scale + wwin_ref[...].reshape(rows, span)
    m = jnp.max(s, axis=1, keepdims=True)
    p = jnp.exp(s - m)
    l = jnp.sum(p, axis=1, keepdims=True)
    o_win = _dot(p.astype(MXU_DTYPE), vwin_pad[pl.ds(w0, span), :]) / l

    g = jax.nn.sigmoid(gate_ref[...])
    for r in range(NSA_REP):
        rs = slice(r * TQ, (r + 1) * TQ)
        o = (g[:, 3 * r:3 * r + 1] * o_cmp[r] + g[:, 3 * r + 1:3 * r + 2] * o_slc[rs]
             + g[:, 3 * r + 2:3 * r + 3] * o_win[rs])
        o_ref[:, r * d:(r + 1) * d] = o.astype(o_ref.dtype)


def _nsa(hm, cmp_kv, misc, wslc, wwin, tcmp, ovl):
    b, _, seq, d = hm.shape
    nq = seq // TQ
    rows = NSA_REP * TQ
    ncmp = cmp_kv.shape[3]
    kv_spec = lambda head: pl.BlockSpec((1, 1, seq, d), lambda bi, g, i: (bi, head + g, 0, 0))
    return pl.pallas_call(
        _nsa_kernel,
        grid=(b, NSA_GROUPS, nq),
        in_specs=[pl.BlockSpec((1, NSA_REP, TQ, d), lambda bi, g, i: (bi, g, i, 0)),
                  pl.BlockSpec((1, 1, 1, ncmp, d), lambda bi, g, i: (bi, 0, g, 0, 0)),
                  pl.BlockSpec((1, 1, 1, ncmp, d), lambda bi, g, i: (bi, 1, g, 0, 0)),
                  kv_spec(HM_K_SLC), kv_spec(HM_V_SLC), kv_spec(HM_K_WIN), kv_spec(HM_V_WIN),
                  pl.BlockSpec((TQ, 128), lambda bi, g, i: (bi * nq + i, MISC_GATE + g)),
                  pl.BlockSpec((NSA_REP, TQ, TMPL_W), lambda bi, g, i: (g, 0, 0)),
                  pl.BlockSpec((NSA_REP, TQ, NSA_WINDOW + TQ), lambda bi, g, i: (g, 0, 0)),
                  pl.BlockSpec((NSA_REP, TQ, ncmp), lambda bi, g, i: (g, 0, 0)),
                  pl.BlockSpec(ovl.shape, lambda bi, g, i: (0, 0))],
        out_specs=pl.BlockSpec((TQ, NSA_REP * d), lambda bi, g, i: (bi * nq + i, g)),
        out_shape=jax.ShapeDtypeStruct((b * seq, NSA_HEADS * d), MXU_DTYPE),
        scratch_shapes=[pltpu.VMEM((seq, 2 * d), MXU_DTYPE),
                        pltpu.VMEM((seq + NSA_WINDOW, 2 * d), MXU_DTYPE),
                        pltpu.VMEM((seq + NSA_WINDOW, d), MXU_DTYPE),
                        pltpu.VMEM((rows, 2 * d), MXU_DTYPE),
                        pltpu.VMEM((rows, seq), F32),
                        pltpu.VMEM((rows, 128), F32),
                        pltpu.VMEM((rows, d), F32)],
        compiler_params=_cparams(("parallel", "parallel", "arbitrary")),
    )(hm, cmp_kv, cmp_kv, hm, hm, hm, hm, misc, wslc, wwin, tcmp, ovl)


def _sb_kernel(q_ref, k_ref, v_ref, o_ref, acc_ref, carry_ref):
    i = pl.program_id(2)
    t = SB_T
    scale = HEAD_DIM ** -0.5
    q = q_ref[0, 0]
    upper = jnp.where(lax.broadcasted_iota(jnp.int32, (t, t), 0) > lax.broadcasted_iota(jnp.int32, (t, t), 1),
                      1.0, 0.0).astype(MXU_DTYPE)
    acc_ref[...] = jnp.zeros_like(acc_ref)
    carry_ref[...] = jnp.zeros_like(carry_ref)

    def tile(kt, mask):
        k0 = pl.multiple_of(kt * t, t)
        z = _dot_nt(q, k_ref[0, 0, pl.ds(k0, t), :]) * scale
        ls = jnp.minimum(z, 0.0) - jnp.log(1.0 + jnp.exp(-jnp.abs(z)))
        lk = ls - z
        if mask is not None:
            lk = jnp.where(mask, lk, 0.0)
        hi = lk.astype(MXU_DTYPE)
        r1 = lk - hi.astype(F32)
        mid = r1.astype(MXU_DTYPE)
        lo = (r1 - mid.astype(F32)).astype(MXU_DTYPE)
        after = carry_ref[...] + (_dot(hi, upper) + _dot(mid, upper) + _dot(lo, upper))
        w = jnp.exp(ls + after)
        if mask is not None:
            w = jnp.where(mask, w, 0.0)
        acc_ref[...] += _dot(w.astype(MXU_DTYPE), v_ref[0, 0, pl.ds(k0, t), :])
        carry_ref[...] += jnp.sum(lk, axis=1, keepdims=True)

    strict = lax.broadcasted_iota(jnp.int32, (t, t), 1) < lax.broadcasted_iota(jnp.int32, (t, t), 0)
    tile(i, strict)

    def body(n, carry):
        tile(i - 1 - n, None)
        return carry

    lax.fori_loop(0, i, body, 0)
    o_ref[...] = acc_ref[...].astype(o_ref.dtype)


def _stick_breaking(hm):
    b, _, seq, d = hm.shape
    nq = seq // SB_T
    return pl.pallas_call(
        _sb_kernel,
        grid=(b, SB_HEADS, nq),
        in_specs=[pl.BlockSpec((1, 1, SB_T, d), lambda bi, h, i: (bi, HM_SB_Q + h, i, 0)),
                  pl.BlockSpec((1, 1, seq, d), lambda bi, h, i: (bi, HM_SB_K + h, 0, 0)),
                  pl.BlockSpec((1, 1, seq, d), lambda bi, h, i: (bi, HM_SB_V + h, 0, 0))],
        out_specs=pl.BlockSpec((SB_T, d), lambda bi, h, i: (bi * nq + i, h)),
        out_shape=jax.ShapeDtypeStruct((b * seq, SB_HEADS * d), MXU_DTYPE),
        scratch_shapes=[pltpu.VMEM((SB_T, d), F32), pltpu.VMEM((SB_T, 1), F32)],
        compiler_params=_cparams(("parallel", "parallel", "parallel")),
    )(hm, hm, hm)


def _dsa_kv_kernel(c_ref, g_ref, wk_ref, wv_ref, k_ref, v_ref):
    c = c_ref[...]
    y = c * lax.rsqrt(jnp.mean(c * c, axis=-1, keepdims=True) + EPS)
    y = (y * g_ref[...]).astype(MXU_DTYPE)
    k_ref[...] = _dot(y, wk_ref[...]).astype(k_ref.dtype)
    v_ref[...] = _dot(y, wv_ref[...]).astype(v_ref.dtype)


def _dsa_kv(misc, kv_norm, w_uk, w_uv, tm=512):
    m = misc.shape[0]
    r = DSA_KV_RANK
    out = jax.ShapeDtypeStruct((m, HEAD_DIM), MXU_DTYPE)
    return pl.pallas_call(
        _dsa_kv_kernel,
        grid=(m // tm,),
        in_specs=[pl.BlockSpec((tm, r), lambda i: (i, MISC_CKV * 128 // r)),
                  pl.BlockSpec((1, r), lambda i: (0, 0)),
                  pl.BlockSpec((r, HEAD_DIM), lambda i: (0, 0)),
                  pl.BlockSpec((r, HEAD_DIM), lambda i: (0, 0))],
        out_specs=[pl.BlockSpec((tm, HEAD_DIM), lambda i: (i, 0)), pl.BlockSpec((tm, HEAD_DIM), lambda i: (i, 0))],
        out_shape=[out, out],
        compiler_params=_cparams(("parallel",)),
    )(misc, kv_norm.reshape(1, r), w_uk, w_uv)


def _dsa_kernel(q_ref, k_ref, v_ref, iq_ref, ik_ref, iw_ref, wd_ref, low_ref, o_ref,
                key_ref, add_ref, lg_ref, mx_ref, acc_ref, *, n_keep):
    i = pl.program_id(1)
    d = HEAD_DIM
    rows = DSA_HEADS * TQ
    scale = d ** -0.5
    t0 = i * TQ
    n_tiles = (t0 + TQ + KT - 1) // KT
    off = t0 - (n_tiles - 1) * KT

    iq = iq_ref[...].astype(MXU_DTYPE)
    qh = [iq[:, h * IDX_DIM:(h + 1) * IDX_DIM] for h in range(IDX_HEADS)]
    wi = iw_ref[:, IDX_DIM:IDX_DIM + IDX_HEADS] * (IDX_HEADS ** -0.5) * (IDX_DIM ** -0.5)
    wh = [wi[:, h:h + 1] for h in range(IDX_HEADS)]
    tq_col = t0 + lax.broadcasted_iota(jnp.int32, (TQ, KT), 0)
    kcol = lax.broadcasted_iota(jnp.int32, (TQ, KT), 1)

    def score_tile(kt, causal):
        k0 = pl.multiple_of(kt * KT, KT)
        ki = ik_ref[pl.ds(k0, KT), 0:IDX_DIM].astype(MXU_DTYPE)
        sc = wh[0] * jnp.maximum(_dot_nt(qh[0], ki), 0.0)
        for h in range(1, IDX_HEADS):
            sc = sc + wh[h] * jnp.maximum(_dot_nt(qh[h], ki), 0.0)
        sc = sc + 0.0
        if causal:
            sc = jnp.where(k0 + kcol <= tq_col, sc, -jnp.inf)
        bits = lax.bitcast_convert_type(sc, jnp.int32)
        key_ref[:, pl.ds(k0, KT)] = bits ^ ((bits >> 31) & 0x7FFFFFFF)

    def score_body(kt, carry):
        score_tile(kt, False)
        return carry

    lax.fori_loop(0, n_tiles - 1, score_body, 0)
    score_tile(n_tiles - 1, True)

    def count(pred):
        def body(kt, c):
            k0 = pl.multiple_of(kt * KT, KT)
            hit = jnp.where(pred(key_ref[:, pl.ds(k0, KT)]), 1, 0)
            for cc in range(KT // 128):
                c = c + hit[:, cc * 128:(cc + 1) * 128]
            return c
        c = lax.fori_loop(0, n_tiles, body, jnp.zeros((TQ, 128), jnp.int32))
        return jnp.sum(c, axis=1, keepdims=True)

    thr = jnp.where(count(lambda kk: kk >= 0) >= n_keep, 0, INT_MIN).astype(jnp.int32)

    def bit_body(bi, thr):
        cand = thr | (jnp.int32(1) << (30 - bi))
        return jnp.where(count(lambda kk: kk >= cand) >= n_keep, cand, thr)

    thr = lax.fori_loop(0, 31, bit_body, thr)
    need = n_keep - count(lambda kk: kk > thr)

    def mask_body(kt, seen):
        k0 = pl.multiple_of(kt * KT, KT)
        kk = key_ref[:, pl.ds(k0, KT)]
        eq = kk == thr
        before = seen + _dot(jnp.where(eq, 1.0, 0.0).astype(MXU_DTYPE), low_ref[...])
        tie = jnp.where(before < need.astype(F32), 0.0, MASKED)
        add_ref[:, pl.ds(k0, KT)] = jnp.where(kk > thr, 0.0, jnp.where(eq, tie, MASKED))
        return seen + jnp.sum(jnp.where(eq, 1.0, 0.0), axis=1, keepdims=True)

    lax.fori_loop(0, n_tiles, mask_body, jnp.zeros((TQ, 1), F32))

    q4 = q_ref[0].reshape(rows, d)
    mx_ref[...] = jnp.full(mx_ref.shape, MASKED, F32)

    def att_tile(kt, tmpl_start):
        k0 = pl.multiple_of(kt * KT, KT)
        s = _dot_nt(q4, k_ref[0, pl.ds(k0, KT), :]) * scale
        addm = add_ref[:, pl.ds(k0, KT)]
        s = s + jnp.concatenate([addm] * DSA_HEADS, axis=0)
        if tmpl_start is not None:
            ts = pl.multiple_of(tmpl_start, 128)
            s = s + wd_ref[:, :, pl.ds(ts, KT)].reshape(rows, KT)
        lg_ref[:, pl.ds(k0, KT)] = s
        mx_ref[...] = jnp.maximum(mx_ref[...], _lane_max(s))

    def far_body(kt, carry):
        att_tile(kt, None)
        return carry

    lax.fori_loop(0, jnp.maximum(n_tiles - 2, 0), far_body, 0)

    @pl.when(n_tiles >= 2)
    def _():
        att_tile(n_tiles - 2, TMPL_C0 - off - KT)

    att_tile(n_tiles - 1, TMPL_C0 - off)
    o = _softmax_pv(lg_ref, mx_ref, acc_ref, v_ref, (0,), n_tiles)
    for r in range(DSA_HEADS):
        o_ref[:, r * d:(r + 1) * d] = o[r * TQ:(r + 1) * TQ].astype(o_ref.dtype)


def _dsa(hm, k, v, misc, wdsa, low):
    b, _, seq, d = hm.shape
    nq = seq // TQ
    rows = DSA_HEADS * TQ
    k = k.reshape(b, seq, d)
    v = v.reshape(b, seq, d)
    n_keep = min(DSA_TOPK, seq // 4)
    return pl.pallas_call(
        functools.partial(_dsa_kernel, n_keep=n_keep),
        grid=(b, nq),
        in_specs=[pl.BlockSpec((1, DSA_HEADS, TQ, d), lambda bi, i: (bi, HM_DSA_Q // DSA_HEADS, i, 0)),
                  pl.BlockSpec((1, seq, d), lambda bi, i: (bi, 0, 0)),
                  pl.BlockSpec((1, seq, d), lambda bi, i: (bi, 0, 0)),
                  pl.BlockSpec((TQ, IDX_HEADS * IDX_DIM), lambda bi, i: (bi * nq + i, MISC_IDXQ)),
                  pl.BlockSpec((seq, 128), lambda bi, i: (bi, MISC_IDXK)),
                  pl.BlockSpec((TQ, 128), lambda bi, i: (bi * nq + i, MISC_IDXK)),
                  pl.BlockSpec((DSA_HEADS, TQ, TMPL_W), lambda bi, i: (0, 0, 0)),
                  pl.BlockSpec(low.shape, lambda bi, i: (0, 0))],
        out_specs=pl.BlockSpec((TQ, DSA_HEADS * d), lambda bi, i: (bi * nq + i, 0)),
        out_shape=jax.ShapeDtypeStruct((b * seq, DSA_HEADS * d), MXU_DTYPE),
        scratch_shapes=[pltpu.VMEM((TQ, seq), jnp.int32),
                        pltpu.VMEM((TQ, seq), F32),
                        pltpu.VMEM((rows, seq), F32),
                        pltpu.VMEM((rows, 128), F32),
                        pltpu.VMEM((rows, d), F32)],
        compiler_params=_cparams(("parallel", "parallel")),
    )(hm, k, v, misc, misc, misc, wdsa, low)


def _t5_bucket(dist):
    n = jnp.maximum(dist, 0)
    max_exact = REL_BUCKETS // 2
    nf = jnp.maximum(n, 1).astype(F32)
    large = max_exact + (jnp.log(nf / max_exact) / math.log(REL_MAX_DIST / max_exact)
                         * (REL_BUCKETS - max_exact)).astype(jnp.int32)
    large = jnp.minimum(large, REL_BUCKETS - 1)
    return jnp.where(n < max_exact, n, large)


def _bias_templates(rel_tab, ncmp):
    far = REL_MAX_DIST
    by_dist = rel_tab[_t5_bucket(jnp.arange(far + 1))] - rel_tab[REL_BUCKETS - 1][None, :]
    by_dist = by_dist.T

    def build(dist, valid, fill=MASKED):
        t = by_dist[:, np.clip(dist, 0, far)]
        return jnp.where(valid[None], t, fill).astype(F32)

    rows = np.arange(TQ)[:, None]
    dist = rows - (np.arange(TMPL_W)[None, :] - TMPL_C0)
    causal = build(dist, dist >= 0)
    dist = rows + NSA_WINDOW - np.arange(NSA_WINDOW + TQ)[None, :]
    window = build(dist, (dist >= 0) & (dist < NSA_WINDOW))
    cc = np.arange(ncmp)[None, :]
    cc = np.where(cc < ncmp // 2, cc, cc - ncmp)
    dist = rows - CMP_STRIDE * cc - (CMP_BLOCK - 1)
    cmp = build(dist, dist >= 0, 0.0)[:NSA_HEADS]
    return causal[:NSA_HEADS], causal[NSA_HEADS:], window[:NSA_HEADS], cmp


def _pack_w_in(w_in):
    d3 = 3 * w_in.shape[1]
    kv = NSA_GROUPS * HEAD_DIM
    o_q = d3
    o_kc = o_q + NSA_HEADS * HEAD_DIM
    o_vc, o_ks, o_vs, o_kw, o_vw = (o_kc + j * kv for j in range(1, 6))
    o_g = o_vw + kv
    o_sbq = o_g + 3 * NSA_HEADS
    o_sbk = o_sbq + SB_HEADS * HEAD_DIM
    o_sbv = o_sbk + SB_HEADS * HEAD_DIM
    o_dq = o_sbv + SB_HEADS * HEAD_DIM
    o_ckv = o_dq + DSA_HEADS * HEAD_DIM
    o_iq = o_ckv + DSA_KV_RANK
    o_ik = o_iq + IDX_HEADS * IDX_DIM
    o_iw = o_ik + IDX_DIM
    c = lambda a, n: w_in[:, :, a:a + n]
    zeros = lambda n: jnp.zeros(w_in.shape[:2] + (n,), w_in.dtype)
    w_gates = c(0, d3)
    w_hm = jnp.concatenate([c(o_q, NSA_HEADS * HEAD_DIM), c(o_ks, kv), c(o_vs, kv), c(o_kw, kv), c(o_vw, kv),
                            c(o_sbq, 3 * SB_HEADS * HEAD_DIM), c(o_dq, DSA_HEADS * HEAD_DIM)], axis=2)
    w_cmp = c(o_kc, 2 * kv)
    gw = 3 * NSA_REP
    w_misc = jnp.concatenate([c(o_iq, IDX_HEADS * IDX_DIM), c(o_ckv, DSA_KV_RANK),
                              c(o_ik, IDX_DIM), c(o_iw, IDX_HEADS), zeros(128 - IDX_DIM - IDX_HEADS),
                              c(o_g, gw), zeros(128 - gw), c(o_g + gw, gw), zeros(128 - gw)], axis=2)
    return tuple(w.astype(MXU_DTYPE) for w in (w_gates, w_hm, w_cmp, w_misc))


def kernel(x, p, w_in, norm_mix, norm_ffn, norm_ple, norm_final, w_proj_a, w_proj_b, w_proj_c, w_out,
           cmp_k_w1, cmp_k_w2, cmp_k_pe, cmp_v_w1, cmp_v_w2, cmp_v_pe, dsa_kv_norm, dsa_w_uk, dsa_w_uv,
           rel_bias_table, ffn_w_gate, ffn_w_up, ffn_w_down, ffn_conv_w, ffn_conv_b, ple_w_gate, ple_w_proj):
    batch, seq, d_model = x.shape
    depth = w_in.shape[0]
    m = batch * seq
    assert seq % KT == 0 and seq % SB_T == 0 and seq // SLC_BLOCK <= SLC_BLOCK
    ncmp = seq // CMP_STRIDE
    bf = lambda w: w.astype(MXU_DTYPE)

    w_gates, w_hm, w_cmp, w_misc = _pack_w_in(w_in)
    w_a, w_b, w_c, w_o = bf(w_proj_a), bf(w_proj_b), bf(w_proj_c), bf(w_out)
    cmp_w1 = bf(jnp.stack([cmp_k_w1, cmp_v_w1], axis=1))
    cmp_w2 = bf(jnp.stack([cmp_k_w2, cmp_v_w2], axis=1))
    cmp_pe = jnp.stack([cmp_k_pe, cmp_v_pe], axis=1).reshape(depth, 2, 2, CMP_STRIDE * HEAD_DIM)
    w_uk, w_uv = bf(dsa_w_uk), bf(dsa_w_uv)
    f_gate, f_up, f_down = bf(ffn_w_gate), bf(ffn_w_up), bf(ffn_w_down)
    pl_gate, pl_proj = bf(ple_w_gate), bf(ple_w_proj)

    wslc, wdsa, wwin, tcmp = _bias_templates(rel_bias_table, ncmp)
    cc = np.arange(ncmp)[:, None]
    jj = np.arange(128)[None, :]
    per = SLC_BLOCK // CMP_STRIDE
    ovl = ((cc >= per * jj - (CMP_BLOCK // CMP_STRIDE - 1)) & (cc <= per * jj + per - 1)
           & (cc < ncmp - 1) & (jj < seq // SLC_BLOCK))
    ovl = jnp.asarray(ovl, MXU_DTYPE)
    low = jnp.asarray(np.arange(KT)[:, None] < np.arange(KT)[None, :], MXU_DTYPE)

    x = x.reshape(m, d_model)
    p = p.reshape(depth, m, p.shape[-1])
    for i in range(depth):
        h = _rmsnorm(x, norm_mix[i], MXU_DTYPE)
        gates = _matmul(h, w_gates[i], F32, 512, 512)
        hm = _matmul_heads(h, w_hm[i], batch, MXU_DTYPE)
        cmp_in = _matmul_heads(h, w_cmp[i], batch, F32)
        misc = _matmul(h, w_misc[i], F32, 512, MISC_COLS // 3)
        cmp_kv = _compress(cmp_in, cmp_w1[i], cmp_w2[i], cmp_pe[i])
        o_a = _nsa(hm, cmp_kv, misc, wslc, wwin, tcmp, ovl)
        o_b = _stick_breaking(hm)
        dk, dv = _dsa_kv(misc, dsa_kv_norm[i], w_uk[i], w_uv[i])
        o_c = _dsa(hm, dk, dv, misc, wdsa, low)
        y = _merge(o_a, o_b, o_c, w_a[i], w_b[i], w_c[i], gates)
        x = _matmul_residual(y, w_o[i], x)
        h = _rmsnorm(x, norm_ffn[i], MXU_DTYPE)
        x = _conv_ffn(h, x, f_gate[i], f_up[i], f_down[i], ffn_conv_w[i], ffn_conv_b[i], seq)
        h = _rmsnorm(x, norm_ple[i], MXU_DTYPE)
        x = _ple(h, p[i], x, pl_gate[i], pl_proj[i])
    return _rmsnorm(x, norm_final, F32).reshape(batch, seq, d_model)
```

```python
import functools
import math

import numpy as np
import jax
import jax.numpy as jnp
from jax import lax
from jax.experimental import pallas as pl
from jax.experimental.pallas import tpu as pltpu

F32 = jnp.float32
MXU_DTYPE = jnp.bfloat16

HEAD_DIM = 128
NSA_HEADS = 8
NSA_GROUPS = 2
NSA_REP = NSA_HEADS // NSA_GROUPS
CMP_BLOCK = 32
CMP_STRIDE = 16
SLC_BLOCK = 64
MAX_SLC_BLOCKS = 64
SLC_TOPN = 16
NSA_WINDOW = 512
SB_HEADS = 4
DSA_HEADS = 4
DSA_KV_RANK = 256
IDX_HEADS = 8
IDX_DIM = 64
DSA_TOPK = 256
REL_BUCKETS = 32
REL_MAX_DIST = 128
CONV_WIDTH = 3
EPS = 1e-6

MASKED = -1e30
UNSELECTED = -1e9
INT_MIN = -2 ** 31
VMEM_LIMIT = 56 * 1024 * 1024

TQ = 128
DSA_TQ = 256
KT = 512
SB_T = 256
TMPL_C0 = 896
TMPL_W = TMPL_C0 + KT

HM_NSA_Q, HM_K_SLC, HM_V_SLC, HM_K_WIN, HM_V_WIN = 0, 8, 10, 12, 14
HM_SB_Q, HM_SB_K, HM_SB_V, HM_DSA_Q, HM_HEADS = 16, 20, 24, 28, 32
MISC_IDXQ, MISC_CKV, MISC_IDXK, MISC_GATE, MISC_COLS = 0, 4, 6, 7, 9 * 128


def _cparams(sem):
    return pltpu.CompilerParams(dimension_semantics=sem, vmem_limit_bytes=VMEM_LIMIT)


def _dot(a, b):
    return jnp.dot(a, b, preferred_element_type=F32)


def _dot_nt(a, b):
    return lax.dot_general(a, b, (((1,), (1,)), ((), ())), preferred_element_type=F32)


def _rmsnorm_kernel(x_ref, g_ref, o_ref):
    x = x_ref[...]
    y = x * lax.rsqrt(jnp.mean(x * x, axis=-1, keepdims=True) + EPS)
    o_ref[...] = (y * g_ref[...]).astype(o_ref.dtype)


def _rmsnorm(x, g, out_dtype, tm=512):
    m, d = x.shape
    return pl.pallas_call(
        _rmsnorm_kernel,
        grid=(m // tm,),
        in_specs=[pl.BlockSpec((tm, d), lambda i: (i, 0)), pl.BlockSpec((1, d), lambda i: (0, 0))],
        out_specs=pl.BlockSpec((tm, d), lambda i: (i, 0)),
        out_shape=jax.ShapeDtypeStruct((m, d), out_dtype),
        name="rmsnorm",
        compiler_params=_cparams(("parallel",)),
    )(x, g.reshape(1, d))


def _mm_kernel(a_ref, w_ref, o_ref):
    o_ref[...] = _dot(a_ref[...], w_ref[...]).astype(o_ref.dtype)


def _matmul(a, w, out_dtype, tm, tn, name):
    m, k = a.shape
    n = w.shape[1]
    return pl.pallas_call(
        _mm_kernel,
        name=name,
        grid=(m // tm, n // tn),
        in_specs=[pl.BlockSpec((tm, k), lambda i, j: (i, 0)), pl.BlockSpec((k, tn), lambda i, j: (0, j))],
        out_specs=pl.BlockSpec((tm, tn), lambda i, j: (i, j)),
        out_shape=jax.ShapeDtypeStruct((m, n), out_dtype),
        compiler_params=_cparams(("parallel", "parallel")),
    )(a, w)


def _mm_heads_kernel(a_ref, w_ref, o_ref, *, hb):
    r = _dot(a_ref[...], w_ref[...])
    for j in range(hb):
        o_ref[0, j] = r[:, j * HEAD_DIM:(j + 1) * HEAD_DIM].astype(o_ref.dtype)


def _matmul_heads(a, w, batch, out_dtype, name, tm=512, hb=4):
    m, k = a.shape
    s = m // batch
    nh = w.shape[1] // HEAD_DIM
    spb = s // tm
    return pl.pallas_call(
        functools.partial(_mm_heads_kernel, hb=hb),
        name=name,
        grid=(batch, spb, nh // hb),
        in_specs=[pl.BlockSpec((tm, k), lambda b, i, j: (b * spb + i, 0)),
                  pl.BlockSpec((k, hb * HEAD_DIM), lambda b, i, j: (0, j))],
        out_specs=pl.BlockSpec((1, hb, tm, HEAD_DIM), lambda b, i, j: (b, j, i, 0)),
        out_shape=jax.ShapeDtypeStruct((batch, nh, s, HEAD_DIM), out_dtype),
        compiler_params=_cparams(("parallel", "parallel", "parallel")),
    )(a, w)


def _mm_res_kernel(a_ref, w_ref, x_ref, o_ref):
    o_ref[...] = x_ref[...] + _dot(a_ref[...], w_ref[...])


def _matmul_residual(a, w, x, tm=512, tn=512):
    m, k = a.shape
    n = w.shape[1]
    return pl.pallas_call(
        _mm_res_kernel,
        name="out_proj_residual",
        grid=(m // tm, n // tn),
        in_specs=[pl.BlockSpec((tm, k), lambda i, j: (i, 0)), pl.BlockSpec((k, tn), lambda i, j: (0, j)),
                  pl.BlockSpec((tm, tn), lambda i, j: (i, j))],
        out_specs=pl.BlockSpec((tm, tn), lambda i, j: (i, j)),
        out_shape=jax.ShapeDtypeStruct((m, n), F32),
        compiler_params=_cparams(("parallel", "parallel")),
    )(a, w, x)


def _merge_kernel(oa_ref, ob_ref, oc_ref, wa_ref, wb_ref, wc_ref, ga_ref, gb_ref, gc_ref, y_ref):
    y = jax.nn.sigmoid(ga_ref[...]) * _dot(oa_ref[...], wa_ref[...])
    y += jax.nn.sigmoid(gb_ref[...]) * _dot(ob_ref[...], wb_ref[...])
    y += jax.nn.sigmoid(gc_ref[...]) * _dot(oc_ref[...], wc_ref[...])
    y_ref[...] = y.astype(y_ref.dtype)


def _merge(o_a, o_b, o_c, w_a, w_b, w_c, gates, tm=512, tn=512):
    m = o_a.shape[0]
    d = w_a.shape[1]
    nb = d // tn
    row = lambda i, j: (i, 0)
    col = lambda i, j: (0, j)
    return pl.pallas_call(
        _merge_kernel,
        name="branch_merge",
        grid=(m // tm, nb),
        in_specs=[pl.BlockSpec((tm, o_a.shape[1]), row), pl.BlockSpec((tm, o_b.shape[1]), row),
                  pl.BlockSpec((tm, o_c.shape[1]), row),
                  pl.BlockSpec((w_a.shape[0], tn), col), pl.BlockSpec((w_b.shape[0], tn), col),
                  pl.BlockSpec((w_c.shape[0], tn), col),
                  pl.BlockSpec((tm, tn), lambda i, j: (i, j)),
                  pl.BlockSpec((tm, tn), lambda i, j: (i, nb + j)),
                  pl.BlockSpec((tm, tn), lambda i, j: (i, 2 * nb + j))],
        out_specs=pl.BlockSpec((tm, tn), lambda i, j: (i, j)),
        out_shape=jax.ShapeDtypeStruct((m, d), MXU_DTYPE),
        compiler_params=_cparams(("parallel", "parallel")),
    )(o_a, o_b, o_c, w_a, w_b, w_c, gates, gates, gates)


FFN_HALO = 16


def _ffn_kernel(h_ref, hp_ref, x_ref, wg_ref, wu_ref, wd_ref, cw_ref, cb_ref, o_ref, hext_ref, *, tiles_per_seq):
    i = pl.program_id(0)
    f = pl.program_id(1)
    tm = h_ref.shape[0]

    @pl.when(f == 0)
    def _():
        first = (i % tiles_per_seq) == 0
        hext_ref[0:FFN_HALO, :] = jnp.where(first, jnp.zeros_like(hp_ref[...]), hp_ref[...])
        hext_ref[FFN_HALO:, :] = h_ref[...]
        o_ref[...] = x_ref[...]

    a = _dot(hext_ref[...], wg_ref[...])
    cw = cw_ref[...]
    c = (cw[0:1] * a[FFN_HALO - 2:FFN_HALO - 2 + tm] + cw[1:2] * a[FFN_HALO - 1:FFN_HALO - 1 + tm]
         + cw[2:3] * a[FFN_HALO:FFN_HALO + tm]) + cb_ref[...]
    u = _dot(h_ref[...], wu_ref[...])
    act = (jax.nn.gelu(c) * u).astype(MXU_DTYPE)
    o_ref[...] += _dot(act, wd_ref[...])


def _conv_ffn(h, x, w_gate, w_up, w_down, conv_w, conv_b, seq, tm=512, tf=512):
    m, d = h.shape
    ff = w_gate.shape[1]
    hb = tm // FFN_HALO
    return pl.pallas_call(
        functools.partial(_ffn_kernel, tiles_per_seq=seq // tm),
        name="conv_ffn",
        grid=(m // tm, ff // tf),
        in_specs=[pl.BlockSpec((tm, d), lambda i, f: (i, 0)),
                  pl.BlockSpec((FFN_HALO, d), lambda i, f: (jnp.maximum(i * hb - 1, 0), 0)),
                  pl.BlockSpec((tm, d), lambda i, f: (i, 0)),
                  pl.BlockSpec((d, tf), lambda i, f: (0, f)),
                  pl.BlockSpec((d, tf), lambda i, f: (0, f)),
                  pl.BlockSpec((tf, d), lambda i, f: (f, 0)),
                  pl.BlockSpec((CONV_WIDTH, tf), lambda i, f: (0, f)),
                  pl.BlockSpec((1, tf), lambda i, f: (0, f))],
        out_specs=pl.BlockSpec((tm, d), lambda i, f: (i, 0)),
        out_shape=jax.ShapeDtypeStruct((m, d), F32),
        scratch_shapes=[pltpu.VMEM((tm + FFN_HALO, d), MXU_DTYPE)],
        compiler_params=_cparams(("parallel", "arbitrary")),
    )(h, h, x, w_gate, w_up, w_down, conv_w, conv_b.reshape(1, ff))


def _ple_kernel(h_ref, p_ref, x_ref, wg_ref, wp_ref, o_ref):
    gate = jax.nn.sigmoid(_dot(h_ref[...], wg_ref[...]))
    o_ref[...] = x_ref[...] + gate * _dot(p_ref[...].astype(MXU_DTYPE), wp_ref[...])


def _ple(h, p, x, w_gate, w_proj, tm=512, tn=512):
    m, d = h.shape
    return pl.pallas_call(
        _ple_kernel,
        name="ple",
        grid=(m // tm, d // tn),
        in_specs=[pl.BlockSpec((tm, d), lambda i, j: (i, 0)),
                  pl.BlockSpec((tm, p.shape[1]), lambda i, j: (i, 0)),
                  pl.BlockSpec((tm, tn), lambda i, j: (i, j)),
                  pl.BlockSpec((d, tn), lambda i, j: (0, j)),
                  pl.BlockSpec((p.shape[1], tn), lambda i, j: (0, j))],
        out_specs=pl.BlockSpec((tm, tn), lambda i, j: (i, j)),
        out_shape=jax.ShapeDtypeStruct((m, d), F32),
        compiler_params=_cparams(("parallel", "parallel")),
    )(h, p, x, w_gate, w_proj)


def _compress_kernel(x_ref, w1_ref, w2_ref, pe_ref, o_ref):
    x = x_ref[0, 0]
    pe = pe_ref[0]
    half = x.shape[1]
    lo = _dot((x + pe[0:1]).astype(MXU_DTYPE), w1_ref[0, :half, :])
    hi = _dot((x + pe[1:2]).astype(MXU_DTYPE), w1_ref[0, half:, :])
    n = x.shape[0]
    hid = lo + pltpu.roll(hi, n - 1, 0)
    o_ref[0, 0, 0] = _dot(jax.nn.gelu(hid).astype(MXU_DTYPE), w2_ref[0]).astype(o_ref.dtype)


def _compress(kv_hm, w1, w2, pe):
    b, _, s, d = kv_hm.shape
    nrow = s // CMP_STRIDE
    x = kv_hm.reshape(b, 2 * NSA_GROUPS, nrow, CMP_STRIDE * d)
    return pl.pallas_call(
        _compress_kernel,
        name="nsa_compress",
        grid=(b, 2, NSA_GROUPS),
        in_specs=[pl.BlockSpec((1, 1, nrow, CMP_STRIDE * d), lambda bi, kv, g: (bi, kv * NSA_GROUPS + g, 0, 0)),
                  pl.BlockSpec((1,) + w1.shape[1:], lambda bi, kv, g: (kv, 0, 0)),
                  pl.BlockSpec((1,) + w2.shape[1:], lambda bi, kv, g: (kv, 0, 0)),
                  pl.BlockSpec((1, 2, CMP_STRIDE * d), lambda bi, kv, g: (kv, 0, 0))],
        out_specs=pl.BlockSpec((1, 1, 1, nrow, d), lambda bi, kv, g: (bi, kv, g, 0, 0)),
        out_shape=jax.ShapeDtypeStruct((b, 2, NSA_GROUPS, nrow, d), MXU_DTYPE),
        compiler_params=_cparams(("parallel", "parallel", "parallel")),
    )(x, w1, w2, pe)


def _softmax_pv(lg_ref, mx_ref, acc_ref, v_ref, v_index, n_tiles):
    rows = lg_ref.shape[0]
    m = jnp.max(mx_ref[...], axis=1, keepdims=True)
    mx_ref[...] = jnp.zeros_like(mx_ref)
    acc_ref[...] = jnp.zeros_like(acc_ref)

    def body(kt, carry):
        k0 = pl.multiple_of(kt * KT, KT)
        p = jnp.exp(lg_ref[:, pl.ds(k0, KT)] - m)
        part = p[:, 0:128]
        for c in range(1, KT // 128):
            part = part + p[:, c * 128:(c + 1) * 128]
        mx_ref[...] += part
        acc_ref[...] += _dot(p.astype(MXU_DTYPE), v_ref[v_index + (pl.ds(k0, KT), slice(None))])
        return carry

    lax.fori_loop(0, n_tiles, body, 0)
    l = jnp.sum(mx_ref[...], axis=1, keepdims=True)
    return acc_ref[...] / l


def _lane_max(s):
    part = s[:, 0:128]
    for c in range(1, s.shape[1] // 128):
        part = jnp.maximum(part, s[:, c * 128:(c + 1) * 128])
    return part


def _nsa_kernel(q_ref, kc_ref, vc_ref, kslc_ref, vslc_ref, kwin_ref, vwin_ref, gate_ref,
                wslc_ref, wwin_ref, tcmp_ref, ovl_ref, o_ref,
                kslc_aug, kwin_aug, vwin_pad, qaug, lg_ref, mx_ref, acc_ref, score_ref):
    i = pl.program_id(2)
    seq = kslc_ref.shape[2]
    d = HEAD_DIM
    rows = NSA_REP * TQ
    scale = d ** -0.5
    t0 = i * TQ

    @pl.when(i == 0)
    def _():
        kslc_aug[:, 0:d] = kslc_ref[0, 0]
        srow = lax.broadcasted_iota(jnp.int32, (seq, d), 0)
        lane = lax.broadcasted_iota(jnp.int32, (seq, d), 1)
        kslc_aug[:, d:2 * d] = jnp.where((srow >> 6) == lane, 1.0, 0.0).astype(kslc_aug.dtype)
        kwin_aug[0:NSA_WINDOW, 0:d] = jnp.zeros((NSA_WINDOW, d), kwin_aug.dtype)
        kwin_aug[NSA_WINDOW:, 0:d] = kwin_ref[0, 0]
        prow = lax.broadcasted_iota(jnp.int32, (seq + NSA_WINDOW, d), 0)
        plane = lax.broadcasted_iota(jnp.int32, (seq + NSA_WINDOW, d), 1)
        flag = jnp.where(prow < NSA_WINDOW, jnp.where(plane == MAX_SLC_BLOCKS, UNSELECTED, 0.0), 0.0)
        kwin_aug[:, d:2 * d] = flag.astype(kwin_aug.dtype)
        vwin_pad[0:NSA_WINDOW, :] = jnp.zeros((NSA_WINDOW, d), vwin_pad.dtype)
        vwin_pad[NSA_WINDOW:, :] = vwin_ref[0, 0]

    q4 = q_ref[0].reshape(rows, d)

    ncmp = kc_ref.shape[3]
    kc = kc_ref[0, 0, 0]
    vc = vc_ref[0, 0, 0]
    trow = t0 + lax.broadcasted_iota(jnp.int32, (TQ, ncmp), 0)
    cend = lax.broadcasted_iota(jnp.int32, (TQ, ncmp), 1) * CMP_STRIDE + (CMP_BLOCK - 1)
    valid_c = cend <= trow
    sc_all = _dot_nt(q4, kc) * scale
    c0 = i * (TQ // CMP_STRIDE)
    imp_t = jnp.zeros((d, TQ), F32)
    o_cmp = []
    for r in range(NSA_REP):
        bias = pltpu.roll(tcmp_ref[r], c0, 1)
        l = jnp.where(valid_c, sc_all[r * TQ:(r + 1) * TQ] + bias, MASKED)
        m = jnp.max(l, axis=1, keepdims=True)
        e = jnp.where(valid_c, jnp.exp(l - m), 0.0)
        p = e / jnp.maximum(jnp.sum(e, axis=1, keepdims=True), 1e-30)
        pb = p.astype(MXU_DTYPE)
        o_cmp.append(_dot(pb, vc))
        imp_t = imp_t + _dot_nt(ovl_ref[...], pb)

    nblk = MAX_SLC_BLOCKS
    jt = lax.broadcasted_iota(jnp.int32, (nblk, TQ), 0)
    tt = t0 + lax.broadcasted_iota(jnp.int32, (nblk, TQ), 1)
    cur = tt >> 6
    imp_t = imp_t[0:nblk]
    score = jnp.where(jt == 0, 1e9, jnp.where(jt == cur, 1e9, jnp.where(jt == cur - 1, 1e9, imp_t)))
    score = jnp.where(jt * SLC_BLOCK <= tt, score, -jnp.inf)
    score_ref[...] = score
    sub = lax.broadcasted_iota(jnp.int32, (8, TQ), 0)
    sv = [score[8 * v:8 * v + 8] for v in range(nblk // 8)]
    beaten = [jnp.zeros((8, TQ), F32) for _ in sv]
    for jp in range(nblk):
        other = score_ref[jp:jp + 1, :]
        for v in range(nblk // 8):
            if 8 * v > jp:
                hit = other >= sv[v]
            elif 8 * v + 7 <= jp:
                hit = other > sv[v]
            else:
                tie_loses = jnp.where(sub > jp - 8 * v, 1.0, 0.0)
                beaten[v] = beaten[v] + jnp.where(other == sv[v], tie_loses, 0.0)
                hit = other > sv[v]
            beaten[v] = beaten[v] + jnp.where(hit, 1.0, 0.0)
    aug_t = jnp.concatenate([jnp.where(b < SLC_TOPN, 0.0, UNSELECTED) for b in beaten], axis=0)
    row = lax.broadcasted_iota(jnp.int32, (d - nblk, TQ), 0)
    aug_t = jnp.concatenate([aug_t, jnp.where(row == 0, 1.0, 0.0)], axis=0)
    aug = aug_t.T
    qaug[:, 0:d] = q4
    for r in range(NSA_REP):
        qaug[r * TQ:(r + 1) * TQ, d:2 * d] = aug.astype(qaug.dtype)
    qa = qaug[...]

    n_tiles = (t0 + TQ + KT - 1) // KT
    off = t0 - (n_tiles - 1) * KT
    mx_ref[...] = jnp.full(mx_ref.shape, MASKED, F32)

    def slc_tile(kt, tmpl_start):
        k0 = pl.multiple_of(kt * KT, KT)
        s = _dot_nt(qa, kslc_aug[pl.ds(k0, KT), :]) * scale
        if tmpl_start is not None:
            ts = pl.multiple_of(tmpl_start, 128)
            s = s + wslc_ref[:, :, pl.ds(ts, KT)].reshape(rows, KT)
        lg_ref[:, pl.ds(k0, KT)] = s
        mx_ref[...] = jnp.maximum(mx_ref[...], _lane_max(s))

    def far_body(kt, carry):
        slc_tile(kt, None)
        return carry

    lax.fori_loop(0, jnp.maximum(n_tiles - 2, 0), far_body, 0)

    @pl.when(n_tiles >= 2)
    def _():
        slc_tile(n_tiles - 2, TMPL_C0 - off - KT)

    slc_tile(n_tiles - 1, TMPL_C0 - off)
    o_slc = _softmax_pv(lg_ref, mx_ref, acc_ref, vslc_ref, (0, 0), n_tiles)

    span = NSA_WINDOW + TQ
    w0 = pl.multiple_of(t0, TQ)
    s = _dot_nt(qa, kwin_aug[pl.ds(w0, span), :]) * scale + wwin_ref[...].reshape(rows, span)
    m = jnp.max(s, axis=1, keepdims=True)
    p = jnp.exp(s - m)
    l = jnp.sum(p, axis=1, keepdims=True)
    o_win = _dot(p.astype(MXU_DTYPE), vwin_pad[pl.ds(w0, span), :]) / l

    g = jax.nn.sigmoid(gate_ref[...])
    for r in range(NSA_REP):
        rs = slice(r * TQ, (r + 1) * TQ)
        o = (g[:, 3 * r:3 * r + 1] * o_cmp[r] + g[:, 3 * r + 1:3 * r + 2] * o_slc[rs]
             + g[:, 3 * r + 2:3 * r + 3] * o_win[rs])
        o_ref[:, r * d:(r + 1) * d] = o.astype(o_ref.dtype)


def _nsa(hm, cmp_kv, misc, wslc, wwin, tcmp, ovl):
    b, _, seq, d = hm.shape
    nq = seq // TQ
    rows = NSA_REP * TQ
    ncmp = cmp_kv.shape[3]
    kv_spec = lambda head: pl.BlockSpec((1, 1, seq, d), lambda bi, g, i: (bi, head + g, 0, 0))
    return pl.pallas_call(
        _nsa_kernel,
        name="nsa",
        grid=(b, NSA_GROUPS, nq),
        in_specs=[pl.BlockSpec((1, NSA_REP, TQ, d), lambda bi, g, i: (bi, g, i, 0)),
                  pl.BlockSpec((1, 1, 1, ncmp, d), lambda bi, g, i: (bi, 0, g, 0, 0)),
                  pl.BlockSpec((1, 1, 1, ncmp, d), lambda bi, g, i: (bi, 1, g, 0, 0)),
                  kv_spec(HM_K_SLC), kv_spec(HM_V_SLC), kv_spec(HM_K_WIN), kv_spec(HM_V_WIN),
                  pl.BlockSpec((TQ, 128), lambda bi, g, i: (bi * nq + i, MISC_GATE + g)),
                  pl.BlockSpec((NSA_REP, TQ, TMPL_W), lambda bi, g, i: (g, 0, 0)),
                  pl.BlockSpec((NSA_REP, TQ, NSA_WINDOW + TQ), lambda bi, g, i: (g, 0, 0)),
                  pl.BlockSpec((NSA_REP, TQ, ncmp), lambda bi, g, i: (g, 0, 0)),
                  pl.BlockSpec(ovl.shape, lambda bi, g, i: (0, 0))],
        out_specs=pl.BlockSpec((TQ, NSA_REP * d), lambda bi, g, i: (bi * nq + i, g)),
        out_shape=jax.ShapeDtypeStruct((b * seq, NSA_HEADS * d), MXU_DTYPE),
        scratch_shapes=[pltpu.VMEM((seq, 2 * d), MXU_DTYPE),
                        pltpu.VMEM((seq + NSA_WINDOW, 2 * d), MXU_DTYPE),
                        pltpu.VMEM((seq + NSA_WINDOW, d), MXU_DTYPE),
                        pltpu.VMEM((rows, 2 * d), MXU_DTYPE),
                        pltpu.VMEM((rows, seq), F32),
                        pltpu.VMEM((rows, 128), F32),
                        pltpu.VMEM((rows, d), F32),
                        pltpu.VMEM((MAX_SLC_BLOCKS, TQ), F32)],
        compiler_params=_cparams(("parallel", "parallel", "arbitrary")),
    )(hm, cmp_kv, cmp_kv, hm, hm, hm, hm, misc, wslc, wwin, tcmp, ovl)


def _sb_kernel(q_ref, k_ref, v_ref, upper_ref, o_ref, acc_ref, carry_ref):
    i = pl.program_id(1)
    t = SB_T
    d = HEAD_DIM
    scale = d ** -0.5
    upper = upper_ref[...]
    acc_ref[...] = jnp.zeros_like(acc_ref)
    carry_ref[...] = jnp.zeros_like(carry_ref)

    heads = range(SB_HEADS)

    def tile(kt, mask):
        k0 = pl.multiple_of(kt * t, t)
        z = [_dot_nt(q_ref[0, h], k_ref[0, h, pl.ds(k0, t), :]) * scale for h in heads]
        ls = [jnp.minimum(z[h], 0.0) - jnp.log(1.0 + jnp.exp(-jnp.abs(z[h]))) for h in heads]
        lk = [ls[h] - z[h] for h in heads]
        if mask is not None:
            lk = [jnp.where(mask, lk[h], 0.0) for h in heads]
        hi = [lk[h].astype(MXU_DTYPE) for h in heads]
        r1 = [lk[h] - hi[h].astype(F32) for h in heads]
        mid = [r1[h].astype(MXU_DTYPE) for h in heads]
        lo = [(r1[h] - mid[h].astype(F32)).astype(MXU_DTYPE) for h in heads]
        after = [carry_ref[h] + (_dot(hi[h], upper) + _dot(mid[h], upper) + _dot(lo[h], upper)) for h in heads]
        w = [jnp.exp(ls[h] + after[h]) for h in heads]
        if mask is not None:
            w = [jnp.where(mask, w[h], 0.0) for h in heads]
        for h in heads:
            acc_ref[h] += _dot(w[h].astype(MXU_DTYPE), v_ref[0, h, pl.ds(k0, t), :])
            carry_ref[h] += jnp.sum(lk[h], axis=1, keepdims=True)

    strict = lax.broadcasted_iota(jnp.int32, (t, t), 1) < lax.broadcasted_iota(jnp.int32, (t, t), 0)
    tile(i, strict)

    def body(n, carry):
        tile(i - 1 - n, None)
        return carry

    lax.fori_loop(0, i, body, 0)
    for h in range(SB_HEADS):
        o_ref[:, h * d:(h + 1) * d] = acc_ref[h].astype(o_ref.dtype)


def _stick_breaking(hm, upper):
    b, _, seq, d = hm.shape
    nq = seq // SB_T
    return pl.pallas_call(
        _sb_kernel,
        grid=(b, nq),
        in_specs=[pl.BlockSpec((1, SB_HEADS, SB_T, d), lambda bi, i: (bi, HM_SB_Q // SB_HEADS, i, 0)),
                  pl.BlockSpec((1, SB_HEADS, seq, d), lambda bi, i: (bi, HM_SB_K // SB_HEADS, 0, 0)),
                  pl.BlockSpec((1, SB_HEADS, seq, d), lambda bi, i: (bi, HM_SB_V // SB_HEADS, 0, 0)),
                  pl.BlockSpec(upper.shape, lambda bi, i: (0, 0))],
        out_specs=pl.BlockSpec((SB_T, SB_HEADS * d), lambda bi, i: (bi * nq + i, 0)),
        out_shape=jax.ShapeDtypeStruct((b * seq, SB_HEADS * d), MXU_DTYPE),
        scratch_shapes=[pltpu.VMEM((SB_HEADS, SB_T, d), F32), pltpu.VMEM((SB_HEADS, SB_T, 1), F32)],
        name="stick_breaking",
        compiler_params=_cparams(("parallel", "parallel")),
    )(hm, hm, hm, upper)


def _dsa_kv_kernel(c_ref, g_ref, wk_ref, wv_ref, k_ref, v_ref):
    c = c_ref[...]
    y = c * lax.rsqrt(jnp.mean(c * c, axis=-1, keepdims=True) + EPS)
    y = (y * g_ref[...]).astype(MXU_DTYPE)
    k_ref[...] = _dot(y, wk_ref[...]).astype(k_ref.dtype)
    v_ref[...] = _dot(y, wv_ref[...]).astype(v_ref.dtype)


def _dsa_kv(misc, kv_norm, w_uk, w_uv, tm=512):
    m = misc.shape[0]
    r = DSA_KV_RANK
    out = jax.ShapeDtypeStruct((m, HEAD_DIM), MXU_DTYPE)
    return pl.pallas_call(
        _dsa_kv_kernel,
        name="dsa_kv",
        grid=(m // tm,),
        in_specs=[pl.BlockSpec((tm, r), lambda i: (i, MISC_CKV * 128 // r)),
                  pl.BlockSpec((1, r), lambda i: (0, 0)),
                  pl.BlockSpec((r, HEAD_DIM), lambda i: (0, 0)),
                  pl.BlockSpec((r, HEAD_DIM), lambda i: (0, 0))],
        out_specs=[pl.BlockSpec((tm, HEAD_DIM), lambda i: (i, 0)), pl.BlockSpec((tm, HEAD_DIM), lambda i: (i, 0))],
        out_shape=[out, out],
        compiler_params=_cparams(("parallel",)),
    )(misc, kv_norm.reshape(1, r), w_uk, w_uv)


def _dsa_kernel(q_ref, k_ref, v_ref, iq_ref, ik_ref, iw_ref, wd_ref, low_ref, o_ref,
                key_ref, add_ref, lg_ref, mx_ref, acc_ref, *, n_keep):
    i = pl.program_id(1)
    d = HEAD_DIM
    tq = DSA_TQ
    rows = DSA_HEADS * tq
    scale = d ** -0.5
    t0 = i * tq
    n_tiles = (t0 + tq + KT - 1) // KT
    off = t0 - (n_tiles - 1) * KT

    iq = iq_ref[...].astype(MXU_DTYPE)
    qh = [iq[:, h * IDX_DIM:(h + 1) * IDX_DIM] for h in range(IDX_HEADS)]
    wi_t = iw_ref[...].T * (IDX_HEADS ** -0.5) * (IDX_DIM ** -0.5)
    wh = [wi_t[IDX_DIM + h:IDX_DIM + h + 1, :] for h in range(IDX_HEADS)]
    tq_row = t0 + lax.broadcasted_iota(jnp.int32, (KT, tq), 1)
    krow = lax.broadcasted_iota(jnp.int32, (KT, tq), 0)

    def column_sum(a):
        return jnp.sum(a.reshape(KT // 8, 8, tq), axis=0)

    def score_tile(kt, causal):
        k0 = pl.multiple_of(kt * KT, KT)
        ki = ik_ref[pl.ds(k0, KT), 0:IDX_DIM].astype(MXU_DTYPE)
        dots = [_dot_nt(ki, qh[h]) for h in range(IDX_HEADS)]
        sc = wh[0] * jnp.maximum(dots[0], 0.0)
        for h in range(1, IDX_HEADS):
            sc = sc + wh[h] * jnp.maximum(dots[h], 0.0)
        sc = sc + 0.0
        if causal:
            sc = jnp.where(k0 + krow <= tq_row, sc, -jnp.inf)
        bits = lax.bitcast_convert_type(sc, jnp.int32)
        key_ref[pl.ds(k0, KT), :] = bits ^ ((bits >> 31) & 0x7FFFFFFF)

    def score_body(kt, carry):
        score_tile(kt, False)
        return carry

    lax.fori_loop(0, n_tiles - 1, score_body, 0)
    score_tile(n_tiles - 1, True)

    def count(pred):
        def body(kt, c):
            k0 = pl.multiple_of(kt * KT, KT)
            return c + column_sum(jnp.where(pred(key_ref[pl.ds(k0, KT), :]), 1, 0))
        c = lax.fori_loop(0, n_tiles, body, jnp.zeros((8, tq), jnp.int32))
        return jnp.sum(c, axis=0, keepdims=True)

    thr = jnp.where(count(lambda kk: kk >= 0) >= n_keep, 0, INT_MIN).astype(jnp.int32)

    def bit_body(bi, thr):
        cand = thr | (jnp.int32(1) << (30 - bi))
        return jnp.where(count(lambda kk: kk >= cand) >= n_keep, cand, thr)

    thr = lax.fori_loop(0, 31, bit_body, thr)
    need = (n_keep - count(lambda kk: kk > thr)).astype(F32)

    def mask_body(kt, seen):
        k0 = pl.multiple_of(kt * KT, KT)
        kk = key_ref[pl.ds(k0, KT), :]
        eq = jnp.where(kk == thr, 1.0, 0.0)
        before = seen + _dot(low_ref[...], eq.astype(MXU_DTYPE))
        tie = jnp.where(before < need, 0.0, MASKED)
        add_t = jnp.where(kk > thr, 0.0, jnp.where(kk == thr, tie, MASKED))
        add_ref[:, pl.ds(k0, KT)] = add_t.T
        return seen + jnp.sum(column_sum(eq), axis=0, keepdims=True)

    lax.fori_loop(0, n_tiles, mask_body, jnp.zeros((1, tq), F32))

    q4 = q_ref[0].reshape(rows, d)
    mx_ref[...] = jnp.full(mx_ref.shape, MASKED, F32)

    def att_tile(kt, tmpl_start):
        k0 = pl.multiple_of(kt * KT, KT)
        s = _dot_nt(q4, k_ref[0, pl.ds(k0, KT), :]) * scale
        addm = add_ref[:, pl.ds(k0, KT)]
        s = s + jnp.concatenate([addm] * DSA_HEADS, axis=0)
        if tmpl_start is not None:
            ts = pl.multiple_of(tmpl_start, 128)
            s = s + wd_ref[:, :, pl.ds(ts, KT)].reshape(rows, KT)
        lg_ref[:, pl.ds(k0, KT)] = s
        mx_ref[...] = jnp.maximum(mx_ref[...], _lane_max(s))

    def far_body(kt, carry):
        att_tile(kt, None)
        return carry

    lax.fori_loop(0, jnp.maximum(n_tiles - 2, 0), far_body, 0)

    @pl.when(n_tiles >= 2)
    def _():
        att_tile(n_tiles - 2, TMPL_C0 - off - KT)

    att_tile(n_tiles - 1, TMPL_C0 - off)
    o = _softmax_pv(lg_ref, mx_ref, acc_ref, v_ref, (0,), n_tiles)
    for r in range(DSA_HEADS):
        o_ref[:, r * d:(r + 1) * d] = o[r * tq:(r + 1) * tq].astype(o_ref.dtype)


def _dsa(hm, k, v, misc, wdsa, low):
    b, _, seq, d = hm.shape
    tq = DSA_TQ
    nq = seq // tq
    rows = DSA_HEADS * tq
    k = k.reshape(b, seq, d)
    v = v.reshape(b, seq, d)
    n_keep = min(DSA_TOPK, seq // 4)
    return pl.pallas_call(
        functools.partial(_dsa_kernel, n_keep=n_keep),
        name="dsa",
        grid=(b, nq),
        in_specs=[pl.BlockSpec((1, DSA_HEADS, tq, d), lambda bi, i: (bi, HM_DSA_Q // DSA_HEADS, i, 0)),
                  pl.BlockSpec((1, seq, d), lambda bi, i: (bi, 0, 0)),
                  pl.BlockSpec((1, seq, d), lambda bi, i: (bi, 0, 0)),
                  pl.BlockSpec((tq, IDX_HEADS * IDX_DIM), lambda bi, i: (bi * nq + i, MISC_IDXQ)),
                  pl.BlockSpec((seq, 128), lambda bi, i: (bi, MISC_IDXK)),
                  pl.BlockSpec((tq, 128), lambda bi, i: (bi * nq + i, MISC_IDXK)),
                  pl.BlockSpec((DSA_HEADS, tq, TMPL_W), lambda bi, i: (0, 0, 0)),
                  pl.BlockSpec(low.shape, lambda bi, i: (0, 0))],
        out_specs=pl.BlockSpec((tq, DSA_HEADS * d), lambda bi, i: (bi * nq + i, 0)),
        out_shape=jax.ShapeDtypeStruct((b * seq, DSA_HEADS * d), MXU_DTYPE),
        scratch_shapes=[pltpu.VMEM((seq, tq), jnp.int32),
                        pltpu.VMEM((tq, seq), F32),
                        pltpu.VMEM((rows, seq), F32),
                        pltpu.VMEM((rows, 128), F32),
                        pltpu.VMEM((rows, d), F32)],
        compiler_params=_cparams(("parallel", "parallel")),
    )(hm, k, v, misc, misc, misc, wdsa, low)


def _t5_bucket(dist):
    n = jnp.maximum(dist, 0)
    max_exact = REL_BUCKETS // 2
    nf = jnp.maximum(n, 1).astype(F32)
    large = max_exact + (jnp.log(nf / max_exact) / math.log(REL_MAX_DIST / max_exact)
                         * (REL_BUCKETS - max_exact)).astype(jnp.int32)
    large = jnp.minimum(large, REL_BUCKETS - 1)
    return jnp.where(n < max_exact, n, large)


def _bias_templates(rel_tab, ncmp):
    far = REL_MAX_DIST
    by_dist = rel_tab[_t5_bucket(jnp.arange(far + 1))] - rel_tab[REL_BUCKETS - 1][None, :]
    by_dist = by_dist.T

    def build(dist, valid, fill=MASKED):
        t = by_dist[:, np.clip(dist, 0, far)]
        return jnp.where(valid[None], t, fill).astype(F32)

    def toeplitz(u, nrows, width):
        nh, l = u.shape
        return jnp.tile(u, (1, nrows))[:, :nrows * (l - 1)].reshape(nh, nrows, l - 1)[:, :, :width]

    def diagonals(nrows, width):
        l = width + nrows
        k = np.arange(l)
        return np.where(k < width, k, k - l)

    def causal(heads, nrows):
        dist = TMPL_C0 - diagonals(nrows, TMPL_W)
        return toeplitz(build(dist, dist >= 0)[heads], nrows, TMPL_W)

    nsa = slice(0, NSA_HEADS)
    span = NSA_WINDOW + TQ
    dist = NSA_WINDOW - diagonals(TQ, span)
    window = toeplitz(build(dist, (dist >= 0) & (dist < NSA_WINDOW))[nsa], TQ, span)
    cc = np.arange(ncmp)[None, :]
    cc = np.where(cc < ncmp // 2, cc, cc - ncmp)
    dist = np.arange(TQ)[:, None] - CMP_STRIDE * cc - (CMP_BLOCK - 1)
    cmp = build(dist, dist >= 0, 0.0)[nsa]
    return causal(nsa, TQ), causal(slice(NSA_HEADS, None), DSA_TQ), window, cmp


def _pack_w_in(w_in):
    d3 = 3 * w_in.shape[1]
    kv = NSA_GROUPS * HEAD_DIM
    o_q = d3
    o_kc = o_q + NSA_HEADS * HEAD_DIM
    o_vc, o_ks, o_vs, o_kw, o_vw = (o_kc + j * kv for j in range(1, 6))
    o_g = o_vw + kv
    o_sbq = o_g + 3 * NSA_HEADS
    o_sbk = o_sbq + SB_HEADS * HEAD_DIM
    o_sbv = o_sbk + SB_HEADS * HEAD_DIM
    o_dq = o_sbv + SB_HEADS * HEAD_DIM
    o_ckv = o_dq + DSA_HEADS * HEAD_DIM
    o_iq = o_ckv + DSA_KV_RANK
    o_ik = o_iq + IDX_HEADS * IDX_DIM
    o_iw = o_ik + IDX_DIM
    c = lambda a, n: w_in[:, :, a:a + n]
    zeros = lambda n: jnp.zeros(w_in.shape[:2] + (n,), w_in.dtype)
    w_gates = c(0, d3)
    w_hm = jnp.concatenate([c(o_q, NSA_HEADS * HEAD_DIM), c(o_ks, kv), c(o_vs, kv), c(o_kw, kv), c(o_vw, kv),
                            c(o_sbq, 3 * SB_HEADS * HEAD_DIM), c(o_dq, DSA_HEADS * HEAD_DIM)], axis=2)
    w_cmp = c(o_kc, 2 * kv)
    gw = 3 * NSA_REP
    w_misc = jnp.concatenate([c(o_iq, IDX_HEADS * IDX_DIM), c(o_ckv, DSA_KV_RANK),
                              c(o_ik, IDX_DIM), c(o_iw, IDX_HEADS), zeros(128 - IDX_DIM - IDX_HEADS),
                              c(o_g, gw), zeros(128 - gw), c(o_g + gw, gw), zeros(128 - gw)], axis=2)
    return tuple(w.astype(MXU_DTYPE) for w in (w_gates, w_hm, w_cmp, w_misc))


def kernel(x, p, w_in, norm_mix, norm_ffn, norm_ple, norm_final, w_proj_a, w_proj_b, w_proj_c, w_out,
           cmp_k_w1, cmp_k_w2, cmp_k_pe, cmp_v_w1, cmp_v_w2, cmp_v_pe, dsa_kv_norm, dsa_w_uk, dsa_w_uv,
           rel_bias_table, ffn_w_gate, ffn_w_up, ffn_w_down, ffn_conv_w, ffn_conv_b, ple_w_gate, ple_w_proj):
    batch, seq, d_model = x.shape
    depth = w_in.shape[0]
    m = batch * seq
    assert seq % KT == 0 and seq % SB_T == 0 and seq // SLC_BLOCK <= MAX_SLC_BLOCKS
    ncmp = seq // CMP_STRIDE
    bf = lambda w: w.astype(MXU_DTYPE)

    w_gates, w_hm, w_cmp, w_misc = _pack_w_in(w_in)
    w_a, w_b, w_c, w_o = bf(w_proj_a), bf(w_proj_b), bf(w_proj_c), bf(w_out)
    cmp_w1 = bf(jnp.stack([cmp_k_w1, cmp_v_w1], axis=1))
    cmp_w2 = bf(jnp.stack([cmp_k_w2, cmp_v_w2], axis=1))
    cmp_pe = jnp.stack([cmp_k_pe, cmp_v_pe], axis=1).reshape(depth, 2, 2, CMP_STRIDE * HEAD_DIM)
    w_uk, w_uv = bf(dsa_w_uk), bf(dsa_w_uv)
    f_gate, f_up, f_down = bf(ffn_w_gate), bf(ffn_w_up), bf(ffn_w_down)
    pl_gate, pl_proj = bf(ple_w_gate), bf(ple_w_proj)

    wslc, wdsa, wwin, tcmp = _bias_templates(rel_bias_table, ncmp)
    cc = np.arange(ncmp)[None, :]
    jj = np.arange(128)[:, None]
    per = SLC_BLOCK // CMP_STRIDE
    ovl = ((cc >= per * jj - (CMP_BLOCK // CMP_STRIDE - 1)) & (cc <= per * jj + per - 1)
           & (cc < ncmp - 1) & (jj < seq // SLC_BLOCK))
    ovl = jnp.asarray(ovl, MXU_DTYPE)
    low = jnp.asarray(np.arange(KT)[:, None] > np.arange(KT)[None, :], MXU_DTYPE)
    upper = jnp.asarray(np.arange(SB_T)[:, None] > np.arange(SB_T)[None, :], MXU_DTYPE)

    x = x.reshape(m, d_model)
    p = p.reshape(depth, m, p.shape[-1])
    for i in range(depth):
        h = _rmsnorm(x, norm_mix[i], MXU_DTYPE)
        gates = _matmul(h, w_gates[i], F32, 512, 512, "in_proj_gates")
        hm = _matmul_heads(h, w_hm[i], batch, MXU_DTYPE, "in_proj_heads")
        cmp_in = _matmul_heads(h, w_cmp[i], batch, F32, "in_proj_cmp")
        misc = _matmul(h, w_misc[i], F32, 512, MISC_COLS // 3, "in_proj_misc")
        cmp_kv = _compress(cmp_in, cmp_w1[i], cmp_w2[i], cmp_pe[i])
        o_a = _nsa(hm, cmp_kv, misc, wslc, wwin, tcmp, ovl)
        o_b = _stick_breaking(hm, upper)
        dk, dv = _dsa_kv(misc, dsa_kv_norm[i], w_uk[i], w_uv[i])
        o_c = _dsa(hm, dk, dv, misc, wdsa, low)
        y = _merge(o_a, o_b, o_c, w_a[i], w_b[i], w_c[i], gates)
        x = _matmul_residual(y, w_o[i], x)
        h = _rmsnorm(x, norm_ffn[i], MXU_DTYPE)
        x = _conv_ffn(h, x, f_gate[i], f_up[i], f_down[i], ffn_conv_w[i], ffn_conv_b[i], seq)
        h = _rmsnorm(x, norm_ple[i], MXU_DTYPE)
        x = _ple(h, p[i], x, pl_gate[i], pl_proj[i])
    return _rmsnorm(x, norm_final, F32).reshape(batch, seq, d_model)
```

```python
import functools
import math

import numpy as np
import jax
import jax.numpy as jnp
from jax import lax
from jax.experimental import pallas as pl
from jax.experimental.pallas import tpu as pltpu

F32 = jnp.float32
MXU_DTYPE = jnp.bfloat16

HEAD_DIM = 128
NSA_HEADS = 8
NSA_GROUPS = 2
NSA_REP = NSA_HEADS // NSA_GROUPS
CMP_BLOCK = 32
CMP_STRIDE = 16
SLC_BLOCK = 64
MAX_SLC_BLOCKS = 64
SLC_TOPN = 16
NSA_WINDOW = 512
SB_HEADS = 4
DSA_HEADS = 4
DSA_KV_RANK = 256
IDX_HEADS = 8
IDX_DIM = 64
DSA_TOPK = 256
REL_BUCKETS = 32
REL_MAX_DIST = 128
CONV_WIDTH = 3
EPS = 1e-6

MASKED = -1e30
UNSELECTED = -1e9
INT_MIN = -2 ** 31
VMEM_LIMIT = 56 * 1024 * 1024

TQ = 256
DSA_TQ = 256
KT = 512
SB_T = 256
TMPL_C0 = 896
TMPL_W = TMPL_C0 + KT

HM_NSA_Q, HM_K_SLC, HM_V_SLC, HM_K_WIN, HM_V_WIN = 0, 8, 10, 12, 14
HM_SB_Q, HM_SB_K, HM_SB_V, HM_DSA_Q, HM_HEADS = 16, 20, 24, 28, 32
MISC_IDXQ, MISC_CKV, MISC_IDXK, MISC_GATE, MISC_COLS = 0, 4, 6, 7, 9 * 128


def _cparams(sem):
    return pltpu.CompilerParams(dimension_semantics=sem, vmem_limit_bytes=VMEM_LIMIT)


def _dot(a, b):
    return jnp.dot(a, b, preferred_element_type=F32)


def _dot_nt(a, b):
    return lax.dot_general(a, b, (((1,), (1,)), ((), ())), preferred_element_type=F32)


def _rmsnorm_kernel(x_ref, g_ref, o_ref):
    x = x_ref[...]
    y = x * lax.rsqrt(jnp.mean(x * x, axis=-1, keepdims=True) + EPS)
    o_ref[...] = (y * g_ref[...]).astype(o_ref.dtype)


def _rmsnorm(x, g, out_dtype, tm=512):
    m, d = x.shape
    return pl.pallas_call(
        _rmsnorm_kernel,
        grid=(m // tm,),
        in_specs=[pl.BlockSpec((tm, d), lambda i: (i, 0)), pl.BlockSpec((1, d), lambda i: (0, 0))],
        out_specs=pl.BlockSpec((tm, d), lambda i: (i, 0)),
        out_shape=jax.ShapeDtypeStruct((m, d), out_dtype),
        name="rmsnorm",
        compiler_params=_cparams(("parallel",)),
    )(x, g.reshape(1, d))


def _mm_kernel(a_ref, w_ref, o_ref):
    o_ref[...] = _dot(a_ref[...], w_ref[...]).astype(o_ref.dtype)


def _matmul(a, w, out_dtype, tm, tn, name):
    m, k = a.shape
    n = w.shape[1]
    return pl.pallas_call(
        _mm_kernel,
        name=name,
        grid=(m // tm, n // tn),
        in_specs=[pl.BlockSpec((tm, k), lambda i, j: (i, 0)), pl.BlockSpec((k, tn), lambda i, j: (0, j))],
        out_specs=pl.BlockSpec((tm, tn), lambda i, j: (i, j)),
        out_shape=jax.ShapeDtypeStruct((m, n), out_dtype),
        compiler_params=_cparams(("parallel", "parallel")),
    )(a, w)


def _mm_heads_kernel(a_ref, w_ref, o_ref, *, hb):
    r = _dot(a_ref[...], w_ref[...])
    for j in range(hb):
        o_ref[0, j] = r[:, j * HEAD_DIM:(j + 1) * HEAD_DIM].astype(o_ref.dtype)


def _matmul_heads(a, w, batch, out_dtype, name, tm=1024, hb=4):
    m, k = a.shape
    s = m // batch
    nh = w.shape[1] // HEAD_DIM
    spb = s // tm
    return pl.pallas_call(
        functools.partial(_mm_heads_kernel, hb=hb),
        name=name,
        grid=(batch, spb, nh // hb),
        in_specs=[pl.BlockSpec((tm, k), lambda b, i, j: (b * spb + i, 0)),
                  pl.BlockSpec((k, hb * HEAD_DIM), lambda b, i, j: (0, j))],
        out_specs=pl.BlockSpec((1, hb, tm, HEAD_DIM), lambda b, i, j: (b, j, i, 0)),
        out_shape=jax.ShapeDtypeStruct((batch, nh, s, HEAD_DIM), out_dtype),
        compiler_params=_cparams(("parallel", "parallel", "parallel")),
    )(a, w)


def _mm_res_kernel(a_ref, w_ref, x_ref, o_ref):
    o_ref[...] = x_ref[...] + _dot(a_ref[...], w_ref[...])


def _matmul_residual(a, w, x, tm=1024, tn=512):
    m, k = a.shape
    n = w.shape[1]
    return pl.pallas_call(
        _mm_res_kernel,
        name="out_proj_residual",
        grid=(m // tm, n // tn),
        in_specs=[pl.BlockSpec((tm, k), lambda i, j: (i, 0)), pl.BlockSpec((k, tn), lambda i, j: (0, j)),
                  pl.BlockSpec((tm, tn), lambda i, j: (i, j))],
        out_specs=pl.BlockSpec((tm, tn), lambda i, j: (i, j)),
        out_shape=jax.ShapeDtypeStruct((m, n), F32),
        compiler_params=_cparams(("parallel", "parallel")),
    )(a, w, x)


def _merge_kernel(oa_ref, ob_ref, oc_ref, wa_ref, wb_ref, wc_ref, ga_ref, gb_ref, gc_ref, y_ref):
    y = jax.nn.sigmoid(ga_ref[...]) * _dot(oa_ref[...], wa_ref[...])
    y += jax.nn.sigmoid(gb_ref[...]) * _dot(ob_ref[...], wb_ref[...])
    y += jax.nn.sigmoid(gc_ref[...]) * _dot(oc_ref[...], wc_ref[...])
    y_ref[...] = y.astype(y_ref.dtype)


def _merge(o_a, o_b, o_c, w_a, w_b, w_c, gates, tm=1024, tn=512):
    m = o_a.shape[0]
    d = w_a.shape[1]
    nb = d // tn
    row = lambda i, j: (i, 0)
    col = lambda i, j: (0, j)
    return pl.pallas_call(
        _merge_kernel,
        name="branch_merge",
        grid=(m // tm, nb),
        in_specs=[pl.BlockSpec((tm, o_a.shape[1]), row), pl.BlockSpec((tm, o_b.shape[1]), row),
                  pl.BlockSpec((tm, o_c.shape[1]), row),
                  pl.BlockSpec((w_a.shape[0], tn), col), pl.BlockSpec((w_b.shape[0], tn), col),
                  pl.BlockSpec((w_c.shape[0], tn), col),
                  pl.BlockSpec((tm, tn), lambda i, j: (i, j)),
                  pl.BlockSpec((tm, tn), lambda i, j: (i, nb + j)),
                  pl.BlockSpec((tm, tn), lambda i, j: (i, 2 * nb + j))],
        out_specs=pl.BlockSpec((tm, tn), lambda i, j: (i, j)),
        out_shape=jax.ShapeDtypeStruct((m, d), MXU_DTYPE),
        compiler_params=_cparams(("parallel", "parallel")),
    )(o_a, o_b, o_c, w_a, w_b, w_c, gates, gates, gates)


FFN_HALO = 16


def _ffn_kernel(h_ref, hp_ref, x_ref, wg_ref, wu_ref, wd_ref, cw_ref, cb_ref, o_ref, hext_ref, *, tiles_per_seq):
    i = pl.program_id(0)
    f = pl.program_id(1)
    tm = h_ref.shape[0]

    @pl.when(f == 0)
    def _():
        first = (i % tiles_per_seq) == 0
        hext_ref[0:FFN_HALO, :] = jnp.where(first, jnp.zeros_like(hp_ref[...]), hp_ref[...])
        hext_ref[FFN_HALO:, :] = h_ref[...]
        o_ref[...] = x_ref[...]

    a = _dot(hext_ref[...], wg_ref[...])
    cw = cw_ref[...]
    c = (cw[0:1] * a[FFN_HALO - 2:FFN_HALO - 2 + tm] + cw[1:2] * a[FFN_HALO - 1:FFN_HALO - 1 + tm]
         + cw[2:3] * a[FFN_HALO:FFN_HALO + tm]) + cb_ref[...]
    u = _dot(h_ref[...], wu_ref[...])
    act = (jax.nn.gelu(c) * u).astype(MXU_DTYPE)
    o_ref[...] += _dot(act, wd_ref[...])


def _conv_ffn(h, x, w_gate, w_up, w_down, conv_w, conv_b, seq, tm=512, tf=512):
    m, d = h.shape
    ff = w_gate.shape[1]
    hb = tm // FFN_HALO
    return pl.pallas_call(
        functools.partial(_ffn_kernel, tiles_per_seq=seq // tm),
        name="conv_ffn",
        grid=(m // tm, ff // tf),
        in_specs=[pl.BlockSpec((tm, d), lambda i, f: (i, 0)),
                  pl.BlockSpec((FFN_HALO, d), lambda i, f: (jnp.maximum(i * hb - 1, 0), 0)),
                  pl.BlockSpec((tm, d), lambda i, f: (i, 0)),
                  pl.BlockSpec((d, tf), lambda i, f: (0, f)),
                  pl.BlockSpec((d, tf), lambda i, f: (0, f)),
                  pl.BlockSpec((tf, d), lambda i, f: (f, 0)),
                  pl.BlockSpec((CONV_WIDTH, tf), lambda i, f: (0, f)),
                  pl.BlockSpec((1, tf), lambda i, f: (0, f))],
        out_specs=pl.BlockSpec((tm, d), lambda i, f: (i, 0)),
        out_shape=jax.ShapeDtypeStruct((m, d), F32),
        scratch_shapes=[pltpu.VMEM((tm + FFN_HALO, d), MXU_DTYPE)],
        compiler_params=_cparams(("parallel", "arbitrary")),
    )(h, h, x, w_gate, w_up, w_down, conv_w, conv_b.reshape(1, ff))


def _ple_kernel(h_ref, p_ref, x_ref, wg_ref, wp_ref, o_ref):
    gate = jax.nn.sigmoid(_dot(h_ref[...], wg_ref[...]))
    o_ref[...] = x_ref[...] + gate * _dot(p_ref[...].astype(MXU_DTYPE), wp_ref[...])


def _ple(h, p, x, w_gate, w_proj, tm=1024, tn=512):
    m, d = h.shape
    return pl.pallas_call(
        _ple_kernel,
        name="ple",
        grid=(m // tm, d // tn),
        in_specs=[pl.BlockSpec((tm, d), lambda i, j: (i, 0)),
                  pl.BlockSpec((tm, p.shape[1]), lambda i, j: (i, 0)),
                  pl.BlockSpec((tm, tn), lambda i, j: (i, j)),
                  pl.BlockSpec((d, tn), lambda i, j: (0, j)),
                  pl.BlockSpec((p.shape[1], tn), lambda i, j: (0, j))],
        out_specs=pl.BlockSpec((tm, tn), lambda i, j: (i, j)),
        out_shape=jax.ShapeDtypeStruct((m, d), F32),
        compiler_params=_cparams(("parallel", "parallel")),
    )(h, p, x, w_gate, w_proj)


def _compress_kernel(x_ref, w1_ref, w2_ref, pe_ref, o_ref):
    x = x_ref[0, 0]
    pe = pe_ref[0]
    half = x.shape[1]
    lo = _dot((x + pe[0:1]).astype(MXU_DTYPE), w1_ref[0, :half, :])
    hi = _dot((x + pe[1:2]).astype(MXU_DTYPE), w1_ref[0, half:, :])
    n = x.shape[0]
    hid = lo + pltpu.roll(hi, n - 1, 0)
    o_ref[0, 0, 0] = _dot(jax.nn.gelu(hid).astype(MXU_DTYPE), w2_ref[0]).astype(o_ref.dtype)


def _compress(kv_hm, w1, w2, pe):
    b, _, s, d = kv_hm.shape
    nrow = s // CMP_STRIDE
    x = kv_hm.reshape(b, 2 * NSA_GROUPS, nrow, CMP_STRIDE * d)
    return pl.pallas_call(
        _compress_kernel,
        name="nsa_compress",
        grid=(b, 2, NSA_GROUPS),
        in_specs=[pl.BlockSpec((1, 1, nrow, CMP_STRIDE * d), lambda bi, kv, g: (bi, kv * NSA_GROUPS + g, 0, 0)),
                  pl.BlockSpec((1,) + w1.shape[1:], lambda bi, kv, g: (kv, 0, 0)),
                  pl.BlockSpec((1,) + w2.shape[1:], lambda bi, kv, g: (kv, 0, 0)),
                  pl.BlockSpec((1, 2, CMP_STRIDE * d), lambda bi, kv, g: (kv, 0, 0))],
        out_specs=pl.BlockSpec((1, 1, 1, nrow, d), lambda bi, kv, g: (bi, kv, g, 0, 0)),
        out_shape=jax.ShapeDtypeStruct((b, 2, NSA_GROUPS, nrow, d), MXU_DTYPE),
        compiler_params=_cparams(("parallel", "parallel", "parallel")),
    )(x, w1, w2, pe)


def _softmax_pv(lg_ref, mx_ref, acc_ref, v_ref, v_index, n_tiles):
    rows = lg_ref.shape[0]
    m = jnp.max(mx_ref[...], axis=1, keepdims=True)
    mx_ref[...] = jnp.zeros_like(mx_ref)
    acc_ref[...] = jnp.zeros_like(acc_ref)

    def body(kt, carry):
        k0 = pl.multiple_of(kt * KT, KT)
        p = jnp.exp(lg_ref[:, pl.ds(k0, KT)] - m)
        part = p[:, 0:128]
        for c in range(1, KT // 128):
            part = part + p[:, c * 128:(c + 1) * 128]
        mx_ref[...] += part
        acc_ref[...] += _dot(p.astype(MXU_DTYPE), v_ref[v_index + (pl.ds(k0, KT), slice(None))])
        return carry

    lax.fori_loop(0, n_tiles, body, 0)
    l = jnp.sum(mx_ref[...], axis=1, keepdims=True)
    return acc_ref[...] / l


def _lane_max(s):
    part = s[:, 0:128]
    for c in range(1, s.shape[1] // 128):
        part = jnp.maximum(part, s[:, c * 128:(c + 1) * 128])
    return part


def _nsa_kernel(q_ref, kc_ref, vc_ref, kslc_ref, vslc_ref, kwin_ref, vwin_ref, gate_ref,
                wslc_ref, wwin_ref, tcmp_ref, ovl_ref, o_ref,
                kslc_aug, kwin_aug, vwin_pad, qaug, lg_ref, mx_ref, acc_ref, score_ref):
    i = pl.program_id(2)
    seq = kslc_ref.shape[2]
    d = HEAD_DIM
    rows = NSA_REP * TQ
    scale = d ** -0.5
    t0 = i * TQ

    @pl.when(i == 0)
    def _():
        kslc_aug[:, 0:d] = kslc_ref[0, 0]
        srow = lax.broadcasted_iota(jnp.int32, (seq, d), 0)
        lane = lax.broadcasted_iota(jnp.int32, (seq, d), 1)
        kslc_aug[:, d:2 * d] = jnp.where((srow >> 6) == lane, 1.0, 0.0).astype(kslc_aug.dtype)
        kwin_aug[0:NSA_WINDOW, 0:d] = jnp.zeros((NSA_WINDOW, d), kwin_aug.dtype)
        kwin_aug[NSA_WINDOW:, 0:d] = kwin_ref[0, 0]
        prow = lax.broadcasted_iota(jnp.int32, (seq + NSA_WINDOW, d), 0)
        plane = lax.broadcasted_iota(jnp.int32, (seq + NSA_WINDOW, d), 1)
        flag = jnp.where(prow < NSA_WINDOW, jnp.where(plane == MAX_SLC_BLOCKS, UNSELECTED, 0.0), 0.0)
        kwin_aug[:, d:2 * d] = flag.astype(kwin_aug.dtype)
        vwin_pad[0:NSA_WINDOW, :] = jnp.zeros((NSA_WINDOW, d), vwin_pad.dtype)
        vwin_pad[NSA_WINDOW:, :] = vwin_ref[0, 0]

    q4 = q_ref[0].reshape(rows, d)

    ncmp = kc_ref.shape[3]
    kc = kc_ref[0, 0, 0]
    vc = vc_ref[0, 0, 0]
    trow = t0 + lax.broadcasted_iota(jnp.int32, (TQ, ncmp), 0)
    cend = lax.broadcasted_iota(jnp.int32, (TQ, ncmp), 1) * CMP_STRIDE + (CMP_BLOCK - 1)
    valid_c = cend <= trow
    sc_all = _dot_nt(q4, kc) * scale
    c0 = i * (TQ // CMP_STRIDE)
    pb = []
    for r in range(NSA_REP):
        bias = pltpu.roll(tcmp_ref[r], c0, 1)
        l = jnp.where(valid_c, sc_all[r * TQ:(r + 1) * TQ] + bias, MASKED)
        m = jnp.max(l, axis=1, keepdims=True)
        e = jnp.where(valid_c, jnp.exp(l - m), 0.0)
        p = e / jnp.maximum(jnp.sum(e, axis=1, keepdims=True), 1e-30)
        pb.append(p.astype(MXU_DTYPE))
    o_cmp = [_dot(pb[r], vc) for r in range(NSA_REP)]
    imp_t = _dot_nt(ovl_ref[...], pb[0])
    for r in range(1, NSA_REP):
        imp_t = imp_t + _dot_nt(ovl_ref[...], pb[r])

    span = NSA_WINDOW + TQ
    w0 = pl.multiple_of(t0, TQ)
    one_lane = lax.broadcasted_iota(jnp.int32, (rows, d), 1) == MAX_SLC_BLOCKS
    q_win = jnp.concatenate([q4, jnp.where(one_lane, 1.0, 0.0).astype(q4.dtype)], axis=1)
    s = _dot_nt(q_win, kwin_aug[pl.ds(w0, span), :]) * scale + wwin_ref[...].reshape(rows, span)
    m = jnp.max(s, axis=1, keepdims=True)
    p = jnp.exp(s - m)
    l = jnp.sum(p, axis=1, keepdims=True)
    o_win = _dot(p.astype(MXU_DTYPE), vwin_pad[pl.ds(w0, span), :]) / l

    nblk = MAX_SLC_BLOCKS
    jt = lax.broadcasted_iota(jnp.int32, (nblk, TQ), 0)
    tt = t0 + lax.broadcasted_iota(jnp.int32, (nblk, TQ), 1)
    cur = tt >> 6
    imp_t = imp_t[0:nblk]
    score = jnp.where(jt == 0, 1e9, jnp.where(jt == cur, 1e9, jnp.where(jt == cur - 1, 1e9, imp_t)))
    score = jnp.where(jt * SLC_BLOCK <= tt, score, -jnp.inf)
    score_ref[...] = score
    sub = lax.broadcasted_iota(jnp.int32, (8, TQ), 0)
    sv = [score[8 * v:8 * v + 8] for v in range(nblk // 8)]
    beaten = [jnp.zeros((8, TQ), F32) for _ in sv]
    for jp in range(nblk):
        other = score_ref[jp:jp + 1, :]
        for v in range(nblk // 8):
            if 8 * v > jp:
                hit = other >= sv[v]
            elif 8 * v + 7 <= jp:
                hit = other > sv[v]
            else:
                tie_loses = jnp.where(sub > jp - 8 * v, 1.0, 0.0)
                beaten[v] = beaten[v] + jnp.where(other == sv[v], tie_loses, 0.0)
                hit = other > sv[v]
            beaten[v] = beaten[v] + jnp.where(hit, 1.0, 0.0)
    aug_t = jnp.concatenate([jnp.where(b < SLC_TOPN, 0.0, UNSELECTED) for b in beaten], axis=0)
    row = lax.broadcasted_iota(jnp.int32, (d - nblk, TQ), 0)
    aug_t = jnp.concatenate([aug_t, jnp.where(row == 0, 1.0, 0.0)], axis=0)
    aug = aug_t.T
    qaug[:, 0:d] = q4
    for r in range(NSA_REP):
        qaug[r * TQ:(r + 1) * TQ, d:2 * d] = aug.astype(qaug.dtype)
    qa = qaug[...]

    n_tiles = (t0 + TQ + KT - 1) // KT
    off = t0 - (n_tiles - 1) * KT
    mx_ref[...] = jnp.full(mx_ref.shape, MASKED, F32)

    def slc_tile(kt, tmpl_start):
        k0 = pl.multiple_of(kt * KT, KT)
        s = _dot_nt(qa, kslc_aug[pl.ds(k0, KT), :]) * scale
        if tmpl_start is not None:
            ts = pl.multiple_of(tmpl_start, 128)
            s = s + wslc_ref[:, :, pl.ds(ts, KT)].reshape(rows, KT)
        lg_ref[:, pl.ds(k0, KT)] = s
        mx_ref[...] = jnp.maximum(mx_ref[...], _lane_max(s))

    def far_body(kt, carry):
        slc_tile(kt, None)
        return carry

    lax.fori_loop(0, jnp.maximum(n_tiles - 2, 0), far_body, 0)

    @pl.when(n_tiles >= 2)
    def _():
        slc_tile(n_tiles - 2, TMPL_C0 - off - KT)

    slc_tile(n_tiles - 1, TMPL_C0 - off)
    o_slc = _softmax_pv(lg_ref, mx_ref, acc_ref, vslc_ref, (0, 0), n_tiles)

    g = jax.nn.sigmoid(gate_ref[...])
    for r in range(NSA_REP):
        rs = slice(r * TQ, (r + 1) * TQ)
        o = (g[:, 3 * r:3 * r + 1] * o_cmp[r] + g[:, 3 * r + 1:3 * r + 2] * o_slc[rs]
             + g[:, 3 * r + 2:3 * r + 3] * o_win[rs])
        o_ref[:, r * d:(r + 1) * d] = o.astype(o_ref.dtype)


def _nsa(hm, cmp_kv, misc, wslc, wwin, tcmp, ovl):
    b, _, seq, d = hm.shape
    nq = seq // TQ
    rows = NSA_REP * TQ
    ncmp = cmp_kv.shape[3]
    kv_spec = lambda head: pl.BlockSpec((1, 1, seq, d), lambda bi, g, i: (bi, head + g, 0, 0))
    once = pl.Buffered(1)
    return pl.pallas_call(
        _nsa_kernel,
        name="nsa",
        grid=(b, NSA_GROUPS, nq),
        in_specs=[pl.BlockSpec((1, NSA_REP, TQ, d), lambda bi, g, i: (bi, g, i, 0)),
                  pl.BlockSpec((1, 1, 1, ncmp, d), lambda bi, g, i: (bi, 0, g, 0, 0)),
                  pl.BlockSpec((1, 1, 1, ncmp, d), lambda bi, g, i: (bi, 1, g, 0, 0)),
                  kv_spec(HM_K_SLC), kv_spec(HM_V_SLC), kv_spec(HM_K_WIN), kv_spec(HM_V_WIN),
                  pl.BlockSpec((TQ, 128), lambda bi, g, i: (bi * nq + i, MISC_GATE + g)),
                  pl.BlockSpec((NSA_REP, TQ, TMPL_W), lambda bi, g, i: (g, 0, 0), pipeline_mode=once),
                  pl.BlockSpec((NSA_REP, TQ, NSA_WINDOW + TQ), lambda bi, g, i: (g, 0, 0), pipeline_mode=once),
                  pl.BlockSpec((NSA_REP, TQ, ncmp), lambda bi, g, i: (g, 0, 0), pipeline_mode=once),
                  pl.BlockSpec(ovl.shape, lambda bi, g, i: (0, 0), pipeline_mode=once)],
        out_specs=pl.BlockSpec((TQ, NSA_REP * d), lambda bi, g, i: (bi * nq + i, g)),
        out_shape=jax.ShapeDtypeStruct((b * seq, NSA_HEADS * d), MXU_DTYPE),
        scratch_shapes=[pltpu.VMEM((seq, 2 * d), MXU_DTYPE),
                        pltpu.VMEM((seq + NSA_WINDOW, 2 * d), MXU_DTYPE),
                        pltpu.VMEM((seq + NSA_WINDOW, d), MXU_DTYPE),
                        pltpu.VMEM((rows, 2 * d), MXU_DTYPE),
                        pltpu.VMEM((rows, seq), F32),
                        pltpu.VMEM((rows, 128), F32),
                        pltpu.VMEM((rows, d), F32),
                        pltpu.VMEM((MAX_SLC_BLOCKS, TQ), F32)],
        compiler_params=_cparams(("parallel", "parallel", "arbitrary")),
    )(hm, cmp_kv, cmp_kv, hm, hm, hm, hm, misc, wslc, wwin, tcmp, ovl)


def _sb_kernel(q_ref, k_ref, v_ref, upper_ref, o_ref, acc_ref, carry_ref):
    i = pl.program_id(1)
    t = SB_T
    d = HEAD_DIM
    scale = d ** -0.5
    upper = upper_ref[...]
    acc_ref[...] = jnp.zeros_like(acc_ref)
    carry_ref[...] = jnp.zeros_like(carry_ref)

    heads = range(SB_HEADS)

    def tile(kt, mask):
        k0 = pl.multiple_of(kt * t, t)
        z = [_dot_nt(q_ref[0, h], k_ref[0, h, pl.ds(k0, t), :]) * scale for h in heads]
        ls = [jnp.minimum(z[h], 0.0) - jnp.log(1.0 + jnp.exp(-jnp.abs(z[h]))) for h in heads]
        lk = [ls[h] - z[h] for h in heads]
        if mask is not None:
            lk = [jnp.where(mask, lk[h], 0.0) for h in heads]
        hi = [lk[h].astype(MXU_DTYPE) for h in heads]
        r1 = [lk[h] - hi[h].astype(F32) for h in heads]
        mid = [r1[h].astype(MXU_DTYPE) for h in heads]
        lo = [(r1[h] - mid[h].astype(F32)).astype(MXU_DTYPE) for h in heads]
        after = [carry_ref[h] + (_dot(hi[h], upper) + _dot(mid[h], upper) + _dot(lo[h], upper)) for h in heads]
        w = [jnp.exp(ls[h] + after[h]) for h in heads]
        if mask is not None:
            w = [jnp.where(mask, w[h], 0.0) for h in heads]
        for h in heads:
            acc_ref[h] += _dot(w[h].astype(MXU_DTYPE), v_ref[0, h, pl.ds(k0, t), :])
            carry_ref[h] += jnp.sum(lk[h], axis=1, keepdims=True)

    strict = lax.broadcasted_iota(jnp.int32, (t, t), 1) < lax.broadcasted_iota(jnp.int32, (t, t), 0)
    tile(i, strict)

    def body(n, carry):
        tile(i - 1 - n, None)
        return carry

    lax.fori_loop(0, i, body, 0)
    for h in range(SB_HEADS):
        o_ref[:, h * d:(h + 1) * d] = acc_ref[h].astype(o_ref.dtype)


def _stick_breaking(hm, upper):
    b, _, seq, d = hm.shape
    nq = seq // SB_T
    return pl.pallas_call(
        _sb_kernel,
        grid=(b, nq),
        in_specs=[pl.BlockSpec((1, SB_HEADS, SB_T, d), lambda bi, i: (bi, HM_SB_Q // SB_HEADS, i, 0)),
                  pl.BlockSpec((1, SB_HEADS, seq, d), lambda bi, i: (bi, HM_SB_K // SB_HEADS, 0, 0)),
                  pl.BlockSpec((1, SB_HEADS, seq, d), lambda bi, i: (bi, HM_SB_V // SB_HEADS, 0, 0)),
                  pl.BlockSpec(upper.shape, lambda bi, i: (0, 0))],
        out_specs=pl.BlockSpec((SB_T, SB_HEADS * d), lambda bi, i: (bi * nq + i, 0)),
        out_shape=jax.ShapeDtypeStruct((b * seq, SB_HEADS * d), MXU_DTYPE),
        scratch_shapes=[pltpu.VMEM((SB_HEADS, SB_T, d), F32), pltpu.VMEM((SB_HEADS, SB_T, 1), F32)],
        name="stick_breaking",
        compiler_params=_cparams(("parallel", "parallel")),
    )(hm, hm, hm, upper)


def _dsa_kv_kernel(c_ref, g_ref, wk_ref, wv_ref, k_ref, v_ref):
    c = c_ref[...]
    y = c * lax.rsqrt(jnp.mean(c * c, axis=-1, keepdims=True) + EPS)
    y = (y * g_ref[...]).astype(MXU_DTYPE)
    k_ref[...] = _dot(y, wk_ref[...]).astype(k_ref.dtype)
    v_ref[...] = _dot(y, wv_ref[...]).astype(v_ref.dtype)


def _dsa_kv(misc, kv_norm, w_uk, w_uv, tm=512):
    m = misc.shape[0]
    r = DSA_KV_RANK
    out = jax.ShapeDtypeStruct((m, HEAD_DIM), MXU_DTYPE)
    return pl.pallas_call(
        _dsa_kv_kernel,
        name="dsa_kv",
        grid=(m // tm,),
        in_specs=[pl.BlockSpec((tm, r), lambda i: (i, MISC_CKV * 128 // r)),
                  pl.BlockSpec((1, r), lambda i: (0, 0)),
                  pl.BlockSpec((r, HEAD_DIM), lambda i: (0, 0)),
                  pl.BlockSpec((r, HEAD_DIM), lambda i: (0, 0))],
        out_specs=[pl.BlockSpec((tm, HEAD_DIM), lambda i: (i, 0)), pl.BlockSpec((tm, HEAD_DIM), lambda i: (i, 0))],
        out_shape=[out, out],
        compiler_params=_cparams(("parallel",)),
    )(misc, kv_norm.reshape(1, r), w_uk, w_uv)


PLANE_GROUPS_PER_TILE = KT // 256


def _bit_planes(words):
    a = list(words)
    j, m = 16, 0x0000FFFF
    while j:
        sh = jnp.full(a[0].shape, j, jnp.int32)
        for k in range(32):
            if not k & j:
                t = (a[k] ^ lax.shift_right_logical(a[k + j], sh)) & m
                a[k] = a[k] ^ t
                a[k + j] = a[k + j] ^ (t << j)
        j >>= 1
        m = (m ^ (m << j)) & 0xFFFFFFFF
    return a


def _dsa_kernel(q_ref, k_ref, v_ref, iq_ref, ik_ref, iw_ref, wd_ref, low_ref, o_ref,
                key_ref, plane_ref, alive_ref, add_ref, lg_ref, mx_ref, acc_ref, *, n_keep):
    i = pl.program_id(1)
    d = HEAD_DIM
    tq = DSA_TQ
    rows = DSA_HEADS * tq
    scale = d ** -0.5
    t0 = i * tq
    n_tiles = (t0 + tq + KT - 1) // KT
    off = t0 - (n_tiles - 1) * KT

    iq = iq_ref[...].astype(MXU_DTYPE)
    qh = [iq[:, h * IDX_DIM:(h + 1) * IDX_DIM] for h in range(IDX_HEADS)]
    wi_t = iw_ref[...].T * (IDX_HEADS ** -0.5) * (IDX_DIM ** -0.5)
    wh = [wi_t[IDX_DIM + h:IDX_DIM + h + 1, :] for h in range(IDX_HEADS)]
    tq_row = t0 + lax.broadcasted_iota(jnp.int32, (KT, tq), 1)
    krow = lax.broadcasted_iota(jnp.int32, (KT, tq), 0)

    def column_sum(a):
        return jnp.sum(a.reshape(KT // 8, 8, tq), axis=0)

    def score_tile(kt, causal):
        k0 = pl.multiple_of(kt * KT, KT)
        ki = ik_ref[pl.ds(k0, KT), 0:IDX_DIM].astype(MXU_DTYPE)
        dots = [_dot_nt(ki, qh[h]) for h in range(IDX_HEADS)]
        sc = wh[0] * jnp.maximum(dots[0], 0.0)
        for h in range(1, IDX_HEADS):
            sc = sc + wh[h] * jnp.maximum(dots[h], 0.0)
        sc = sc + 0.0
        if causal:
            sc = jnp.where(k0 + krow <= tq_row, sc, -jnp.inf)
        bits = lax.bitcast_convert_type(sc, jnp.int32)
        key = bits ^ ((bits >> 31) & 0x7FFFFFFF)
        key_ref[pl.ds(k0, KT), :] = key
        ukey = key ^ INT_MIN
        for g in range(PLANE_GROUPS_PER_TILE):
            words = [ukey[(32 * g + w) * 8:(32 * g + w + 1) * 8] for w in range(32)]
            for x, plane in enumerate(_bit_planes(words)):
                plane_ref[x, PLANE_GROUPS_PER_TILE * kt + g] = plane

    @pl.when((pl.program_id(0) == 0) & (i == 0))
    def _():
        plane_ref[...] = jnp.zeros_like(plane_ref)

    def score_body(kt, carry):
        score_tile(kt, False)
        return carry

    lax.fori_loop(0, n_tiles - 1, score_body, 0)
    score_tile(n_tiles - 1, True)

    ngrp = alive_ref.shape[0]
    for g in range(ngrp):
        alive_ref[g] = jnp.where(g < PLANE_GROUPS_PER_TILE * n_tiles, -1, 0) + jnp.zeros((8, tq), jnp.int32)

    def bit_body(x, carry):
        thr_u, remaining = carry
        ones = [alive_ref[g] & plane_ref[x, g] for g in range(ngrp)]
        c = lax.population_count(ones[0])
        for g in range(1, ngrp):
            c = c + lax.population_count(ones[g])
        c = jnp.sum(c, axis=0, keepdims=True)
        take = c >= remaining
        for g in range(ngrp):
            alive_ref[g] = jnp.where(take, ones[g], alive_ref[g] ^ ones[g])
        bit = jnp.int32(1) << (31 - x)
        return jnp.where(take, thr_u | bit, thr_u), jnp.where(take, remaining, remaining - c)

    thr_u, need = lax.fori_loop(0, 32, bit_body,
                                (jnp.zeros((1, tq), jnp.int32), jnp.full((1, tq), n_keep, jnp.int32)))
    thr = thr_u ^ INT_MIN
    need = need.astype(F32)

    def mask_body(kt, seen):
        k0 = pl.multiple_of(kt * KT, KT)
        kk = key_ref[pl.ds(k0, KT), :]
        eq = jnp.where(kk == thr, 1.0, 0.0)
        before = seen + _dot(low_ref[...], eq.astype(MXU_DTYPE))
        tie = jnp.where(before < need, 0.0, MASKED)
        add_t = jnp.where(kk > thr, 0.0, jnp.where(kk == thr, tie, MASKED))
        add_ref[:, pl.ds(k0, KT)] = add_t.T
        return seen + jnp.sum(column_sum(eq), axis=0, keepdims=True)

    lax.fori_loop(0, n_tiles, mask_body, jnp.zeros((1, tq), F32))

    q4 = q_ref[0].reshape(rows, d)
    mx_ref[...] = jnp.full(mx_ref.shape, MASKED, F32)

    def att_tile(kt, tmpl_start):
        k0 = pl.multiple_of(kt * KT, KT)
        s = _dot_nt(q4, k_ref[0, pl.ds(k0, KT), :]) * scale
        addm = add_ref[:, pl.ds(k0, KT)]
        s = s + jnp.concatenate([addm] * DSA_HEADS, axis=0)
        if tmpl_start is not None:
            ts = pl.multiple_of(tmpl_start, 128)
            s = s + wd_ref[:, :, pl.ds(ts, KT)].reshape(rows, KT)
        lg_ref[:, pl.ds(k0, KT)] = s
        mx_ref[...] = jnp.maximum(mx_ref[...], _lane_max(s))

    def far_body(kt, carry):
        att_tile(kt, None)
        return carry

    lax.fori_loop(0, jnp.maximum(n_tiles - 2, 0), far_body, 0)

    @pl.when(n_tiles >= 2)
    def _():
        att_tile(n_tiles - 2, TMPL_C0 - off - KT)

    att_tile(n_tiles - 1, TMPL_C0 - off)
    o = _softmax_pv(lg_ref, mx_ref, acc_ref, v_ref, (0,), n_tiles)
    for r in range(DSA_HEADS):
        o_ref[:, r * d:(r + 1) * d] = o[r * tq:(r + 1) * tq].astype(o_ref.dtype)


def _dsa(hm, k, v, misc, wdsa, low):
    b, _, seq, d = hm.shape
    tq = DSA_TQ
    nq = seq // tq
    rows = DSA_HEADS * tq
    k = k.reshape(b, seq, d)
    v = v.reshape(b, seq, d)
    n_keep = min(DSA_TOPK, seq // 4)
    ngrp = PLANE_GROUPS_PER_TILE * (seq // KT)
    return pl.pallas_call(
        functools.partial(_dsa_kernel, n_keep=n_keep),
        name="dsa",
        grid=(b, nq),
        in_specs=[pl.BlockSpec((1, DSA_HEADS, tq, d), lambda bi, i: (bi, HM_DSA_Q // DSA_HEADS, i, 0)),
                  pl.BlockSpec((1, seq, d), lambda bi, i: (bi, 0, 0)),
                  pl.BlockSpec((1, seq, d), lambda bi, i: (bi, 0, 0)),
                  pl.BlockSpec((tq, IDX_HEADS * IDX_DIM), lambda bi, i: (bi * nq + i, MISC_IDXQ)),
                  pl.BlockSpec((seq, 128), lambda bi, i: (bi, MISC_IDXK)),
                  pl.BlockSpec((tq, 128), lambda bi, i: (bi * nq + i, MISC_IDXK)),
                  pl.BlockSpec((DSA_HEADS, tq, TMPL_W), lambda bi, i: (0, 0, 0), pipeline_mode=pl.Buffered(1)),
                  pl.BlockSpec(low.shape, lambda bi, i: (0, 0), pipeline_mode=pl.Buffered(1))],
        out_specs=pl.BlockSpec((tq, DSA_HEADS * d), lambda bi, i: (bi * nq + i, 0)),
        out_shape=jax.ShapeDtypeStruct((b * seq, DSA_HEADS * d), MXU_DTYPE),
        scratch_shapes=[pltpu.VMEM((seq, tq), jnp.int32),
                        pltpu.VMEM((32, ngrp, 8, tq), jnp.int32),
                        pltpu.VMEM((ngrp, 8, tq), jnp.int32),
                        pltpu.VMEM((tq, seq), F32),
                        pltpu.VMEM((rows, seq), F32),
                        pltpu.VMEM((rows, 128), F32),
                        pltpu.VMEM((rows, d), F32)],
        compiler_params=_cparams(("arbitrary", "arbitrary")),
    )(hm, k, v, misc, misc, misc, wdsa, low)


def _t5_bucket(dist):
    n = jnp.maximum(dist, 0)
    max_exact = REL_BUCKETS // 2
    nf = jnp.maximum(n, 1).astype(F32)
    large = max_exact + (jnp.log(nf / max_exact) / math.log(REL_MAX_DIST / max_exact)
                         * (REL_BUCKETS - max_exact)).astype(jnp.int32)
    large = jnp.minimum(large, REL_BUCKETS - 1)
    return jnp.where(n < max_exact, n, large)


def _bias_templates(rel_tab, ncmp):
    far = REL_MAX_DIST
    by_dist = rel_tab[_t5_bucket(jnp.arange(far + 1))] - rel_tab[REL_BUCKETS - 1][None, :]
    by_dist = by_dist.T

    def build(dist, valid, fill=MASKED):
        t = by_dist[:, np.clip(dist, 0, far)]
        return jnp.where(valid[None], t, fill).astype(F32)

    def toeplitz(u, nrows, width):
        nh, l = u.shape
        return jnp.tile(u, (1, nrows))[:, :nrows * (l - 1)].reshape(nh, nrows, l - 1)[:, :, :width]

    def diagonals(nrows, width):
        l = width + nrows
        k = np.arange(l)
        return np.where(k < width, k, k - l)

    def causal(heads, nrows):
        dist = TMPL_C0 - diagonals(nrows, TMPL_W)
        return toeplitz(build(dist, dist >= 0)[heads], nrows, TMPL_W)

    nsa = slice(0, NSA_HEADS)
    span = NSA_WINDOW + TQ
    dist = NSA_WINDOW - diagonals(TQ, span)
    window = toeplitz(build(dist, (dist >= 0) & (dist < NSA_WINDOW))[nsa], TQ, span)
    cc = np.arange(ncmp)[None, :]
    cc = np.where(cc < ncmp // 2, cc, cc - ncmp)
    dist = np.arange(TQ)[:, None] - CMP_STRIDE * cc - (CMP_BLOCK - 1)
    cmp = build(dist, dist >= 0, 0.0)[nsa]
    return causal(nsa, TQ), causal(slice(NSA_HEADS, None), DSA_TQ), window, cmp


def _pack_w_in(w_in):
    d3 = 3 * w_in.shape[1]
    kv = NSA_GROUPS * HEAD_DIM
    o_q = d3
    o_kc = o_q + NSA_HEADS * HEAD_DIM
    o_vc, o_ks, o_vs, o_kw, o_vw = (o_kc + j * kv for j in range(1, 6))
    o_g = o_vw + kv
    o_sbq = o_g + 3 * NSA_HEADS
    o_sbk = o_sbq + SB_HEADS * HEAD_DIM
    o_sbv = o_sbk + SB_HEADS * HEAD_DIM
    o_dq = o_sbv + SB_HEADS * HEAD_DIM
    o_ckv = o_dq + DSA_HEADS * HEAD_DIM
    o_iq = o_ckv + DSA_KV_RANK
    o_ik = o_iq + IDX_HEADS * IDX_DIM
    o_iw = o_ik + IDX_DIM
    c = lambda a, n: w_in[:, :, a:a + n]
    zeros = lambda n: jnp.zeros(w_in.shape[:2] + (n,), w_in.dtype)
    w_gates = c(0, d3)
    w_hm = jnp.concatenate([c(o_q, NSA_HEADS * HEAD_DIM), c(o_ks, kv), c(o_vs, kv), c(o_kw, kv), c(o_vw, kv),
                            c(o_sbq, 3 * SB_HEADS * HEAD_DIM), c(o_dq, DSA_HEADS * HEAD_DIM)], axis=2)
    w_cmp = c(o_kc, 2 * kv)
    gw = 3 * NSA_REP
    w_misc = jnp.concatenate([c(o_iq, IDX_HEADS * IDX_DIM), c(o_ckv, DSA_KV_RANK),
                              c(o_ik, IDX_DIM), c(o_iw, IDX_HEADS), zeros(128 - IDX_DIM - IDX_HEADS),
                              c(o_g, gw), zeros(128 - gw), c(o_g + gw, gw), zeros(128 - gw)], axis=2)
    return tuple(w.astype(MXU_DTYPE) for w in (w_gates, w_hm, w_cmp, w_misc))


def kernel(x, p, w_in, norm_mix, norm_ffn, norm_ple, norm_final, w_proj_a, w_proj_b, w_proj_c, w_out,
           cmp_k_w1, cmp_k_w2, cmp_k_pe, cmp_v_w1, cmp_v_w2, cmp_v_pe, dsa_kv_norm, dsa_w_uk, dsa_w_uv,
           rel_bias_table, ffn_w_gate, ffn_w_up, ffn_w_down, ffn_conv_w, ffn_conv_b, ple_w_gate, ple_w_proj):
    batch, seq, d_model = x.shape
    depth = w_in.shape[0]
    m = batch * seq
    assert seq % KT == 0 and seq % SB_T == 0 and seq // SLC_BLOCK <= MAX_SLC_BLOCKS
    ncmp = seq // CMP_STRIDE
    bf = lambda w: w.astype(MXU_DTYPE)

    w_gates, w_hm, w_cmp, w_misc = _pack_w_in(w_in)
    w_a, w_b, w_c, w_o = bf(w_proj_a), bf(w_proj_b), bf(w_proj_c), bf(w_out)
    cmp_w1 = bf(jnp.stack([cmp_k_w1, cmp_v_w1], axis=1))
    cmp_w2 = bf(jnp.stack([cmp_k_w2, cmp_v_w2], axis=1))
    cmp_pe = jnp.stack([cmp_k_pe, cmp_v_pe], axis=1).reshape(depth, 2, 2, CMP_STRIDE * HEAD_DIM)
    w_uk, w_uv = bf(dsa_w_uk), bf(dsa_w_uv)
    f_gate, f_up, f_down = bf(ffn_w_gate), bf(ffn_w_up), bf(ffn_w_down)
    pl_gate, pl_proj = bf(ple_w_gate), bf(ple_w_proj)

    wslc, wdsa, wwin, tcmp = _bias_templates(rel_bias_table, ncmp)
    cc = np.arange(ncmp)[None, :]
    jj = np.arange(128)[:, None]
    per = SLC_BLOCK // CMP_STRIDE
    ovl = ((cc >= per * jj - (CMP_BLOCK // CMP_STRIDE - 1)) & (cc <= per * jj + per - 1)
           & (cc < ncmp - 1) & (jj < seq // SLC_BLOCK))
    ovl = jnp.asarray(ovl, MXU_DTYPE)
    low = jnp.asarray(np.arange(KT)[:, None] > np.arange(KT)[None, :], MXU_DTYPE)
    upper = jnp.asarray(np.arange(SB_T)[:, None] > np.arange(SB_T)[None, :], MXU_DTYPE)

    x = x.reshape(m, d_model)
    p = p.reshape(depth, m, p.shape[-1])
    for i in range(depth):
        h = _rmsnorm(x, norm_mix[i], MXU_DTYPE)
        gates = _matmul(h, w_gates[i], F32, 1024, 512, "in_proj_gates")
        hm = _matmul_heads(h, w_hm[i], batch, MXU_DTYPE, "in_proj_heads")
        cmp_in = _matmul_heads(h, w_cmp[i], batch, F32, "in_proj_cmp")
        misc = _matmul(h, w_misc[i], F32, 1024, MISC_COLS // 3, "in_proj_misc")
        cmp_kv = _compress(cmp_in, cmp_w1[i], cmp_w2[i], cmp_pe[i])
        o_a = _nsa(hm, cmp_kv, misc, wslc, wwin, tcmp, ovl)
        o_b = _stick_breaking(hm, upper)
        dk, dv = _dsa_kv(misc, dsa_kv_norm[i], w_uk[i], w_uv[i])
        o_c = _dsa(hm, dk, dv, misc, wdsa, low)
        y = _merge(o_a, o_b, o_c, w_a[i], w_b[i], w_c[i], gates)
        x = _matmul_residual(y, w_o[i], x)
        h = _rmsnorm(x, norm_ffn[i], MXU_DTYPE)
        x = _conv_ffn(h, x, f_gate[i], f_up[i], f_down[i], ffn_conv_w[i], ffn_conv_b[i], seq)
        h = _rmsnorm(x, norm_ple[i], MXU_DTYPE)
        x = _ple(h, p[i], x, pl_gate[i], pl_proj[i])
    return _rmsnorm(x, norm_final, F32).reshape(batch, seq, d_model)
```

```python
import functools
import math

import numpy as np
import jax
import jax.numpy as jnp
from jax import lax
from jax.experimental import pallas as pl
from jax.experimental.pallas import tpu as pltpu

F32 = jnp.float32
MXU_DTYPE = jnp.bfloat16

HEAD_DIM = 128
NSA_HEADS = 8
NSA_GROUPS = 2
NSA_REP = NSA_HEADS // NSA_GROUPS
CMP_BLOCK = 32
CMP_STRIDE = 16
SLC_BLOCK = 64
MAX_SLC_BLOCKS = 64
SLC_TOPN = 16
NSA_WINDOW = 512
SB_HEADS = 4
DSA_HEADS = 4
DSA_KV_RANK = 256
IDX_HEADS = 8
IDX_DIM = 64
DSA_TOPK = 256
REL_BUCKETS = 32
REL_MAX_DIST = 128
CONV_WIDTH = 3
EPS = 1e-6

MASKED = -1e30
UNSELECTED = -1e9
INT_MIN = -2 ** 31
VMEM_LIMIT = 56 * 1024 * 1024

TQ = 256
DSA_TQ = 256
KT = 512
SB_T = 256
TMPL_C0 = 896
TMPL_W = TMPL_C0 + KT
CMP_BAND = 32

HM_NSA_Q, HM_K_SLC, HM_V_SLC, HM_K_WIN, HM_V_WIN = 0, 8, 10, 12, 14
HM_SB_Q, HM_SB_K, HM_SB_V, HM_DSA_Q, HM_HEADS = 16, 20, 24, 28, 32
MISC_IDXQ, MISC_CKV, MISC_IDXK, MISC_GATE, MISC_COLS = 0, 4, 6, 7, 9 * 128


def _cparams(sem):
    return pltpu.CompilerParams(dimension_semantics=sem, vmem_limit_bytes=VMEM_LIMIT)


def _dot(a, b):
    return jnp.dot(a, b, preferred_element_type=F32)


def _dot_nt(a, b):
    return lax.dot_general(a, b, (((1,), (1,)), ((), ())), preferred_element_type=F32)


def _rmsnorm_kernel(x_ref, g_ref, o_ref):
    x = x_ref[...]
    y = x * lax.rsqrt(jnp.mean(x * x, axis=-1, keepdims=True) + EPS)
    o_ref[...] = (y * g_ref[...]).astype(o_ref.dtype)


def _rmsnorm(x, g, out_dtype, tm=512):
    m, d = x.shape
    return pl.pallas_call(
        _rmsnorm_kernel,
        grid=(m // tm,),
        in_specs=[pl.BlockSpec((tm, d), lambda i: (i, 0)), pl.BlockSpec((1, d), lambda i: (0, 0))],
        out_specs=pl.BlockSpec((tm, d), lambda i: (i, 0)),
        out_shape=jax.ShapeDtypeStruct((m, d), out_dtype),
        name="rmsnorm",
        compiler_params=_cparams(("parallel",)),
    )(x, g.reshape(1, d))


def _mm_kernel(a_ref, w_ref, o_ref):
    o_ref[...] = _dot(a_ref[...], w_ref[...]).astype(o_ref.dtype)


def _matmul(a, w, out_dtype, tm, tn, name):
    m, k = a.shape
    n = w.shape[1]
    return pl.pallas_call(
        _mm_kernel,
        name=name,
        grid=(m // tm, n // tn),
        in_specs=[pl.BlockSpec((tm, k), lambda i, j: (i, 0)), pl.BlockSpec((k, tn), lambda i, j: (0, j))],
        out_specs=pl.BlockSpec((tm, tn), lambda i, j: (i, j)),
        out_shape=jax.ShapeDtypeStruct((m, n), out_dtype),
        compiler_params=_cparams(("parallel", "parallel")),
    )(a, w)


def _mm_heads_kernel(a_ref, w_ref, o_ref, *, hb):
    r = _dot(a_ref[...], w_ref[...])
    for j in range(hb):
        o_ref[0, j] = r[:, j * HEAD_DIM:(j + 1) * HEAD_DIM].astype(o_ref.dtype)


def _matmul_heads(a, w, batch, out_dtype, name, tm=1024, hb=4):
    m, k = a.shape
    s = m // batch
    nh = w.shape[1] // HEAD_DIM
    spb = s // tm
    return pl.pallas_call(
        functools.partial(_mm_heads_kernel, hb=hb),
        name=name,
        grid=(batch, spb, nh // hb),
        in_specs=[pl.BlockSpec((tm, k), lambda b, i, j: (b * spb + i, 0)),
                  pl.BlockSpec((k, hb * HEAD_DIM), lambda b, i, j: (0, j))],
        out_specs=pl.BlockSpec((1, hb, tm, HEAD_DIM), lambda b, i, j: (b, j, i, 0)),
        out_shape=jax.ShapeDtypeStruct((batch, nh, s, HEAD_DIM), out_dtype),
        compiler_params=_cparams(("parallel", "parallel", "parallel")),
    )(a, w)


def _norm_rows(x, g):
    return x * lax.rsqrt(jnp.mean(x * x, axis=-1, keepdims=True) + EPS) * g


def _mm_res_kernel(a_ref, w_ref, x_ref, g_ref, o_ref, h_ref):
    x = x_ref[...] + _dot(a_ref[...], w_ref[...])
    o_ref[...] = x
    h_ref[...] = _norm_rows(x, g_ref[...]).astype(h_ref.dtype)


def _matmul_residual(a, w, x, g, tm=512):
    m, k = a.shape
    n = w.shape[1]
    row = lambda i: (i, 0)
    fixed = lambda i: (0, 0)
    return pl.pallas_call(
        _mm_res_kernel,
        name="out_proj_residual",
        grid=(m // tm,),
        in_specs=[pl.BlockSpec((tm, k), row),
                  pl.BlockSpec((k, n), fixed, pipeline_mode=pl.Buffered(1)),
                  pl.BlockSpec((tm, n), row),
                  pl.BlockSpec((1, n), fixed)],
        out_specs=[pl.BlockSpec((tm, n), row), pl.BlockSpec((tm, n), row)],
        out_shape=[jax.ShapeDtypeStruct((m, n), F32), jax.ShapeDtypeStruct((m, n), MXU_DTYPE)],
        compiler_params=_cparams(("parallel",)),
    )(a, w, x, g.reshape(1, n))


def _merge_kernel(oa_ref, ob_ref, oc_ref, wa_ref, wb_ref, wc_ref, ga_ref, gb_ref, gc_ref, y_ref):
    y = jax.nn.sigmoid(ga_ref[...]) * _dot(oa_ref[...], wa_ref[...])
    y += jax.nn.sigmoid(gb_ref[...]) * _dot(ob_ref[...], wb_ref[...])
    y += jax.nn.sigmoid(gc_ref[...]) * _dot(oc_ref[...], wc_ref[...])
    y_ref[...] = y.astype(y_ref.dtype)


def _merge(o_a, o_b, o_c, w_a, w_b, w_c, gates, tm=1024, tn=512):
    m = o_a.shape[0]
    d = w_a.shape[1]
    nb = d // tn
    row = lambda i, j: (i, 0)
    col = lambda i, j: (0, j)
    return pl.pallas_call(
        _merge_kernel,
        name="branch_merge",
        grid=(m // tm, nb),
        in_specs=[pl.BlockSpec((tm, o_a.shape[1]), row), pl.BlockSpec((tm, o_b.shape[1]), row),
                  pl.BlockSpec((tm, o_c.shape[1]), row),
                  pl.BlockSpec((w_a.shape[0], tn), col), pl.BlockSpec((w_b.shape[0], tn), col),
                  pl.BlockSpec((w_c.shape[0], tn), col),
                  pl.BlockSpec((tm, tn), lambda i, j: (i, j)),
                  pl.BlockSpec((tm, tn), lambda i, j: (i, nb + j)),
                  pl.BlockSpec((tm, tn), lambda i, j: (i, 2 * nb + j))],
        out_specs=pl.BlockSpec((tm, tn), lambda i, j: (i, j)),
        out_shape=jax.ShapeDtypeStruct((m, d), MXU_DTYPE),
        compiler_params=_cparams(("parallel", "parallel")),
    )(o_a, o_b, o_c, w_a, w_b, w_c, gates, gates, gates)


FFN_HALO = 16


def _ffn_kernel(h_ref, hp_ref, x_ref, wg_ref, wu_ref, wd_ref, cw_ref, cb_ref, g_ref, o_ref, hn_ref, hext_ref,
                *, tiles_per_seq):
    i = pl.program_id(0)
    f = pl.program_id(1)
    tm = h_ref.shape[0]

    @pl.when(f == 0)
    def _():
        first = (i % tiles_per_seq) == 0
        hext_ref[0:FFN_HALO, :] = jnp.where(first, jnp.zeros_like(hp_ref[...]), hp_ref[...])
        hext_ref[FFN_HALO:, :] = h_ref[...]
        o_ref[...] = x_ref[...]

    tf = wg_ref.shape[1]
    halves = [slice(0, tf // 2), slice(tf // 2, tf)]
    a = [_dot(hext_ref[...], wg_ref[:, s]) for s in halves]
    u = [_dot(h_ref[...], wu_ref[:, s]) for s in halves]
    cw = cw_ref[...]
    cb = cb_ref[...]
    act = []
    for a_j, u_j, s in zip(a, u, halves):
        c = (cw[0:1, s] * a_j[FFN_HALO - 2:FFN_HALO - 2 + tm] + cw[1:2, s] * a_j[FFN_HALO - 1:FFN_HALO - 1 + tm]
             + cw[2:3, s] * a_j[FFN_HALO:FFN_HALO + tm]) + cb[:, s]
        act.append((jax.nn.gelu(c) * u_j).astype(MXU_DTYPE))
    for act_j, s in zip(act, halves):
        o_ref[...] += _dot(act_j, wd_ref[s, :])

    @pl.when(f == pl.num_programs(1) - 1)
    def _():
        hn_ref[...] = _norm_rows(o_ref[...], g_ref[...]).astype(hn_ref.dtype)


def _conv_ffn(h, x, w_gate, w_up, w_down, conv_w, conv_b, g, seq, tm=512, tf=512):
    m, d = h.shape
    ff = w_gate.shape[1]
    hb = tm // FFN_HALO
    return pl.pallas_call(
        functools.partial(_ffn_kernel, tiles_per_seq=seq // tm),
        name="conv_ffn",
        grid=(m // tm, ff // tf),
        in_specs=[pl.BlockSpec((tm, d), lambda i, f: (i, 0)),
                  pl.BlockSpec((FFN_HALO, d), lambda i, f: (jnp.maximum(i * hb - 1, 0), 0)),
                  pl.BlockSpec((tm, d), lambda i, f: (i, 0)),
                  pl.BlockSpec((d, tf), lambda i, f: (0, f)),
                  pl.BlockSpec((d, tf), lambda i, f: (0, f)),
                  pl.BlockSpec((tf, d), lambda i, f: (f, 0)),
                  pl.BlockSpec((CONV_WIDTH, tf), lambda i, f: (0, f)),
                  pl.BlockSpec((1, tf), lambda i, f: (0, f)),
                  pl.BlockSpec((1, d), lambda i, f: (0, 0))],
        out_specs=[pl.BlockSpec((tm, d), lambda i, f: (i, 0)), pl.BlockSpec((tm, d), lambda i, f: (i, 0))],
        out_shape=[jax.ShapeDtypeStruct((m, d), F32), jax.ShapeDtypeStruct((m, d), MXU_DTYPE)],
        scratch_shapes=[pltpu.VMEM((tm + FFN_HALO, d), MXU_DTYPE)],
        compiler_params=_cparams(("parallel", "arbitrary")),
    )(h, h, x, w_gate, w_up, w_down, conv_w, conv_b.reshape(1, ff), g.reshape(1, d))


def _ple_kernel(h_ref, p_ref, x_ref, wg_ref, wp_ref, g_ref, o_ref, hn_ref):
    gate = jax.nn.sigmoid(_dot(h_ref[...], wg_ref[...]))
    x = x_ref[...] + gate * _dot(p_ref[...].astype(MXU_DTYPE), wp_ref[...])
    o_ref[...] = x
    hn_ref[...] = _norm_rows(x, g_ref[...]).astype(hn_ref.dtype)


def _ple(h, p, x, w_gate, w_proj, g, norm_dtype, tm=512):
    m, d = h.shape
    row = lambda i: (i, 0)
    fixed = lambda i: (0, 0)
    once = pl.Buffered(1)
    return pl.pallas_call(
        _ple_kernel,
        name="ple",
        grid=(m // tm,),
        in_specs=[pl.BlockSpec((tm, d), row),
                  pl.BlockSpec((tm, p.shape[1]), row),
                  pl.BlockSpec((tm, d), row),
                  pl.BlockSpec((d, d), fixed, pipeline_mode=once),
                  pl.BlockSpec((p.shape[1], d), fixed, pipeline_mode=once),
                  pl.BlockSpec((1, d), fixed)],
        out_specs=[pl.BlockSpec((tm, d), row), pl.BlockSpec((tm, d), row)],
        out_shape=[jax.ShapeDtypeStruct((m, d), F32), jax.ShapeDtypeStruct((m, d), norm_dtype)],
        compiler_params=_cparams(("parallel",)),
    )(h, p, x, w_gate, w_proj, g.reshape(1, d))


def _compress_kernel(x_ref, w1_ref, w2_ref, pe_ref, o_ref):
    x = x_ref[0, 0]
    pe = pe_ref[0]
    half = x.shape[1]
    lo = _dot((x + pe[0:1]).astype(MXU_DTYPE), w1_ref[0, :half, :])
    hi = _dot((x + pe[1:2]).astype(MXU_DTYPE), w1_ref[0, half:, :])
    n = x.shape[0]
    hid = lo + pltpu.roll(hi, n - 1, 0)
    o_ref[0, 0, 0] = _dot(jax.nn.gelu(hid).astype(MXU_DTYPE), w2_ref[0]).astype(o_ref.dtype)


def _compress(kv_hm, w1, w2, pe):
    b, _, s, d = kv_hm.shape
    nrow = s // CMP_STRIDE
    x = kv_hm.reshape(b, 2 * NSA_GROUPS, nrow, CMP_STRIDE * d)
    return pl.pallas_call(
        _compress_kernel,
        name="nsa_compress",
        grid=(b, 2, NSA_GROUPS),
        in_specs=[pl.BlockSpec((1, 1, nrow, CMP_STRIDE * d), lambda bi, kv, g: (bi, kv * NSA_GROUPS + g, 0, 0)),
                  pl.BlockSpec((1,) + w1.shape[1:], lambda bi, kv, g: (kv, 0, 0)),
                  pl.BlockSpec((1,) + w2.shape[1:], lambda bi, kv, g: (kv, 0, 0)),
                  pl.BlockSpec((1, 2, CMP_STRIDE * d), lambda bi, kv, g: (kv, 0, 0))],
        out_specs=pl.BlockSpec((1, 1, 1, nrow, d), lambda bi, kv, g: (bi, kv, g, 0, 0)),
        out_shape=jax.ShapeDtypeStruct((b, 2, NSA_GROUPS, nrow, d), MXU_DTYPE),
        compiler_params=_cparams(("parallel", "parallel", "parallel")),
    )(x, w1, w2, pe)


def _softmax_pv(lg_ref, mx_ref, acc_ref, v_ref, v_index, n_tiles):
    rows = lg_ref.shape[0]
    m = jnp.max(mx_ref[...], axis=1, keepdims=True)
    mx_ref[...] = jnp.zeros_like(mx_ref)
    acc_ref[...] = jnp.zeros_like(acc_ref)

    def body(kt, carry):
        k0 = pl.multiple_of(kt * KT, KT)
        p = jnp.exp(lg_ref[:, pl.ds(k0, KT)] - m)
        part = p[:, 0:128]
        for c in range(1, KT // 128):
            part = part + p[:, c * 128:(c + 1) * 128]
        mx_ref[...] += part
        acc_ref[...] += _dot(p.astype(MXU_DTYPE), v_ref[v_index + (pl.ds(k0, KT), slice(None))])
        return carry

    lax.fori_loop(0, n_tiles, body, 0)
    l = jnp.sum(mx_ref[...], axis=1, keepdims=True)
    return acc_ref[...] / l


def _lane_max(s):
    part = s[:, 0:128]
    for c in range(1, s.shape[1] // 128):
        part = jnp.maximum(part, s[:, c * 128:(c + 1) * 128])
    return part


def _nsa_kernel(q_ref, kc_ref, vc_ref, kslc_ref, vslc_ref, kwin_ref, vwin_ref, gate_ref,
                wslc_ref, wwin_ref, tcmp_ref, ovl_ref, o_ref,
                kslc_aug, kwin_aug, vwin_pad, qaug, lg_ref, mx_ref, acc_ref, score_ref):
    i = pl.program_id(2)
    seq = kslc_ref.shape[2]
    d = HEAD_DIM
    rows = NSA_REP * TQ
    scale = d ** -0.5
    t0 = i * TQ

    @pl.when(i == 0)
    def _():
        kslc_aug[:, 0:d] = kslc_ref[0, 0]
        srow = lax.broadcasted_iota(jnp.int32, (seq, d), 0)
        lane = lax.broadcasted_iota(jnp.int32, (seq, d), 1)
        kslc_aug[:, d:2 * d] = jnp.where((srow >> 6) == lane, 1.0, 0.0).astype(kslc_aug.dtype)
        kwin_aug[0:NSA_WINDOW, 0:d] = jnp.zeros((NSA_WINDOW, d), kwin_aug.dtype)
        kwin_aug[NSA_WINDOW:, 0:d] = kwin_ref[0, 0]
        prow = lax.broadcasted_iota(jnp.int32, (seq + NSA_WINDOW, d), 0)
        plane = lax.broadcasted_iota(jnp.int32, (seq + NSA_WINDOW, d), 1)
        flag = jnp.where(prow < NSA_WINDOW, jnp.where(plane == MAX_SLC_BLOCKS, UNSELECTED, 0.0), 0.0)
        kwin_aug[:, d:2 * d] = flag.astype(kwin_aug.dtype)
        vwin_pad[0:NSA_WINDOW, :] = jnp.zeros((NSA_WINDOW, d), vwin_pad.dtype)
        vwin_pad[NSA_WINDOW:, :] = vwin_ref[0, 0]

    q4 = q_ref[0].reshape(rows, d)

    ncmp = kc_ref.shape[3]
    kc = kc_ref[0, 0, 0]
    vc = vc_ref[0, 0, 0]
    trow = t0 + lax.broadcasted_iota(jnp.int32, (TQ, ncmp), 0)
    cend = lax.broadcasted_iota(jnp.int32, (TQ, ncmp), 1) * CMP_STRIDE + (CMP_BLOCK - 1)
    valid_c = cend <= trow
    sc_all = _dot_nt(q4, kc) * scale
    c0 = i * (TQ // CMP_STRIDE)
    bias_start = pl.multiple_of(ncmp - 128 - 128 * (c0 // 128), 128)
    pb = []
    for r in range(NSA_REP):
        bias = tcmp_ref[0, r, :, pl.ds(bias_start, ncmp)]
        l = jnp.where(valid_c, sc_all[r * TQ:(r + 1) * TQ] + bias, MASKED)
        m = jnp.max(l, axis=1, keepdims=True)
        e = jnp.where(valid_c, jnp.exp(l - m), 0.0)
        p = e / jnp.maximum(jnp.sum(e, axis=1, keepdims=True), 1e-30)
        pb.append(p.astype(MXU_DTYPE))
    o_cmp = [_dot(pb[r], vc) for r in range(NSA_REP)]
    imp_t = _dot_nt(ovl_ref[...], pb[0])
    for r in range(1, NSA_REP):
        imp_t = imp_t + _dot_nt(ovl_ref[...], pb[r])

    span = NSA_WINDOW + TQ
    w0 = pl.multiple_of(t0, TQ)
    one_lane = lax.broadcasted_iota(jnp.int32, (rows, d), 1) == MAX_SLC_BLOCKS
    q_win = jnp.concatenate([q4, jnp.where(one_lane, 1.0, 0.0).astype(q4.dtype)], axis=1)
    s = _dot_nt(q_win, kwin_aug[pl.ds(w0, span), :]) * scale + wwin_ref[...].reshape(rows, span)
    m = jnp.max(s, axis=1, keepdims=True)
    p = jnp.exp(s - m)
    l = jnp.sum(p, axis=1, keepdims=True)
    o_win = _dot(p.astype(MXU_DTYPE), vwin_pad[pl.ds(w0, span), :]) / l

    nblk = MAX_SLC_BLOCKS
    jt = lax.broadcasted_iota(jnp.int32, (nblk, TQ), 0)
    tt = t0 + lax.broadcasted_iota(jnp.int32, (nblk, TQ), 1)
    cur = tt >> 6
    imp_t = imp_t[0:nblk]
    score = jnp.where(jt == 0, 1e9, jnp.where(jt == cur, 1e9, jnp.where(jt == cur - 1, 1e9, imp_t)))
    score = jnp.where(jt * SLC_BLOCK <= tt, score, -jnp.inf)
    score_ref[...] = score
    sub = lax.broadcasted_iota(jnp.int32, (8, TQ), 0)
    sv = [score[8 * v:8 * v + 8] for v in range(nblk // 8)]
    beaten = [jnp.zeros((8, TQ), F32) for _ in sv]
    for jp in range(nblk):
        other = score_ref[jp:jp + 1, :]
        for v in range(nblk // 8):
            if 8 * v > jp:
                hit = other >= sv[v]
            elif 8 * v + 7 <= jp:
                hit = other > sv[v]
            else:
                tie_loses = jnp.where(sub > jp - 8 * v, 1.0, 0.0)
                beaten[v] = beaten[v] + jnp.where(other == sv[v], tie_loses, 0.0)
                hit = other > sv[v]
            beaten[v] = beaten[v] + jnp.where(hit, 1.0, 0.0)
    aug_t = jnp.concatenate([jnp.where(b < SLC_TOPN, 0.0, UNSELECTED) for b in beaten], axis=0)
    row = lax.broadcasted_iota(jnp.int32, (d - nblk, TQ), 0)
    aug_t = jnp.concatenate([aug_t, jnp.where(row == 0, 1.0, 0.0)], axis=0)
    aug = aug_t.T
    qaug[:, 0:d] = q4
    for r in range(NSA_REP):
        qaug[r * TQ:(r + 1) * TQ, d:2 * d] = aug.astype(qaug.dtype)
    qa = qaug[...]

    n_tiles = (t0 + TQ + KT - 1) // KT
    off = t0 - (n_tiles - 1) * KT
    mx_ref[...] = jnp.full(mx_ref.shape, MASKED, F32)

    def slc_tile(kt, tmpl_start):
        k0 = pl.multiple_of(kt * KT, KT)
        s = _dot_nt(qa, kslc_aug[pl.ds(k0, KT), :]) * scale
        if tmpl_start is not None:
            ts = pl.multiple_of(tmpl_start, 128)
            s = s + wslc_ref[:, :, pl.ds(ts, KT)].reshape(rows, KT)
        lg_ref[:, pl.ds(k0, KT)] = s
        mx_ref[...] = jnp.maximum(mx_ref[...], _lane_max(s))

    def far_body(kt, carry):
        slc_tile(kt, None)
        return carry

    lax.fori_loop(0, jnp.maximum(n_tiles - 2, 0), far_body, 0)

    @pl.when(n_tiles >= 2)
    def _():
        slc_tile(n_tiles - 2, TMPL_C0 - off - KT)

    slc_tile(n_tiles - 1, TMPL_C0 - off)
    o_slc = _softmax_pv(lg_ref, mx_ref, acc_ref, vslc_ref, (0, 0), n_tiles)

    g = jax.nn.sigmoid(gate_ref[...])
    for r in range(NSA_REP):
        rs = slice(r * TQ, (r + 1) * TQ)
        o = (g[:, 3 * r:3 * r + 1] * o_cmp[r] + g[:, 3 * r + 1:3 * r + 2] * o_slc[rs]
             + g[:, 3 * r + 2:3 * r + 3] * o_win[rs])
        o_ref[:, r * d:(r + 1) * d] = o.astype(o_ref.dtype)


def _nsa(hm, cmp_kv, misc, wslc, wwin, tcmp, ovl):
    b, _, seq, d = hm.shape
    nq = seq // TQ
    rows = NSA_REP * TQ
    ncmp = cmp_kv.shape[3]
    kv_spec = lambda head: pl.BlockSpec((1, 1, seq, d), lambda bi, g, i: (bi, head + g, 0, 0))
    once = pl.Buffered(1)
    return pl.pallas_call(
        _nsa_kernel,
        name="nsa",
        grid=(b, NSA_GROUPS, nq),
        in_specs=[pl.BlockSpec((1, NSA_REP, TQ, d), lambda bi, g, i: (bi, g, i, 0)),
                  pl.BlockSpec((1, 1, 1, ncmp, d), lambda bi, g, i: (bi, 0, g, 0, 0)),
                  pl.BlockSpec((1, 1, 1, ncmp, d), lambda bi, g, i: (bi, 1, g, 0, 0)),
                  kv_spec(HM_K_SLC), kv_spec(HM_V_SLC), kv_spec(HM_K_WIN), kv_spec(HM_V_WIN),
                  pl.BlockSpec((TQ, 128), lambda bi, g, i: (bi * nq + i, MISC_GATE + g)),
                  pl.BlockSpec((NSA_REP, TQ, TMPL_W), lambda bi, g, i: (g, 0, 0), pipeline_mode=once),
                  pl.BlockSpec((NSA_REP, TQ, NSA_WINDOW + TQ), lambda bi, g, i: (g, 0, 0), pipeline_mode=once),
                  pl.BlockSpec((1, NSA_REP) + tcmp.shape[2:], lambda bi, g, i: (i % tcmp.shape[0], g, 0, 0)),
                  pl.BlockSpec(ovl.shape, lambda bi, g, i: (0, 0), pipeline_mode=once)],
        out_specs=pl.BlockSpec((TQ, NSA_REP * d), lambda bi, g, i: (bi * nq + i, g)),
        out_shape=jax.ShapeDtypeStruct((b * seq, NSA_HEADS * d), MXU_DTYPE),
        scratch_shapes=[pltpu.VMEM((seq, 2 * d), MXU_DTYPE),
                        pltpu.VMEM((seq + NSA_WINDOW, 2 * d), MXU_DTYPE),
                        pltpu.VMEM((seq + NSA_WINDOW, d), MXU_DTYPE),
                        pltpu.VMEM((rows, 2 * d), MXU_DTYPE),
                        pltpu.VMEM((rows, seq), F32),
                        pltpu.VMEM((rows, 128), F32),
                        pltpu.VMEM((rows, d), F32),
                        pltpu.VMEM((MAX_SLC_BLOCKS, TQ), F32)],
        compiler_params=_cparams(("parallel", "parallel", "arbitrary")),
    )(hm, cmp_kv, cmp_kv, hm, hm, hm, hm, misc, wslc, wwin, tcmp, ovl)


def _sb_kernel(q_ref, k_ref, v_ref, upper_ref, o_ref, acc_ref, carry_ref):
    i = pl.program_id(1)
    t = SB_T
    d = HEAD_DIM
    scale = d ** -0.5
    upper = upper_ref[...]
    acc_ref[...] = jnp.zeros_like(acc_ref)
    carry_ref[...] = jnp.zeros_like(carry_ref)

    heads = range(SB_HEADS)

    def tile(kt, mask):
        k0 = pl.multiple_of(kt * t, t)
        z = [_dot_nt(q_ref[0, h], k_ref[0, h, pl.ds(k0, t), :]) * scale for h in heads]
        ls = [jnp.minimum(z[h], 0.0) - jnp.log(1.0 + jnp.exp(-jnp.abs(z[h]))) for h in heads]
        lk = [ls[h] - z[h] for h in heads]
        if mask is not None:
            lk = [jnp.where(mask, lk[h], 0.0) for h in heads]
        hi = [lk[h].astype(MXU_DTYPE) for h in heads]
        r1 = [lk[h] - hi[h].astype(F32) for h in heads]
        mid = [r1[h].astype(MXU_DTYPE) for h in heads]
        lo = [(r1[h] - mid[h].astype(F32)).astype(MXU_DTYPE) for h in heads]
        after = [carry_ref[h] + (_dot(hi[h], upper) + _dot(mid[h], upper) + _dot(lo[h], upper)) for h in heads]
        w = [jnp.exp(ls[h] + after[h]) for h in heads]
        if mask is not None:
            w = [jnp.where(mask, w[h], 0.0) for h in heads]
        for h in heads:
            acc_ref[h] += _dot(w[h].astype(MXU_DTYPE), v_ref[0, h, pl.ds(k0, t), :])
            carry_ref[h] += jnp.sum(lk[h], axis=1, keepdims=True)

    strict = lax.broadcasted_iota(jnp.int32, (t, t), 1) < lax.broadcasted_iota(jnp.int32, (t, t), 0)
    tile(i, strict)

    def body(n, carry):
        tile(i - 1 - n, None)
        return carry

    lax.fori_loop(0, i, body, 0)
    for h in range(SB_HEADS):
        o_ref[:, h * d:(h + 1) * d] = acc_ref[h].astype(o_ref.dtype)


def _stick_breaking(hm, upper):
    b, _, seq, d = hm.shape
    nq = seq // SB_T
    return pl.pallas_call(
        _sb_kernel,
        grid=(b, nq),
        in_specs=[pl.BlockSpec((1, SB_HEADS, SB_T, d), lambda bi, i: (bi, HM_SB_Q // SB_HEADS, i, 0)),
                  pl.BlockSpec((1, SB_HEADS, seq, d), lambda bi, i: (bi, HM_SB_K // SB_HEADS, 0, 0)),
                  pl.BlockSpec((1, SB_HEADS, seq, d), lambda bi, i: (bi, HM_SB_V // SB_HEADS, 0, 0)),
                  pl.BlockSpec(upper.shape, lambda bi, i: (0, 0))],
        out_specs=pl.BlockSpec((SB_T, SB_HEADS * d), lambda bi, i: (bi * nq + i, 0)),
        out_shape=jax.ShapeDtypeStruct((b * seq, SB_HEADS * d), MXU_DTYPE),
        scratch_shapes=[pltpu.VMEM((SB_HEADS, SB_T, d), F32), pltpu.VMEM((SB_HEADS, SB_T, 1), F32)],
        name="stick_breaking",
        compiler_params=_cparams(("parallel", "parallel")),
    )(hm, hm, hm, upper)


def _dsa_kv_kernel(c_ref, g_ref, wk_ref, wv_ref, k_ref, v_ref):
    c = c_ref[...]
    y = c * lax.rsqrt(jnp.mean(c * c, axis=-1, keepdims=True) + EPS)
    y = (y * g_ref[...]).astype(MXU_DTYPE)
    k_ref[...] = _dot(y, wk_ref[...]).astype(k_ref.dtype)
    v_ref[...] = _dot(y, wv_ref[...]).astype(v_ref.dtype)


def _dsa_kv(misc, kv_norm, w_uk, w_uv, tm=512):
    m = misc.shape[0]
    r = DSA_KV_RANK
    out = jax.ShapeDtypeStruct((m, HEAD_DIM), MXU_DTYPE)
    return pl.pallas_call(
        _dsa_kv_kernel,
        name="dsa_kv",
        grid=(m // tm,),
        in_specs=[pl.BlockSpec((tm, r), lambda i: (i, MISC_CKV * 128 // r)),
                  pl.BlockSpec((1, r), lambda i: (0, 0)),
                  pl.BlockSpec((r, HEAD_DIM), lambda i: (0, 0)),
                  pl.BlockSpec((r, HEAD_DIM), lambda i: (0, 0))],
        out_specs=[pl.BlockSpec((tm, HEAD_DIM), lambda i: (i, 0)), pl.BlockSpec((tm, HEAD_DIM), lambda i: (i, 0))],
        out_shape=[out, out],
        compiler_params=_cparams(("parallel",)),
    )(misc, kv_norm.reshape(1, r), w_uk, w_uv)


PLANE_GROUPS_PER_TILE = KT // 256


def _bit_planes(words):
    a = list(words)
    j, m = 16, 0x0000FFFF
    while j:
        sh = jnp.full(a[0].shape, j, jnp.int32)
        for k in range(32):
            if not k & j:
                t = (a[k] ^ lax.shift_right_logical(a[k + j], sh)) & m
                a[k] = a[k] ^ t
                a[k + j] = a[k + j] ^ (t << j)
        j >>= 1
        m = (m ^ (m << j)) & 0xFFFFFFFF
    return a


def _dsa_kernel(q_ref, k_ref, v_ref, iq_ref, ik_ref, iw_ref, wd_ref, low_ref, o_ref,
                key_ref, plane_ref, alive_ref, add_ref, lg_ref, mx_ref, acc_ref, *, n_keep):
    i = pl.program_id(1)
    d = HEAD_DIM
    tq = DSA_TQ
    rows = DSA_HEADS * tq
    scale = d ** -0.5
    t0 = i * tq
    n_tiles = (t0 + tq + KT - 1) // KT
    off = t0 - (n_tiles - 1) * KT

    iq = iq_ref[...].astype(MXU_DTYPE)
    qh = [iq[:, h * IDX_DIM:(h + 1) * IDX_DIM] for h in range(IDX_HEADS)]
    wi_t = iw_ref[...].T * (IDX_HEADS ** -0.5) * (IDX_DIM ** -0.5)
    wh = [wi_t[IDX_DIM + h:IDX_DIM + h + 1, :] for h in range(IDX_HEADS)]
    tq_row = t0 + lax.broadcasted_iota(jnp.int32, (KT, tq), 1)
    krow = lax.broadcasted_iota(jnp.int32, (KT, tq), 0)

    def column_sum(a):
        return jnp.sum(a.reshape(KT // 8, 8, tq), axis=0)

    def score_tile(kt, causal):
        k0 = pl.multiple_of(kt * KT, KT)
        ki = ik_ref[pl.ds(k0, KT), 0:IDX_DIM].astype(MXU_DTYPE)
        dots = [_dot_nt(ki, qh[h]) for h in range(IDX_HEADS)]
        sc = wh[0] * jnp.maximum(dots[0], 0.0)
        for h in range(1, IDX_HEADS):
            sc = sc + wh[h] * jnp.maximum(dots[h], 0.0)
        sc = sc + 0.0
        if causal:
            sc = jnp.where(k0 + krow <= tq_row, sc, -jnp.inf)
        bits = lax.bitcast_convert_type(sc, jnp.int32)
        key = bits ^ ((bits >> 31) & 0x7FFFFFFF)
        key_ref[pl.ds(k0, KT), :] = key
        ukey = key ^ INT_MIN
        for g in range(PLANE_GROUPS_PER_TILE):
            words = [ukey[(32 * g + w) * 8:(32 * g + w + 1) * 8] for w in range(32)]
            for x, plane in enumerate(_bit_planes(words)):
                plane_ref[x, PLANE_GROUPS_PER_TILE * kt + g] = plane

    @pl.when((pl.program_id(0) == 0) & (i == 0))
    def _():
        plane_ref[...] = jnp.zeros_like(plane_ref)

    def score_body(kt, carry):
        score_tile(kt, False)
        return carry

    lax.fori_loop(0, n_tiles - 1, score_body, 0)
    score_tile(n_tiles - 1, True)

    ngrp = alive_ref.shape[0]
    for g in range(ngrp):
        alive_ref[g] = jnp.where(g < PLANE_GROUPS_PER_TILE * n_tiles, -1, 0) + jnp.zeros((8, tq), jnp.int32)

    def bit_body(x, carry):
        thr_u, remaining = carry
        ones = [alive_ref[g] & plane_ref[x, g] for g in range(ngrp)]
        c = lax.population_count(ones[0])
        for g in range(1, ngrp):
            c = c + lax.population_count(ones[g])
        c = jnp.sum(c, axis=0, keepdims=True)
        take = c >= remaining
        for g in range(ngrp):
            alive_ref[g] = jnp.where(take, ones[g], alive_ref[g] ^ ones[g])
        bit = jnp.int32(1) << (31 - x)
        return jnp.where(take, thr_u | bit, thr_u), jnp.where(take, remaining, remaining - c)

    thr_u, need = lax.fori_loop(0, 32, bit_body,
                                (jnp.zeros((1, tq), jnp.int32), jnp.full((1, tq), n_keep, jnp.int32)))
    thr = thr_u ^ INT_MIN
    need = need.astype(F32)

    def mask_body(kt, seen):
        k0 = pl.multiple_of(kt * KT, KT)
        kk = key_ref[pl.ds(k0, KT), :]
        eq = jnp.where(kk == thr, 1.0, 0.0)
        before = seen + _dot(low_ref[...], eq.astype(MXU_DTYPE))
        tie = jnp.where(before < need, 0.0, MASKED)
        add_t = jnp.where(kk > thr, 0.0, jnp.where(kk == thr, tie, MASKED))
        add_ref[:, pl.ds(k0, KT)] = add_t.T
        return seen + jnp.sum(column_sum(eq), axis=0, keepdims=True)

    lax.fori_loop(0, n_tiles, mask_body, jnp.zeros((1, tq), F32))

    q4 = q_ref[0].reshape(rows, d)
    mx_ref[...] = jnp.full(mx_ref.shape, MASKED, F32)

    def att_tile(kt, tmpl_start):
        k0 = pl.multiple_of(kt * KT, KT)
        s = _dot_nt(q4, k_ref[0, pl.ds(k0, KT), :]) * scale
        addm = add_ref[:, pl.ds(k0, KT)]
        s = s + jnp.concatenate([addm] * DSA_HEADS, axis=0)
        if tmpl_start is not None:
            ts = pl.multiple_of(tmpl_start, 128)
            s = s + wd_ref[:, :, pl.ds(ts, KT)].reshape(rows, KT)
        lg_ref[:, pl.ds(k0, KT)] = s
        mx_ref[...] = jnp.maximum(mx_ref[...], _lane_max(s))

    def far_body(kt, carry):
        att_tile(kt, None)
        return carry

    lax.fori_loop(0, jnp.maximum(n_tiles - 2, 0), far_body, 0)

    @pl.when(n_tiles >= 2)
    def _():
        att_tile(n_tiles - 2, TMPL_C0 - off - KT)

    att_tile(n_tiles - 1, TMPL_C0 - off)
    o = _softmax_pv(lg_ref, mx_ref, acc_ref, v_ref, (0,), n_tiles)
    for r in range(DSA_HEADS):
        o_ref[:, r * d:(r + 1) * d] = o[r * tq:(r + 1) * tq].astype(o_ref.dtype)


def _dsa(hm, k, v, misc, wdsa, low):
    b, _, seq, d = hm.shape
    tq = DSA_TQ
    nq = seq // tq
    rows = DSA_HEADS * tq
    k = k.reshape(b, seq, d)
    v = v.reshape(b, seq, d)
    n_keep = min(DSA_TOPK, seq // 4)
    ngrp = PLANE_GROUPS_PER_TILE * (seq // KT)
    return pl.pallas_call(
        functools.partial(_dsa_kernel, n_keep=n_keep),
        name="dsa",
        grid=(b, nq),
        in_specs=[pl.BlockSpec((1, DSA_HEADS, tq, d), lambda bi, i: (bi, HM_DSA_Q // DSA_HEADS, i, 0)),
                  pl.BlockSpec((1, seq, d), lambda bi, i: (bi, 0, 0)),
                  pl.BlockSpec((1, seq, d), lambda bi, i: (bi, 0, 0)),
                  pl.BlockSpec((tq, IDX_HEADS * IDX_DIM), lambda bi, i: (bi * nq + i, MISC_IDXQ)),
                  pl.BlockSpec((seq, 128), lambda bi, i: (bi, MISC_IDXK)),
                  pl.BlockSpec((tq, 128), lambda bi, i: (bi * nq + i, MISC_IDXK)),
                  pl.BlockSpec((DSA_HEADS, tq, TMPL_W), lambda bi, i: (0, 0, 0), pipeline_mode=pl.Buffered(1)),
                  pl.BlockSpec(low.shape, lambda bi, i: (0, 0), pipeline_mode=pl.Buffered(1))],
        out_specs=pl.BlockSpec((tq, DSA_HEADS * d), lambda bi, i: (bi * nq + i, 0)),
        out_shape=jax.ShapeDtypeStruct((b * seq, DSA_HEADS * d), MXU_DTYPE),
        scratch_shapes=[pltpu.VMEM((seq, tq), jnp.int32),
                        pltpu.VMEM((32, ngrp, 8, tq), jnp.int32),
                        pltpu.VMEM((ngrp, 8, tq), jnp.int32),
                        pltpu.VMEM((tq, seq), F32),
                        pltpu.VMEM((rows, seq), F32),
                        pltpu.VMEM((rows, 128), F32),
                        pltpu.VMEM((rows, d), F32)],
        compiler_params=_cparams(("arbitrary", "arbitrary")),
    )(hm, k, v, misc, misc, misc, wdsa, low)


def _t5_bucket(dist):
    n = jnp.maximum(dist, 0)
    max_exact = REL_BUCKETS // 2
    nf = jnp.maximum(n, 1).astype(F32)
    large = max_exact + (jnp.log(nf / max_exact) / math.log(REL_MAX_DIST / max_exact)
                         * (REL_BUCKETS - max_exact)).astype(jnp.int32)
    large = jnp.minimum(large, REL_BUCKETS - 1)
    return jnp.where(n < max_exact, n, large)


def _bias_templates(rel_tab, ncmp):
    far = REL_MAX_DIST
    by_dist = rel_tab[_t5_bucket(jnp.arange(far + 1))] - rel_tab[REL_BUCKETS - 1][None, :]
    by_dist = by_dist.T

    def build(dist, valid, fill=MASKED):
        t = by_dist[:, np.clip(dist, 0, far)]
        return jnp.where(valid[None], t, fill).astype(F32)

    def toeplitz(u, nrows, width):
        nh, l = u.shape
        return jnp.tile(u, (1, nrows))[:, :nrows * (l - 1)].reshape(nh, nrows, l - 1)[:, :, :width]

    def diagonals(nrows, width):
        l = width + nrows
        k = np.arange(l)
        return np.where(k < width, k, k - l)

    def causal(heads, nrows):
        dist = TMPL_C0 - diagonals(nrows, TMPL_W)
        return toeplitz(build(dist, dist >= 0)[heads], nrows, TMPL_W)

    nsa = slice(0, NSA_HEADS)
    span = NSA_WINDOW + TQ
    dist = NSA_WINDOW - diagonals(TQ, span)
    window = toeplitz(build(dist, (dist >= 0) & (dist < NSA_WINDOW))[nsa], TQ, span)
    cc = np.arange(CMP_BAND)[None, :] - CMP_BAND // 2
    dist = np.arange(TQ)[:, None] - CMP_STRIDE * cc - (CMP_BLOCK - 1)
    band = build(dist, dist >= 0, 0.0)[nsa]
    width = 2 * ncmp - 128
    step = TQ // CMP_STRIDE
    cmp = []
    for v in range(128 // step):
        left = ncmp - 128 + step * v
        canvas = jnp.pad(band, ((0, 0), (0, 0), (left, width - left)))
        cmp.append(canvas[:, :, CMP_BAND // 2:CMP_BAND // 2 + width])
    return causal(nsa, TQ), causal(slice(NSA_HEADS, None), DSA_TQ), window, jnp.stack(cmp)


def _pack_w_in(w_in):
    d3 = 3 * w_in.shape[1]
    kv = NSA_GROUPS * HEAD_DIM
    o_q = d3
    o_kc = o_q + NSA_HEADS * HEAD_DIM
    o_vc, o_ks, o_vs, o_kw, o_vw = (o_kc + j * kv for j in range(1, 6))
    o_g = o_vw + kv
    o_sbq = o_g + 3 * NSA_HEADS
    o_sbk = o_sbq + SB_HEADS * HEAD_DIM
    o_sbv = o_sbk + SB_HEADS * HEAD_DIM
    o_dq = o_sbv + SB_HEADS * HEAD_DIM
    o_ckv = o_dq + DSA_HEADS * HEAD_DIM
    o_iq = o_ckv + DSA_KV_RANK
    o_ik = o_iq + IDX_HEADS * IDX_DIM
    o_iw = o_ik + IDX_DIM
    c = lambda a, n: w_in[:, :, a:a + n]
    zeros = lambda n: jnp.zeros(w_in.shape[:2] + (n,), w_in.dtype)
    w_gates = c(0, d3)
    w_hm = jnp.concatenate([c(o_q, NSA_HEADS * HEAD_DIM), c(o_ks, kv), c(o_vs, kv), c(o_kw, kv), c(o_vw, kv),
                            c(o_sbq, 3 * SB_HEADS * HEAD_DIM), c(o_dq, DSA_HEADS * HEAD_DIM)], axis=2)
    w_cmp = c(o_kc, 2 * kv)
    gw = 3 * NSA_REP
    w_misc = jnp.concatenate([c(o_iq, IDX_HEADS * IDX_DIM), c(o_ckv, DSA_KV_RANK),
                              c(o_ik, IDX_DIM), c(o_iw, IDX_HEADS), zeros(128 - IDX_DIM - IDX_HEADS),
                              c(o_g, gw), zeros(128 - gw), c(o_g + gw, gw), zeros(128 - gw)], axis=2)
    return tuple(w.astype(MXU_DTYPE) for w in (w_gates, w_hm, w_cmp, w_misc))


def kernel(x, p, w_in, norm_mix, norm_ffn, norm_ple, norm_final, w_proj_a, w_proj_b, w_proj_c, w_out,
           cmp_k_w1, cmp_k_w2, cmp_k_pe, cmp_v_w1, cmp_v_w2, cmp_v_pe, dsa_kv_norm, dsa_w_uk, dsa_w_uv,
           rel_bias_table, ffn_w_gate, ffn_w_up, ffn_w_down, ffn_conv_w, ffn_conv_b, ple_w_gate, ple_w_proj):
    batch, seq, d_model = x.shape
    depth = w_in.shape[0]
    m = batch * seq
    assert seq % KT == 0 and seq % SB_T == 0 and seq // SLC_BLOCK <= MAX_SLC_BLOCKS
    ncmp = seq // CMP_STRIDE
    bf = lambda w: w.astype(MXU_DTYPE)

    w_gates, w_hm, w_cmp, w_misc = _pack_w_in(w_in)
    w_a, w_b, w_c, w_o = bf(w_proj_a), bf(w_proj_b), bf(w_proj_c), bf(w_out)
    cmp_w1 = bf(jnp.stack([cmp_k_w1, cmp_v_w1], axis=1))
    cmp_w2 = bf(jnp.stack([cmp_k_w2, cmp_v_w2], axis=1))
    cmp_pe = jnp.stack([cmp_k_pe, cmp_v_pe], axis=1).reshape(depth, 2, 2, CMP_STRIDE * HEAD_DIM)
    w_uk, w_uv = bf(dsa_w_uk), bf(dsa_w_uv)
    f_gate, f_up, f_down = bf(ffn_w_gate), bf(ffn_w_up), bf(ffn_w_down)
    pl_gate, pl_proj = bf(ple_w_gate), bf(ple_w_proj)

    wslc, wdsa, wwin, tcmp = _bias_templates(rel_bias_table, ncmp)
    cc = np.arange(ncmp)[None, :]
    jj = np.arange(128)[:, None]
    per = SLC_BLOCK // CMP_STRIDE
    ovl = ((cc >= per * jj - (CMP_BLOCK // CMP_STRIDE - 1)) & (cc <= per * jj + per - 1)
           & (cc < ncmp - 1) & (jj < seq // SLC_BLOCK))
    ovl = jnp.asarray(ovl, MXU_DTYPE)
    low = jnp.asarray(np.arange(KT)[:, None] > np.arange(KT)[None, :], MXU_DTYPE)
    upper = jnp.asarray(np.arange(SB_T)[:, None] > np.arange(SB_T)[None, :], MXU_DTYPE)

    x = x.reshape(m, d_model)
    p = p.reshape(depth, m, p.shape[-1])
    h = _rmsnorm(x, norm_mix[0], MXU_DTYPE)
    for i in range(depth):
        gates = _matmul(h, w_gates[i], F32, 1024, 512, "in_proj_gates")
        hm = _matmul_heads(h, w_hm[i], batch, MXU_DTYPE, "in_proj_heads")
        cmp_in = _matmul_heads(h, w_cmp[i], batch, F32, "in_proj_cmp")
        misc = _matmul(h, w_misc[i], F32, 1024, MISC_COLS // 3, "in_proj_misc")
        cmp_kv = _compress(cmp_in, cmp_w1[i], cmp_w2[i], cmp_pe[i])
        o_a = _nsa(hm, cmp_kv, misc, wslc, wwin, tcmp, ovl)
        o_b = _stick_breaking(hm, upper)
        dk, dv = _dsa_kv(misc, dsa_kv_norm[i], w_uk[i], w_uv[i])
        o_c = _dsa(hm, dk, dv, misc, wdsa, low)
        y = _merge(o_a, o_b, o_c, w_a[i], w_b[i], w_c[i], gates)
        x, h = _matmul_residual(y, w_o[i], x, norm_ffn[i])
        x, h = _conv_ffn(h, x, f_gate[i], f_up[i], f_down[i], ffn_conv_w[i], ffn_conv_b[i], norm_ple[i], seq)
        last = i == depth - 1
        x, h = _ple(h, p[i], x, pl_gate[i], pl_proj[i], norm_final if last else norm_mix[i + 1],
                    F32 if last else MXU_DTYPE)
    return h.reshape(batch, seq, d_model)
```

```python
import functools
import math

import numpy as np
import jax
import jax.numpy as jnp
from jax import lax
from jax.experimental import pallas as pl
from jax.experimental.pallas import tpu as pltpu

F32 = jnp.float32
MXU_DTYPE = jnp.bfloat16

HEAD_DIM = 128
NSA_HEADS = 8
NSA_GROUPS = 2
NSA_REP = NSA_HEADS // NSA_GROUPS
CMP_BLOCK = 32
CMP_STRIDE = 16
SLC_BLOCK = 64
MAX_SLC_BLOCKS = 64
SLC_TOPN = 16
NSA_WINDOW = 512
SB_HEADS = 4
DSA_HEADS = 4
DSA_KV_RANK = 256
IDX_HEADS = 8
IDX_DIM = 64
DSA_TOPK = 256
REL_BUCKETS = 32
REL_MAX_DIST = 128
CONV_WIDTH = 3
EPS = 1e-6

MASKED = -1e30
UNSELECTED = -1e9
INT_MIN = -2 ** 31
VMEM_LIMIT = 56 * 1024 * 1024

TQ = 256
DSA_TQ = 256
KT = 512
SB_T = 256
TMPL_C0 = 896
TMPL_W = TMPL_C0 + KT
PROJ_TN = 512
FFN_TF = 512
CMP_BAND = 32

HM_NSA_Q, HM_K_SLC, HM_V_SLC, HM_K_WIN, HM_V_WIN = 0, 8, 10, 12, 14
HM_SB_Q, HM_SB_K, HM_SB_V, HM_DSA_Q, HM_HEADS = 16, 20, 24, 28, 32
MISC_IDXQ, MISC_CKV, MISC_IDXK, MISC_GATE, MISC_COLS = 0, 4, 6, 7, 9 * 128


def _cparams(sem):
    return pltpu.CompilerParams(dimension_semantics=sem, vmem_limit_bytes=VMEM_LIMIT)


def _dot(a, b):
    return jnp.dot(a, b, preferred_element_type=F32)


def _dot_nt(a, b):
    return lax.dot_general(a, b, (((1,), (1,)), ((), ())), preferred_element_type=F32)


def _rmsnorm_kernel(x_ref, g_ref, o_ref):
    x = x_ref[...]
    y = x * lax.rsqrt(jnp.mean(x * x, axis=-1, keepdims=True) + EPS)
    o_ref[...] = (y * g_ref[...]).astype(o_ref.dtype)


def _rmsnorm(x, g, out_dtype, tm=512):
    m, d = x.shape
    return pl.pallas_call(
        _rmsnorm_kernel,
        grid=(m // tm,),
        in_specs=[pl.BlockSpec((tm, d), lambda i: (i, 0)), pl.BlockSpec((1, d), lambda i: (0, 0))],
        out_specs=pl.BlockSpec((tm, d), lambda i: (i, 0)),
        out_shape=jax.ShapeDtypeStruct((m, d), out_dtype),
        name="rmsnorm",
        compiler_params=_cparams(("parallel",)),
    )(x, g.reshape(1, d))


def _col_blocks(w, tn):
    *lead, k, n = w.shape
    w = w.reshape(*lead, k, n // tn, tn)
    return jnp.moveaxis(w, -2, -3)


def _mm_kernel(a_ref, w_ref, o_ref):
    o_ref[...] = _dot(a_ref[...], w_ref[0]).astype(o_ref.dtype)


def _matmul(a, w, out_dtype, tm, name):
    m, k = a.shape
    nb, _, tn = w.shape
    return pl.pallas_call(
        _mm_kernel,
        name=name,
        grid=(m // tm, nb),
        in_specs=[pl.BlockSpec((tm, k), lambda i, j: (i, 0)), pl.BlockSpec((1, k, tn), lambda i, j: (j, 0, 0))],
        out_specs=pl.BlockSpec((tm, tn), lambda i, j: (i, j)),
        out_shape=jax.ShapeDtypeStruct((m, nb * tn), out_dtype),
        compiler_params=_cparams(("parallel", "parallel")),
    )(a, w)


def _mm_heads_kernel(a_ref, w_ref, o_ref, *, hb):
    r = _dot(a_ref[...], w_ref[0])
    for j in range(hb):
        o_ref[0, j] = r[:, j * HEAD_DIM:(j + 1) * HEAD_DIM].astype(o_ref.dtype)


def _matmul_heads(a, w, batch, out_dtype, name, tm=1024):
    m, k = a.shape
    s = m // batch
    nb, _, tn = w.shape
    hb = tn // HEAD_DIM
    nh = nb * hb
    spb = s // tm
    return pl.pallas_call(
        functools.partial(_mm_heads_kernel, hb=hb),
        name=name,
        grid=(batch, spb, nb),
        in_specs=[pl.BlockSpec((tm, k), lambda b, i, j: (b * spb + i, 0)),
                  pl.BlockSpec((1, k, tn), lambda b, i, j: (j, 0, 0))],
        out_specs=pl.BlockSpec((1, hb, tm, HEAD_DIM), lambda b, i, j: (b, j, i, 0)),
        out_shape=jax.ShapeDtypeStruct((batch, nh, s, HEAD_DIM), out_dtype),
        compiler_params=_cparams(("parallel", "parallel", "parallel")),
    )(a, w)


def _norm_rows(x, g):
    return x * lax.rsqrt(jnp.mean(x * x, axis=-1, keepdims=True) + EPS) * g


def _mm_res_kernel(a_ref, w_ref, x_ref, g_ref, o_ref, h_ref):
    x = x_ref[...] + _dot(a_ref[...], w_ref[...])
    o_ref[...] = x
    h_ref[...] = _norm_rows(x, g_ref[...]).astype(h_ref.dtype)


def _matmul_residual(a, w, x, g, tm=512):
    m, k = a.shape
    n = w.shape[1]
    row = lambda i: (i, 0)
    fixed = lambda i: (0, 0)
    return pl.pallas_call(
        _mm_res_kernel,
        name="out_proj_residual",
        grid=(m // tm,),
        in_specs=[pl.BlockSpec((tm, k), row),
                  pl.BlockSpec((k, n), fixed, pipeline_mode=pl.Buffered(1)),
                  pl.BlockSpec((tm, n), row),
                  pl.BlockSpec((1, n), fixed)],
        out_specs=[pl.BlockSpec((tm, n), row), pl.BlockSpec((tm, n), row)],
        out_shape=[jax.ShapeDtypeStruct((m, n), F32), jax.ShapeDtypeStruct((m, n), MXU_DTYPE)],
        compiler_params=_cparams(("parallel",)),
    )(a, w, x, g.reshape(1, n))


def _merge_kernel(oa_ref, ob_ref, oc_ref, wa_ref, wb_ref, wc_ref, ga_ref, gb_ref, gc_ref, y_ref):
    y = jax.nn.sigmoid(ga_ref[...]) * _dot(oa_ref[...], wa_ref[...])
    y += jax.nn.sigmoid(gb_ref[...]) * _dot(ob_ref[...], wb_ref[...])
    y += jax.nn.sigmoid(gc_ref[...]) * _dot(oc_ref[...], wc_ref[...])
    y_ref[...] = y.astype(y_ref.dtype)


def _merge(o_a, o_b, o_c, w_a, w_b, w_c, gates, tm=1024, tn=512):
    m = o_a.shape[0]
    d = w_a.shape[1]
    nb = d // tn
    row = lambda i, j: (i, 0)
    col = lambda i, j: (0, j)
    return pl.pallas_call(
        _merge_kernel,
        name="branch_merge",
        grid=(m // tm, nb),
        in_specs=[pl.BlockSpec((tm, o_a.shape[1]), row), pl.BlockSpec((tm, o_b.shape[1]), row),
                  pl.BlockSpec((tm, o_c.shape[1]), row),
                  pl.BlockSpec((w_a.shape[0], tn), col), pl.BlockSpec((w_b.shape[0], tn), col),
                  pl.BlockSpec((w_c.shape[0], tn), col),
                  pl.BlockSpec((tm, tn), lambda i, j: (i, j)),
                  pl.BlockSpec((tm, tn), lambda i, j: (i, nb + j)),
                  pl.BlockSpec((tm, tn), lambda i, j: (i, 2 * nb + j))],
        out_specs=pl.BlockSpec((tm, tn), lambda i, j: (i, j)),
        out_shape=jax.ShapeDtypeStruct((m, d), MXU_DTYPE),
        compiler_params=_cparams(("parallel", "parallel")),
    )(o_a, o_b, o_c, w_a, w_b, w_c, gates, gates, gates)


FFN_HALO = 16


def _ffn_kernel(h_ref, hp_ref, x_ref, wg_ref, wu_ref, wd_ref, cw_ref, cb_ref, g_ref, o_ref, hn_ref, hext_ref,
                *, tiles_per_seq):
    i = pl.program_id(0)
    f = pl.program_id(1)
    tm = h_ref.shape[0]

    @pl.when(f == 0)
    def _():
        first = (i % tiles_per_seq) == 0
        hext_ref[0:FFN_HALO, :] = jnp.where(first, jnp.zeros_like(hp_ref[...]), hp_ref[...])
        hext_ref[FFN_HALO:, :] = h_ref[...]
        o_ref[...] = x_ref[...]

    tf = wg_ref.shape[2]
    halves = [slice(0, tf // 2), slice(tf // 2, tf)]
    a = [_dot(hext_ref[...], wg_ref[0, :, s]) for s in halves]
    u = [_dot(h_ref[...], wu_ref[0, :, s]) for s in halves]
    cw = cw_ref[...]
    cb = cb_ref[...]
    act = []
    for a_j, u_j, s in zip(a, u, halves):
        c = (cw[0:1, s] * a_j[FFN_HALO - 2:FFN_HALO - 2 + tm] + cw[1:2, s] * a_j[FFN_HALO - 1:FFN_HALO - 1 + tm]
             + cw[2:3, s] * a_j[FFN_HALO:FFN_HALO + tm]) + cb[:, s]
        act.append((jax.nn.gelu(c) * u_j).astype(MXU_DTYPE))
    for act_j, s in zip(act, halves):
        o_ref[...] += _dot(act_j, wd_ref[s, :])

    @pl.when(f == pl.num_programs(1) - 1)
    def _():
        hn_ref[...] = _norm_rows(o_ref[...], g_ref[...]).astype(hn_ref.dtype)


def _conv_ffn(h, x, w_gate, w_up, w_down, conv_w, conv_b, g, seq, tm=512):
    m, d = h.shape
    nf, _, tf = w_gate.shape
    ff = nf * tf
    assert seq % tm == 0
    hb = tm // FFN_HALO
    return pl.pallas_call(
        functools.partial(_ffn_kernel, tiles_per_seq=seq // tm),
        name="conv_ffn",
        grid=(m // tm, ff // tf),
        in_specs=[pl.BlockSpec((tm, d), lambda i, f: (i, 0)),
                  pl.BlockSpec((FFN_HALO, d), lambda i, f: (jnp.maximum(i * hb - 1, 0), 0)),
                  pl.BlockSpec((tm, d), lambda i, f: (i, 0)),
                  pl.BlockSpec((1, d, tf), lambda i, f: (f, 0, 0)),
                  pl.BlockSpec((1, d, tf), lambda i, f: (f, 0, 0)),
                  pl.BlockSpec((tf, d), lambda i, f: (f, 0)),
                  pl.BlockSpec((CONV_WIDTH, tf), lambda i, f: (0, f)),
                  pl.BlockSpec((1, tf), lambda i, f: (0, f)),
                  pl.BlockSpec((1, d), lambda i, f: (0, 0))],
        out_specs=[pl.BlockSpec((tm, d), lambda i, f: (i, 0)), pl.BlockSpec((tm, d), lambda i, f: (i, 0))],
        out_shape=[jax.ShapeDtypeStruct((m, d), F32), jax.ShapeDtypeStruct((m, d), MXU_DTYPE)],
        scratch_shapes=[pltpu.VMEM((tm + FFN_HALO, d), MXU_DTYPE)],
        compiler_params=_cparams(("parallel", "arbitrary")),
    )(h, h, x, w_gate, w_up, w_down, conv_w, conv_b.reshape(1, ff), g.reshape(1, d))


def _ple_kernel(h_ref, p_ref, x_ref, wg_ref, wp_ref, g_ref, o_ref, hn_ref):
    gate = jax.nn.sigmoid(_dot(h_ref[...], wg_ref[...]))
    x = x_ref[...] + gate * _dot(p_ref[...].astype(MXU_DTYPE), wp_ref[...])
    o_ref[...] = x
    hn_ref[...] = _norm_rows(x, g_ref[...]).astype(hn_ref.dtype)


def _ple(h, p, x, w_gate, w_proj, g, norm_dtype, tm=512):
    m, d = h.shape
    row = lambda i: (i, 0)
    fixed = lambda i: (0, 0)
    once = pl.Buffered(1)
    return pl.pallas_call(
        _ple_kernel,
        name="ple",
        grid=(m // tm,),
        in_specs=[pl.BlockSpec((tm, d), row),
                  pl.BlockSpec((tm, p.shape[1]), row),
                  pl.BlockSpec((tm, d), row),
                  pl.BlockSpec((d, d), fixed, pipeline_mode=once),
                  pl.BlockSpec((p.shape[1], d), fixed, pipeline_mode=once),
                  pl.BlockSpec((1, d), fixed)],
        out_specs=[pl.BlockSpec((tm, d), row), pl.BlockSpec((tm, d), row)],
        out_shape=[jax.ShapeDtypeStruct((m, d), F32), jax.ShapeDtypeStruct((m, d), norm_dtype)],
        compiler_params=_cparams(("parallel",)),
    )(h, p, x, w_gate, w_proj, g.reshape(1, d))


def _compress_kernel(x_ref, w1_ref, w2_ref, pe_ref, o_ref):
    x = x_ref[0, 0]
    pe = pe_ref[0]
    half = x.shape[1]
    lo = _dot((x + pe[0:1]).astype(MXU_DTYPE), w1_ref[0, :half, :])
    hi = _dot((x + pe[1:2]).astype(MXU_DTYPE), w1_ref[0, half:, :])
    n = x.shape[0]
    hid = lo + pltpu.roll(hi, n - 1, 0)
    o_ref[0, 0, 0] = _dot(jax.nn.gelu(hid).astype(MXU_DTYPE), w2_ref[0]).astype(o_ref.dtype)


def _compress(kv_hm, w1, w2, pe):
    b, _, s, d = kv_hm.shape
    nrow = s // CMP_STRIDE
    x = kv_hm.reshape(b, 2 * NSA_GROUPS, nrow, CMP_STRIDE * d)
    return pl.pallas_call(
        _compress_kernel,
        name="nsa_compress",
        grid=(b, 2, NSA_GROUPS),
        in_specs=[pl.BlockSpec((1, 1, nrow, CMP_STRIDE * d), lambda bi, kv, g: (bi, kv * NSA_GROUPS + g, 0, 0)),
                  pl.BlockSpec((1,) + w1.shape[1:], lambda bi, kv, g: (kv, 0, 0)),
                  pl.BlockSpec((1,) + w2.shape[1:], lambda bi, kv, g: (kv, 0, 0)),
                  pl.BlockSpec((1, 2, CMP_STRIDE * d), lambda bi, kv, g: (kv, 0, 0))],
        out_specs=pl.BlockSpec((1, 1, 1, nrow, d), lambda bi, kv, g: (bi, kv, g, 0, 0)),
        out_shape=jax.ShapeDtypeStruct((b, 2, NSA_GROUPS, nrow, d), MXU_DTYPE),
        compiler_params=_cparams(("parallel", "parallel", "parallel")),
    )(x, w1, w2, pe)


def _softmax_pv(lg_ref, mx_ref, acc_ref, v_ref, v_index, n_tiles):
    rows = lg_ref.shape[0]
    m = jnp.max(mx_ref[...], axis=1, keepdims=True)
    mx_ref[...] = jnp.zeros_like(mx_ref)
    acc_ref[...] = jnp.zeros_like(acc_ref)

    def body(kt, carry):
        k0 = pl.multiple_of(kt * KT, KT)
        p = jnp.exp(lg_ref[:, pl.ds(k0, KT)] - m)
        part = p[:, 0:128]
        for c in range(1, KT // 128):
            part = part + p[:, c * 128:(c + 1) * 128]
        mx_ref[...] += part
        acc_ref[...] += _dot(p.astype(MXU_DTYPE), v_ref[v_index + (pl.ds(k0, KT), slice(None))])
        return carry

    lax.fori_loop(0, n_tiles, body, 0)
    l = jnp.sum(mx_ref[...], axis=1, keepdims=True)
    return acc_ref[...] / l


def _lane_max(s):
    part = s[:, 0:128]
    for c in range(1, s.shape[1] // 128):
        part = jnp.maximum(part, s[:, c * 128:(c + 1) * 128])
    return part


def _nsa_kernel(q_ref, kc_ref, vc_ref, kslc_ref, vslc_ref, kwin_ref, vwin_ref, gate_ref,
                wslc_ref, wwin_ref, tcmp_ref, ovl_ref, o_ref,
                kslc_aug, kwin_aug, vwin_pad, qaug, lg_ref, mx_ref, acc_ref, score_ref):
    i = pl.program_id(2)
    seq = kslc_ref.shape[2]
    d = HEAD_DIM
    rows = NSA_REP * TQ
    scale = d ** -0.5
    t0 = i * TQ

    @pl.when(i == 0)
    def _():
        kslc_aug[:, 0:d] = kslc_ref[0, 0]
        srow = lax.broadcasted_iota(jnp.int32, (seq, d), 0)
        lane = lax.broadcasted_iota(jnp.int32, (seq, d), 1)
        kslc_aug[:, d:2 * d] = jnp.where((srow >> 6) == lane, 1.0, 0.0).astype(kslc_aug.dtype)
        kwin_aug[0:NSA_WINDOW, 0:d] = jnp.zeros((NSA_WINDOW, d), kwin_aug.dtype)
        kwin_aug[NSA_WINDOW:, 0:d] = kwin_ref[0, 0]
        prow = lax.broadcasted_iota(jnp.int32, (seq + NSA_WINDOW, d), 0)
        plane = lax.broadcasted_iota(jnp.int32, (seq + NSA_WINDOW, d), 1)
        flag = jnp.where(prow < NSA_WINDOW, jnp.where(plane == MAX_SLC_BLOCKS, UNSELECTED, 0.0), 0.0)
        kwin_aug[:, d:2 * d] = flag.astype(kwin_aug.dtype)
        vwin_pad[0:NSA_WINDOW, :] = jnp.zeros((NSA_WINDOW, d), vwin_pad.dtype)
        vwin_pad[NSA_WINDOW:, :] = vwin_ref[0, 0]

    q4 = q_ref[0].reshape(rows, d)

    ncmp = kc_ref.shape[3]
    kc = kc_ref[0, 0, 0]
    vc = vc_ref[0, 0, 0]
    trow = t0 + lax.broadcasted_iota(jnp.int32, (TQ, ncmp), 0)
    cend = lax.broadcasted_iota(jnp.int32, (TQ, ncmp), 1) * CMP_STRIDE + (CMP_BLOCK - 1)
    valid_c = cend <= trow
    sc_all = _dot_nt(q4, kc) * scale
    c0 = i * (TQ // CMP_STRIDE)
    bias_start = pl.multiple_of(ncmp - 128 - 128 * (c0 // 128), 128)
    pb = []
    for r in range(NSA_REP):
        bias = tcmp_ref[0, r, :, pl.ds(bias_start, ncmp)]
        l = jnp.where(valid_c, sc_all[r * TQ:(r + 1) * TQ] + bias, MASKED)
        m = jnp.max(l, axis=1, keepdims=True)
        e = jnp.where(valid_c, jnp.exp(l - m), 0.0)
        p = e / jnp.maximum(jnp.sum(e, axis=1, keepdims=True), 1e-30)
        pb.append(p.astype(MXU_DTYPE))
    o_cmp = [_dot(pb[r], vc) for r in range(NSA_REP)]
    imp_t = _dot_nt(ovl_ref[...], pb[0])
    for r in range(1, NSA_REP):
        imp_t = imp_t + _dot_nt(ovl_ref[...], pb[r])

    span = NSA_WINDOW + TQ
    w0 = pl.multiple_of(t0, TQ)
    one_lane = lax.broadcasted_iota(jnp.int32, (rows, d), 1) == MAX_SLC_BLOCKS
    q_win = jnp.concatenate([q4, jnp.where(one_lane, 1.0, 0.0).astype(q4.dtype)], axis=1)
    s = _dot_nt(q_win, kwin_aug[pl.ds(w0, span), :]) * scale + wwin_ref[...].reshape(rows, span)
    m = jnp.max(s, axis=1, keepdims=True)
    p = jnp.exp(s - m)
    l = jnp.sum(p, axis=1, keepdims=True)
    o_win = _dot(p.astype(MXU_DTYPE), vwin_pad[pl.ds(w0, span), :]) / l

    nblk = MAX_SLC_BLOCKS
    jt = lax.broadcasted_iota(jnp.int32, (nblk, TQ), 0)
    tt = t0 + lax.broadcasted_iota(jnp.int32, (nblk, TQ), 1)
    cur = tt >> 6
    imp_t = imp_t[0:nblk]
    score = jnp.where(jt == 0, 1e9, jnp.where(jt == cur, 1e9, jnp.where(jt == cur - 1, 1e9, imp_t)))
    score = jnp.where(jt * SLC_BLOCK <= tt, score, -jnp.inf)
    score_ref[...] = score
    sub = lax.broadcasted_iota(jnp.int32, (8, TQ), 0)
    sv = [score[8 * v:8 * v + 8] for v in range(nblk // 8)]
    beaten = [jnp.zeros((8, TQ), F32) for _ in sv]
    for jp in range(nblk):
        other = score_ref[jp:jp + 1, :]
        for v in range(nblk // 8):
            if 8 * v > jp:
                hit = other >= sv[v]
            elif 8 * v + 7 <= jp:
                hit = other > sv[v]
            else:
                tie_loses = jnp.where(sub > jp - 8 * v, 1.0, 0.0)
                beaten[v] = beaten[v] + jnp.where(other == sv[v], tie_loses, 0.0)
                hit = other > sv[v]
            beaten[v] = beaten[v] + jnp.where(hit, 1.0, 0.0)
    aug_t = jnp.concatenate([jnp.where(b < SLC_TOPN, 0.0, UNSELECTED) for b in beaten], axis=0)
    row = lax.broadcasted_iota(jnp.int32, (d - nblk, TQ), 0)
    aug_t = jnp.concatenate([aug_t, jnp.where(row == 0, 1.0, 0.0)], axis=0)
    aug = aug_t.T
    qaug[:, 0:d] = q4
    for r in range(NSA_REP):
        qaug[r * TQ:(r + 1) * TQ, d:2 * d] = aug.astype(qaug.dtype)
    qa = qaug[...]

    n_tiles = (t0 + TQ + KT - 1) // KT
    off = t0 - (n_tiles - 1) * KT
    mx_ref[...] = jnp.full(mx_ref.shape, MASKED, F32)

    def slc_tile(kt, tmpl_start):
        k0 = pl.multiple_of(kt * KT, KT)
        s = _dot_nt(qa, kslc_aug[pl.ds(k0, KT), :]) * scale
        if tmpl_start is not None:
            ts = pl.multiple_of(tmpl_start, 128)
            s = s + wslc_ref[:, :, pl.ds(ts, KT)].reshape(rows, KT)
        lg_ref[:, pl.ds(k0, KT)] = s
        mx_ref[...] = jnp.maximum(mx_ref[...], _lane_max(s))

    def far_body(kt, carry):
        slc_tile(kt, None)
        return carry

    lax.fori_loop(0, jnp.maximum(n_tiles - 2, 0), far_body, 0)

    @pl.when(n_tiles >= 2)
    def _():
        slc_tile(n_tiles - 2, TMPL_C0 - off - KT)

    slc_tile(n_tiles - 1, TMPL_C0 - off)
    o_slc = _softmax_pv(lg_ref, mx_ref, acc_ref, vslc_ref, (0, 0), n_tiles)

    g = jax.nn.sigmoid(gate_ref[...])
    for r in range(NSA_REP):
        rs = slice(r * TQ, (r + 1) * TQ)
        o = (g[:, 3 * r:3 * r + 1] * o_cmp[r] + g[:, 3 * r + 1:3 * r + 2] * o_slc[rs]
             + g[:, 3 * r + 2:3 * r + 3] * o_win[rs])
        o_ref[:, r * d:(r + 1) * d] = o.astype(o_ref.dtype)


def _nsa(hm, cmp_kv, misc, wslc, wwin, tcmp, ovl):
    b, _, seq, d = hm.shape
    nq = seq // TQ
    rows = NSA_REP * TQ
    ncmp = cmp_kv.shape[3]
    kv_spec = lambda head: pl.BlockSpec((1, 1, seq, d), lambda bi, g, i: (bi, head + g, 0, 0))
    once = pl.Buffered(1)
    return pl.pallas_call(
        _nsa_kernel,
        name="nsa",
        grid=(b, NSA_GROUPS, nq),
        in_specs=[pl.BlockSpec((1, NSA_REP, TQ, d), lambda bi, g, i: (bi, g, i, 0)),
                  pl.BlockSpec((1, 1, 1, ncmp, d), lambda bi, g, i: (bi, 0, g, 0, 0)),
                  pl.BlockSpec((1, 1, 1, ncmp, d), lambda bi, g, i: (bi, 1, g, 0, 0)),
                  kv_spec(HM_K_SLC), kv_spec(HM_V_SLC), kv_spec(HM_K_WIN), kv_spec(HM_V_WIN),
                  pl.BlockSpec((TQ, 128), lambda bi, g, i: (bi * nq + i, MISC_GATE + g)),
                  pl.BlockSpec((NSA_REP, TQ, TMPL_W), lambda bi, g, i: (g, 0, 0), pipeline_mode=once),
                  pl.BlockSpec((NSA_REP, TQ, NSA_WINDOW + TQ), lambda bi, g, i: (g, 0, 0), pipeline_mode=once),
                  pl.BlockSpec((1, NSA_REP) + tcmp.shape[2:], lambda bi, g, i: (i % tcmp.shape[0], g, 0, 0)),
                  pl.BlockSpec(ovl.shape, lambda bi, g, i: (0, 0), pipeline_mode=once)],
        out_specs=pl.BlockSpec((TQ, NSA_REP * d), lambda bi, g, i: (bi * nq + i, g)),
        out_shape=jax.ShapeDtypeStruct((b * seq, NSA_HEADS * d), MXU_DTYPE),
        scratch_shapes=[pltpu.VMEM((seq, 2 * d), MXU_DTYPE),
                        pltpu.VMEM((seq + NSA_WINDOW, 2 * d), MXU_DTYPE),
                        pltpu.VMEM((seq + NSA_WINDOW, d), MXU_DTYPE),
                        pltpu.VMEM((rows, 2 * d), MXU_DTYPE),
                        pltpu.VMEM((rows, seq), F32),
                        pltpu.VMEM((rows, 128), F32),
                        pltpu.VMEM((rows, d), F32),
                        pltpu.VMEM((MAX_SLC_BLOCKS, TQ), F32)],
        compiler_params=_cparams(("parallel", "parallel", "arbitrary")),
    )(hm, cmp_kv, cmp_kv, hm, hm, hm, hm, misc, wslc, wwin, tcmp, ovl)


def _sb_kernel(q_ref, k_ref, v_ref, upper_ref, o_ref, acc_ref, carry_ref):
    i = pl.program_id(1)
    t = SB_T
    d = HEAD_DIM
    scale = d ** -0.5
    upper = upper_ref[...]
    acc_ref[...] = jnp.zeros_like(acc_ref)
    carry_ref[...] = jnp.zeros_like(carry_ref)

    heads = range(SB_HEADS)

    def tile(kt, mask):
        k0 = pl.multiple_of(kt * t, t)
        z = [_dot_nt(q_ref[0, h], k_ref[0, h, pl.ds(k0, t), :]) * scale for h in heads]
        ls = [jnp.minimum(z[h], 0.0) - jnp.log(1.0 + jnp.exp(-jnp.abs(z[h]))) for h in heads]
        lk = [ls[h] - z[h] for h in heads]
        if mask is not None:
            lk = [jnp.where(mask, lk[h], 0.0) for h in heads]
        hi = [lk[h].astype(MXU_DTYPE) for h in heads]
        lo = [(lk[h] - hi[h].astype(F32)).astype(MXU_DTYPE) for h in heads]
        after = [carry_ref[h] + (_dot(hi[h], upper) + _dot(lo[h], upper)) for h in heads]
        w = [jnp.exp(ls[h] + after[h]) for h in heads]
        if mask is not None:
            w = [jnp.where(mask, w[h], 0.0) for h in heads]
        for h in heads:
            acc_ref[h] += _dot(w[h].astype(MXU_DTYPE), v_ref[0, h, pl.ds(k0, t), :])
            carry_ref[h] += jnp.sum(lk[h], axis=1, keepdims=True)

    strict = lax.broadcasted_iota(jnp.int32, (t, t), 1) < lax.broadcasted_iota(jnp.int32, (t, t), 0)
    tile(i, strict)

    def body(n, carry):
        tile(i - 1 - n, None)
        return carry

    lax.fori_loop(0, i, body, 0)
    for h in range(SB_HEADS):
        o_ref[:, h * d:(h + 1) * d] = acc_ref[h].astype(o_ref.dtype)


def _stick_breaking(hm, upper):
    b, _, seq, d = hm.shape
    nq = seq // SB_T
    return pl.pallas_call(
        _sb_kernel,
        grid=(b, nq),
        in_specs=[pl.BlockSpec((1, SB_HEADS, SB_T, d), lambda bi, i: (bi, HM_SB_Q // SB_HEADS, i, 0)),
                  pl.BlockSpec((1, SB_HEADS, seq, d), lambda bi, i: (bi, HM_SB_K // SB_HEADS, 0, 0)),
                  pl.BlockSpec((1, SB_HEADS, seq, d), lambda bi, i: (bi, HM_SB_V // SB_HEADS, 0, 0)),
                  pl.BlockSpec(upper.shape, lambda bi, i: (0, 0))],
        out_specs=pl.BlockSpec((SB_T, SB_HEADS * d), lambda bi, i: (bi * nq + i, 0)),
        out_shape=jax.ShapeDtypeStruct((b * seq, SB_HEADS * d), MXU_DTYPE),
        scratch_shapes=[pltpu.VMEM((SB_HEADS, SB_T, d), F32), pltpu.VMEM((SB_HEADS, SB_T, 1), F32)],
        name="stick_breaking",
        compiler_params=_cparams(("parallel", "parallel")),
    )(hm, hm, hm, upper)


def _dsa_kv_kernel(c_ref, g_ref, wk_ref, wv_ref, k_ref, v_ref):
    c = c_ref[...]
    y = c * lax.rsqrt(jnp.mean(c * c, axis=-1, keepdims=True) + EPS)
    y = (y * g_ref[...]).astype(MXU_DTYPE)
    k_ref[...] = _dot(y, wk_ref[...]).astype(k_ref.dtype)
    v_ref[...] = _dot(y, wv_ref[...]).astype(v_ref.dtype)


def _dsa_kv(misc, kv_norm, w_uk, w_uv, tm=512):
    m = misc.shape[0]
    r = DSA_KV_RANK
    out = jax.ShapeDtypeStruct((m, HEAD_DIM), MXU_DTYPE)
    return pl.pallas_call(
        _dsa_kv_kernel,
        name="dsa_kv",
        grid=(m // tm,),
        in_specs=[pl.BlockSpec((tm, r), lambda i: (i, MISC_CKV * 128 // r)),
                  pl.BlockSpec((1, r), lambda i: (0, 0)),
                  pl.BlockSpec((r, HEAD_DIM), lambda i: (0, 0)),
                  pl.BlockSpec((r, HEAD_DIM), lambda i: (0, 0))],
        out_specs=[pl.BlockSpec((tm, HEAD_DIM), lambda i: (i, 0)), pl.BlockSpec((tm, HEAD_DIM), lambda i: (i, 0))],
        out_shape=[out, out],
        compiler_params=_cparams(("parallel",)),
    )(misc, kv_norm.reshape(1, r), w_uk, w_uv)


PLANE_GROUPS_PER_TILE = KT // 256


def _bit_planes(words):
    a = list(words)
    j, m = 16, 0x0000FFFF
    while j:
        sh = jnp.full(a[0].shape, j, jnp.int32)
        for k in range(32):
            if not k & j:
                t = (a[k] ^ lax.shift_right_logical(a[k + j], sh)) & m
                a[k] = a[k] ^ t
                a[k + j] = a[k + j] ^ (t << j)
        j >>= 1
        m = (m ^ (m << j)) & 0xFFFFFFFF
    return a


def _dsa_kernel(q_ref, k_ref, v_ref, iq_ref, ik_ref, iw_ref, wd_ref, low_ref, o_ref,
                key_ref, plane_ref, alive_ref, add_ref, lg_ref, mx_ref, acc_ref, *, n_keep):
    i = pl.program_id(1)
    d = HEAD_DIM
    tq = DSA_TQ
    rows = DSA_HEADS * tq
    scale = d ** -0.5
    t0 = i * tq
    n_tiles = (t0 + tq + KT - 1) // KT
    off = t0 - (n_tiles - 1) * KT

    iq = iq_ref[...].astype(MXU_DTYPE)
    qh = [iq[:, h * IDX_DIM:(h + 1) * IDX_DIM] for h in range(IDX_HEADS)]
    wi_t = iw_ref[...].T * (IDX_HEADS ** -0.5) * (IDX_DIM ** -0.5)
    wh = [wi_t[IDX_DIM + h:IDX_DIM + h + 1, :] for h in range(IDX_HEADS)]
    tq_row = t0 + lax.broadcasted_iota(jnp.int32, (KT, tq), 1)
    krow = lax.broadcasted_iota(jnp.int32, (KT, tq), 0)

    def column_sum(a):
        return jnp.sum(a.reshape(KT // 8, 8, tq), axis=0)

    def score_tile(kt, causal):
        k0 = pl.multiple_of(kt * KT, KT)
        ki = ik_ref[pl.ds(k0, KT), 0:IDX_DIM].astype(MXU_DTYPE)
        dots = [_dot_nt(ki, qh[h]) for h in range(IDX_HEADS)]
        sc = wh[0] * jnp.maximum(dots[0], 0.0)
        for h in range(1, IDX_HEADS):
            sc = sc + wh[h] * jnp.maximum(dots[h], 0.0)
        sc = sc + 0.0
        if causal:
            sc = jnp.where(k0 + krow <= tq_row, sc, -jnp.inf)
        bits = lax.bitcast_convert_type(sc, jnp.int32)
        key = bits ^ ((bits >> 31) & 0x7FFFFFFF)
        key_ref[pl.ds(k0, KT), :] = key
        ukey = key ^ INT_MIN
        for g in range(PLANE_GROUPS_PER_TILE):
            words = [ukey[(32 * g + w) * 8:(32 * g + w + 1) * 8] for w in range(32)]
            for x, plane in enumerate(_bit_planes(words)):
                plane_ref[x, PLANE_GROUPS_PER_TILE * kt + g] = plane

    @pl.when((pl.program_id(0) == 0) & (i == 0))
    def _():
        plane_ref[...] = jnp.zeros_like(plane_ref)

    def score_body(kt, carry):
        score_tile(kt, False)
        return carry

    lax.fori_loop(0, n_tiles - 1, score_body, 0)
    score_tile(n_tiles - 1, True)

    ngrp = alive_ref.shape[0]
    for g in range(ngrp):
        alive_ref[g] = jnp.where(g < PLANE_GROUPS_PER_TILE * n_tiles, -1, 0) + jnp.zeros((8, tq), jnp.int32)

    def bit_body(x, carry):
        thr_u, remaining = carry
        ones = [alive_ref[g] & plane_ref[x, g] for g in range(ngrp)]
        c = lax.population_count(ones[0])
        for g in range(1, ngrp):
            c = c + lax.population_count(ones[g])
        c = jnp.sum(c, axis=0, keepdims=True)
        take = c >= remaining
        for g in range(ngrp):
            alive_ref[g] = jnp.where(take, ones[g], alive_ref[g] ^ ones[g])
        bit = jnp.int32(1) << (31 - x)
        return jnp.where(take, thr_u | bit, thr_u), jnp.where(take, remaining, remaining - c)

    thr_u, need = lax.fori_loop(0, 32, bit_body,
                                (jnp.zeros((1, tq), jnp.int32), jnp.full((1, tq), n_keep, jnp.int32)))
    thr = thr_u ^ INT_MIN
    n_equal = lax.population_count(alive_ref[0])
    for g in range(1, ngrp):
        n_equal = n_equal + lax.population_count(alive_ref[g])
    n_equal = jnp.sum(n_equal, axis=0, keepdims=True)
    need = need.astype(F32)
    surplus = jnp.max(n_equal.astype(F32) - need)

    @pl.when(surplus <= 0)
    def _():
        def mask_body(kt, carry):
            k0 = pl.multiple_of(kt * KT, KT)
            add_ref[:, pl.ds(k0, KT)] = jnp.where(key_ref[pl.ds(k0, KT), :] >= thr, 0.0, MASKED).T
            return carry

        lax.fori_loop(0, n_tiles, mask_body, 0)

    @pl.when(surplus > 0)
    def _():
        def mask_body(kt, seen):
            k0 = pl.multiple_of(kt * KT, KT)
            kk = key_ref[pl.ds(k0, KT), :]
            eq = jnp.where(kk == thr, 1.0, 0.0)
            before = seen + _dot(low_ref[...], eq.astype(MXU_DTYPE))
            tie = jnp.where(before < need, 0.0, MASKED)
            add_t = jnp.where(kk > thr, 0.0, jnp.where(kk == thr, tie, MASKED))
            add_ref[:, pl.ds(k0, KT)] = add_t.T
            return seen + jnp.sum(column_sum(eq), axis=0, keepdims=True)

        lax.fori_loop(0, n_tiles, mask_body, jnp.zeros((1, tq), F32))

    q4 = q_ref[0].reshape(rows, d)
    mx_ref[...] = jnp.full(mx_ref.shape, MASKED, F32)

    def att_tile(kt, tmpl_start):
        k0 = pl.multiple_of(kt * KT, KT)
        s = _dot_nt(q4, k_ref[0, pl.ds(k0, KT), :]) * scale
        addm = add_ref[:, pl.ds(k0, KT)]
        s = s + jnp.concatenate([addm] * DSA_HEADS, axis=0)
        if tmpl_start is not None:
            ts = pl.multiple_of(tmpl_start, 128)
            s = s + wd_ref[:, :, pl.ds(ts, KT)].reshape(rows, KT)
        lg_ref[:, pl.ds(k0, KT)] = s
        mx_ref[...] = jnp.maximum(mx_ref[...], _lane_max(s))

    def far_body(kt, carry):
        att_tile(kt, None)
        return carry

    lax.fori_loop(0, jnp.maximum(n_tiles - 2, 0), far_body, 0)

    @pl.when(n_tiles >= 2)
    def _():
        att_tile(n_tiles - 2, TMPL_C0 - off - KT)

    att_tile(n_tiles - 1, TMPL_C0 - off)
    o = _softmax_pv(lg_ref, mx_ref, acc_ref, v_ref, (0,), n_tiles)
    for r in range(DSA_HEADS):
        o_ref[:, r * d:(r + 1) * d] = o[r * tq:(r + 1) * tq].astype(o_ref.dtype)


def _dsa(hm, k, v, misc, wdsa, low):
    b, _, seq, d = hm.shape
    tq = DSA_TQ
    nq = seq // tq
    rows = DSA_HEADS * tq
    k = k.reshape(b, seq, d)
    v = v.reshape(b, seq, d)
    n_keep = min(DSA_TOPK, seq // 4)
    ngrp = PLANE_GROUPS_PER_TILE * (seq // KT)
    return pl.pallas_call(
        functools.partial(_dsa_kernel, n_keep=n_keep),
        name="dsa",
        grid=(b, nq),
        in_specs=[pl.BlockSpec((1, DSA_HEADS, tq, d), lambda bi, i: (bi, HM_DSA_Q // DSA_HEADS, i, 0)),
                  pl.BlockSpec((1, seq, d), lambda bi, i: (bi, 0, 0)),
                  pl.BlockSpec((1, seq, d), lambda bi, i: (bi, 0, 0)),
                  pl.BlockSpec((tq, IDX_HEADS * IDX_DIM), lambda bi, i: (bi * nq + i, MISC_IDXQ)),
                  pl.BlockSpec((seq, 128), lambda bi, i: (bi, MISC_IDXK)),
                  pl.BlockSpec((tq, 128), lambda bi, i: (bi * nq + i, MISC_IDXK)),
                  pl.BlockSpec((DSA_HEADS, tq, TMPL_W), lambda bi, i: (0, 0, 0), pipeline_mode=pl.Buffered(1)),
                  pl.BlockSpec(low.shape, lambda bi, i: (0, 0), pipeline_mode=pl.Buffered(1))],
        out_specs=pl.BlockSpec((tq, DSA_HEADS * d), lambda bi, i: (bi * nq + i, 0)),
        out_shape=jax.ShapeDtypeStruct((b * seq, DSA_HEADS * d), MXU_DTYPE),
        scratch_shapes=[pltpu.VMEM((seq, tq), jnp.int32),
                        pltpu.VMEM((32, ngrp, 8, tq), jnp.int32),
                        pltpu.VMEM((ngrp, 8, tq), jnp.int32),
                        pltpu.VMEM((tq, seq), F32),
                        pltpu.VMEM((rows, seq), F32),
                        pltpu.VMEM((rows, 128), F32),
                        pltpu.VMEM((rows, d), F32)],
        compiler_params=_cparams(("arbitrary", "arbitrary")),
    )(hm, k, v, misc, misc, misc, wdsa, low)


def _t5_bucket(dist):
    n = jnp.maximum(dist, 0)
    max_exact = REL_BUCKETS // 2
    nf = jnp.maximum(n, 1).astype(F32)
    large = max_exact + (jnp.log(nf / max_exact) / math.log(REL_MAX_DIST / max_exact)
                         * (REL_BUCKETS - max_exact)).astype(jnp.int32)
    large = jnp.minimum(large, REL_BUCKETS - 1)
    return jnp.where(n < max_exact, n, large)


def _bias_templates(rel_tab, ncmp):
    far = REL_MAX_DIST
    by_dist = rel_tab[_t5_bucket(jnp.arange(far + 1))] - rel_tab[REL_BUCKETS - 1][None, :]
    by_dist = by_dist.T

    def build(dist, valid, fill=MASKED):
        t = by_dist[:, np.clip(dist, 0, far)]
        return jnp.where(valid[None], t, fill).astype(F32)

    def toeplitz(u, nrows, width):
        nh, l = u.shape
        return jnp.tile(u, (1, nrows))[:, :nrows * (l - 1)].reshape(nh, nrows, l - 1)[:, :, :width]

    def diagonals(nrows, width):
        l = width + nrows
        k = np.arange(l)
        return np.where(k < width, k, k - l)

    def causal(heads, nrows):
        dist = TMPL_C0 - diagonals(nrows, TMPL_W)
        return toeplitz(build(dist, dist >= 0)[heads], nrows, TMPL_W)

    nsa = slice(0, NSA_HEADS)
    span = NSA_WINDOW + TQ
    dist = NSA_WINDOW - diagonals(TQ, span)
    window = toeplitz(build(dist, (dist >= 0) & (dist < NSA_WINDOW))[nsa], TQ, span)
    cc = np.arange(CMP_BAND)[None, :] - CMP_BAND // 2
    dist = np.arange(TQ)[:, None] - CMP_STRIDE * cc - (CMP_BLOCK - 1)
    band = build(dist, dist >= 0, 0.0)[nsa]
    width = 2 * ncmp - 128
    step = TQ // CMP_STRIDE
    cmp = []
    for v in range(128 // step):
        left = ncmp - 128 + step * v
        canvas = jnp.pad(band, ((0, 0), (0, 0), (left, width - left)))
        cmp.append(canvas[:, :, CMP_BAND // 2:CMP_BAND // 2 + width])
    return causal(nsa, TQ), causal(slice(NSA_HEADS, None), DSA_TQ), window, jnp.stack(cmp)


def _pack_w_in(w_in):
    d3 = 3 * w_in.shape[1]
    kv = NSA_GROUPS * HEAD_DIM
    o_q = d3
    o_kc = o_q + NSA_HEADS * HEAD_DIM
    o_vc, o_ks, o_vs, o_kw, o_vw = (o_kc + j * kv for j in range(1, 6))
    o_g = o_vw + kv
    o_sbq = o_g + 3 * NSA_HEADS
    o_sbk = o_sbq + SB_HEADS * HEAD_DIM
    o_sbv = o_sbk + SB_HEADS * HEAD_DIM
    o_dq = o_sbv + SB_HEADS * HEAD_DIM
    o_ckv = o_dq + DSA_HEADS * HEAD_DIM
    o_iq = o_ckv + DSA_KV_RANK
    o_ik = o_iq + IDX_HEADS * IDX_DIM
    o_iw = o_ik + IDX_DIM
    c = lambda a, n: w_in[:, :, a:a + n]
    zeros = lambda n: jnp.zeros(w_in.shape[:2] + (n,), w_in.dtype)
    w_gates = c(0, d3)
    w_hm = jnp.concatenate([c(o_q, NSA_HEADS * HEAD_DIM), c(o_ks, kv), c(o_vs, kv), c(o_kw, kv), c(o_vw, kv),
                            c(o_sbq, 3 * SB_HEADS * HEAD_DIM), c(o_dq, DSA_HEADS * HEAD_DIM)], axis=2)
    w_cmp = c(o_kc, 2 * kv)
    gw = 3 * NSA_REP
    w_misc = jnp.concatenate([c(o_iq, IDX_HEADS * IDX_DIM), c(o_ckv, DSA_KV_RANK),
                              c(o_ik, IDX_DIM), c(o_iw, IDX_HEADS), zeros(128 - IDX_DIM - IDX_HEADS),
                              c(o_g, gw), zeros(128 - gw), c(o_g + gw, gw), zeros(128 - gw)], axis=2)
    blocks = (PROJ_TN, PROJ_TN, PROJ_TN, MISC_COLS // 3)
    return tuple(_col_blocks(w.astype(MXU_DTYPE), tn) for w, tn in zip((w_gates, w_hm, w_cmp, w_misc), blocks))


def kernel(x, p, w_in, norm_mix, norm_ffn, norm_ple, norm_final, w_proj_a, w_proj_b, w_proj_c, w_out,
           cmp_k_w1, cmp_k_w2, cmp_k_pe, cmp_v_w1, cmp_v_w2, cmp_v_pe, dsa_kv_norm, dsa_w_uk, dsa_w_uv,
           rel_bias_table, ffn_w_gate, ffn_w_up, ffn_w_down, ffn_conv_w, ffn_conv_b, ple_w_gate, ple_w_proj):
    batch, seq, d_model = x.shape
    depth = w_in.shape[0]
    m = batch * seq
    assert seq % KT == 0 and seq % SB_T == 0 and seq // SLC_BLOCK <= MAX_SLC_BLOCKS
    ncmp = seq // CMP_STRIDE
    bf = lambda w: w.astype(MXU_DTYPE)

    w_gates, w_hm, w_cmp, w_misc = _pack_w_in(w_in)
    w_a, w_b, w_c, w_o = bf(w_proj_a), bf(w_proj_b), bf(w_proj_c), bf(w_out)
    cmp_w1 = bf(jnp.stack([cmp_k_w1, cmp_v_w1], axis=1))
    cmp_w2 = bf(jnp.stack([cmp_k_w2, cmp_v_w2], axis=1))
    cmp_pe = jnp.stack([cmp_k_pe, cmp_v_pe], axis=1).reshape(depth, 2, 2, CMP_STRIDE * HEAD_DIM)
    w_uk, w_uv = bf(dsa_w_uk), bf(dsa_w_uv)
    f_gate, f_up, f_down = _col_blocks(bf(ffn_w_gate), FFN_TF), _col_blocks(bf(ffn_w_up), FFN_TF), bf(ffn_w_down)
    pl_gate, pl_proj = bf(ple_w_gate), bf(ple_w_proj)

    wslc, wdsa, wwin, tcmp = _bias_templates(rel_bias_table, ncmp)
    cc = np.arange(ncmp)[None, :]
    jj = np.arange(128)[:, None]
    per = SLC_BLOCK // CMP_STRIDE
    ovl = ((cc >= per * jj - (CMP_BLOCK // CMP_STRIDE - 1)) & (cc <= per * jj + per - 1)
           & (cc < ncmp - 1) & (jj < seq // SLC_BLOCK))
    ovl = jnp.asarray(ovl, MXU_DTYPE)
    low = jnp.asarray(np.arange(KT)[:, None] > np.arange(KT)[None, :], MXU_DTYPE)
    upper = jnp.asarray(np.arange(SB_T)[:, None] > np.arange(SB_T)[None, :], MXU_DTYPE)

    x = x.reshape(m, d_model)
    p = p.reshape(depth, m, p.shape[-1])
    h = _rmsnorm(x, norm_mix[0], MXU_DTYPE)
    for i in range(depth):
        gates = _matmul(h, w_gates[i], F32, 1024, "in_proj_gates")
        hm = _matmul_heads(h, w_hm[i], batch, MXU_DTYPE, "in_proj_heads")
        cmp_in = _matmul_heads(h, w_cmp[i], batch, F32, "in_proj_cmp")
        misc = _matmul(h, w_misc[i], F32, 1024, "in_proj_misc")
        cmp_kv = _compress(cmp_in, cmp_w1[i], cmp_w2[i], cmp_pe[i])
        o_a = _nsa(hm, cmp_kv, misc, wslc, wwin, tcmp, ovl)
        o_b = _stick_breaking(hm, upper)
        dk, dv = _dsa_kv(misc, dsa_kv_norm[i], w_uk[i], w_uv[i])
        o_c = _dsa(hm, dk, dv, misc, wdsa, low)
        y = _merge(o_a, o_b, o_c, w_a[i], w_b[i], w_c[i], gates)
        x, h = _matmul_residual(y, w_o[i], x, norm_ffn[i])
        x, h = _conv_ffn(h, x, f_gate[i], f_up[i], f_down[i], ffn_conv_w[i], ffn_conv_b[i], norm_ple[i], seq)
        last = i == depth - 1
        x, h = _ple(h, p[i], x, pl_gate[i], pl_proj[i], norm_final if last else norm_mix[i + 1],
                    F32 if last else MXU_DTYPE)
    return h.reshape(batch, seq, d_model)
```

```python
import functools
import math

import numpy as np
import jax
import jax.numpy as jnp
from jax import lax
from jax.experimental import pallas as pl
from jax.experimental.pallas import tpu as pltpu

F32 = jnp.float32
MXU_DTYPE = jnp.bfloat16

HEAD_DIM = 128
NSA_HEADS = 8
NSA_GROUPS = 2
NSA_REP = NSA_HEADS // NSA_GROUPS
CMP_BLOCK = 32
CMP_STRIDE = 16
SLC_BLOCK = 64
MAX_SLC_BLOCKS = 64
SLC_TOPN = 16
NSA_WINDOW = 512
SB_HEADS = 4
DSA_HEADS = 4
DSA_KV_RANK = 256
IDX_HEADS = 8
IDX_DIM = 64
DSA_TOPK = 256
REL_BUCKETS = 32
REL_MAX_DIST = 128
CONV_WIDTH = 3
EPS = 1e-6

MASKED = -1e30
UNSELECTED = -1e9
INT_MIN = -2 ** 31
VMEM_LIMIT = 56 * 1024 * 1024

TQ = 256
DSA_TQ = 256
KT = 512
SB_T = 256
TMPL_C0 = 896
TMPL_W = TMPL_C0 + KT
PROJ_TN = 512
FFN_TF = 512
CMP_BAND = 32

HM_NSA_Q, HM_K_SLC, HM_V_SLC, HM_K_WIN, HM_V_WIN = 0, 8, 10, 12, 14
HM_SB_Q, HM_SB_K, HM_SB_V, HM_DSA_Q = 0, 4, 8, 12
MISC_IDXQ, MISC_CKV, MISC_IDXK, MISC_GATE, MISC_COLS = 0, 4, 6, 7, 9 * 128


def _cparams(sem):
    return pltpu.CompilerParams(dimension_semantics=sem, vmem_limit_bytes=VMEM_LIMIT)


def _dot(a, b):
    return jnp.dot(a, b, preferred_element_type=F32)


def _dot_nt(a, b):
    return lax.dot_general(a, b, (((1,), (1,)), ((), ())), preferred_element_type=F32)


def _rmsnorm_kernel(x_ref, g_ref, o_ref):
    x = x_ref[...]
    y = x * lax.rsqrt(jnp.mean(x * x, axis=-1, keepdims=True) + EPS)
    o_ref[...] = (y * g_ref[...]).astype(o_ref.dtype)


def _rmsnorm(x, g, out_dtype, tm=512):
    m, d = x.shape
    return pl.pallas_call(
        _rmsnorm_kernel,
        grid=(m // tm,),
        in_specs=[pl.BlockSpec((tm, d), lambda i: (i, 0)), pl.BlockSpec((1, d), lambda i: (0, 0))],
        out_specs=pl.BlockSpec((tm, d), lambda i: (i, 0)),
        out_shape=jax.ShapeDtypeStruct((m, d), out_dtype),
        name="rmsnorm",
        compiler_params=_cparams(("parallel",)),
    )(x, g.reshape(1, d))


def _mm_kernel(a_ref, w_ref, o_ref):
    o_ref[...] = _dot(a_ref[...], w_ref[...]).astype(o_ref.dtype)


def _matmul(a, w, out_dtype, tm, tn, name):
    m, k = a.shape
    n = w.shape[1]
    return pl.pallas_call(
        _mm_kernel,
        name=name,
        grid=(m // tm, n // tn),
        in_specs=[pl.BlockSpec((tm, k), lambda i, j: (i, 0)), pl.BlockSpec((k, tn), lambda i, j: (0, j))],
        out_specs=pl.BlockSpec((tm, tn), lambda i, j: (i, j)),
        out_shape=jax.ShapeDtypeStruct((m, n), out_dtype),
        compiler_params=_cparams(("parallel", "parallel")),
    )(a, w)


def _mm_heads_kernel(a_ref, w_ref, o_ref, *, hb):
    r = _dot(a_ref[...], w_ref[0])
    for j in range(hb):
        o_ref[0, j] = r[:, j * HEAD_DIM:(j + 1) * HEAD_DIM].astype(o_ref.dtype)


def _matmul_heads(a, w, layer, col_block, nb, batch, out_dtype, name, tm=1024):
    m, k = a.shape
    s = m // batch
    tn = PROJ_TN
    hb = tn // HEAD_DIM
    spb = s // tm
    return pl.pallas_call(
        functools.partial(_mm_heads_kernel, hb=hb),
        name=name,
        grid=(batch, spb, nb),
        in_specs=[pl.BlockSpec((tm, k), lambda b, i, j: (b * spb + i, 0)),
                  pl.BlockSpec((1, k, tn), lambda b, i, j: (layer, 0, col_block(j)))],
        out_specs=pl.BlockSpec((1, hb, tm, HEAD_DIM), lambda b, i, j: (b, j, i, 0)),
        out_shape=jax.ShapeDtypeStruct((batch, nb * hb, s, HEAD_DIM), out_dtype),
        compiler_params=_cparams(("parallel", "parallel", "parallel")),
    )(a, w)


def _norm_rows(x, g):
    return x * lax.rsqrt(jnp.mean(x * x, axis=-1, keepdims=True) + EPS) * g


def _mm_res_kernel(a_ref, w_ref, x_ref, g_ref, o_ref, h_ref):
    x = x_ref[...] + _dot(a_ref[...], w_ref[...])
    o_ref[...] = x
    h_ref[...] = _norm_rows(x, g_ref[...]).astype(h_ref.dtype)


def _matmul_residual(a, w, x, g, tm=512):
    m, k = a.shape
    n = w.shape[1]
    row = lambda i: (i, 0)
    fixed = lambda i: (0, 0)
    return pl.pallas_call(
        _mm_res_kernel,
        name="out_proj_residual",
        grid=(m // tm,),
        in_specs=[pl.BlockSpec((tm, k), row),
                  pl.BlockSpec((k, n), fixed, pipeline_mode=pl.Buffered(1)),
                  pl.BlockSpec((tm, n), row),
                  pl.BlockSpec((1, n), fixed)],
        out_specs=[pl.BlockSpec((tm, n), row), pl.BlockSpec((tm, n), row)],
        out_shape=[jax.ShapeDtypeStruct((m, n), F32), jax.ShapeDtypeStruct((m, n), MXU_DTYPE)],
        compiler_params=_cparams(("parallel",)),
    )(a, w, x, g.reshape(1, n))


def _merge_kernel(h_ref, oa_ref, ob_ref, oc_ref, ga_ref, gb_ref, gc_ref, wa_ref, wb_ref, wc_ref, y_ref):
    h = h_ref[...]
    y = jax.nn.sigmoid(_dot(h, ga_ref[0])) * _dot(oa_ref[...], wa_ref[...])
    y += jax.nn.sigmoid(_dot(h, gb_ref[0])) * _dot(ob_ref[...], wb_ref[...])
    y += jax.nn.sigmoid(_dot(h, gc_ref[0])) * _dot(oc_ref[...], wc_ref[...])
    y_ref[...] = y.astype(y_ref.dtype)


def _merge(h, o_a, o_b, o_c, w_in, layer, w_a, w_b, w_c, tm=1024, tn=PROJ_TN):
    m, k = h.shape
    d = w_a.shape[1]
    nb = d // tn
    row = lambda i, j: (i, 0)
    col = lambda i, j: (0, j)
    gate = lambda branch: pl.BlockSpec((1, k, tn), lambda i, j: (layer, 0, branch * nb + j))
    return pl.pallas_call(
        _merge_kernel,
        name="branch_merge",
        grid=(m // tm, nb),
        in_specs=[pl.BlockSpec((tm, k), row),
                  pl.BlockSpec((tm, o_a.shape[1]), row), pl.BlockSpec((tm, o_b.shape[1]), row),
                  pl.BlockSpec((tm, o_c.shape[1]), row),
                  gate(0), gate(1), gate(2),
                  pl.BlockSpec((w_a.shape[0], tn), col), pl.BlockSpec((w_b.shape[0], tn), col),
                  pl.BlockSpec((w_c.shape[0], tn), col)],
        out_specs=pl.BlockSpec((tm, tn), lambda i, j: (i, j)),
        out_shape=jax.ShapeDtypeStruct((m, d), MXU_DTYPE),
        compiler_params=_cparams(("parallel", "parallel")),
    )(h, o_a, o_b, o_c, w_in, w_in, w_in, w_a, w_b, w_c)


FFN_HALO = 16


def _ffn_kernel(h_ref, hp_ref, x_ref, wg_ref, wu_ref, wd_ref, cw_ref, cb_ref, g_ref, o_ref, hn_ref, hext_ref,
                *, tiles_per_seq):
    i = pl.program_id(0)
    f = pl.program_id(1)
    tm = h_ref.shape[0]

    @pl.when(f == 0)
    def _():
        first = (i % tiles_per_seq) == 0
        hext_ref[0:FFN_HALO, :] = jnp.where(first, jnp.zeros_like(hp_ref[...]), hp_ref[...])
        hext_ref[FFN_HALO:, :] = h_ref[...]
        o_ref[...] = x_ref[...]

    tf = wg_ref.shape[1]
    halves = [slice(0, tf // 2), slice(tf // 2, tf)]
    a = [_dot(hext_ref[...], wg_ref[:, s]) for s in halves]
    u = [_dot(h_ref[...], wu_ref[:, s]) for s in halves]
    cw = cw_ref[...]
    cb = cb_ref[...]
    act = []
    for a_j, u_j, s in zip(a, u, halves):
        c = (cw[0:1, s] * a_j[FFN_HALO - 2:FFN_HALO - 2 + tm] + cw[1:2, s] * a_j[FFN_HALO - 1:FFN_HALO - 1 + tm]
             + cw[2:3, s] * a_j[FFN_HALO:FFN_HALO + tm]) + cb[:, s]
        act.append((jax.nn.gelu(c) * u_j).astype(MXU_DTYPE))
    for act_j, s in zip(act, halves):
        o_ref[...] += _dot(act_j, wd_ref[s, :])

    @pl.when(f == pl.num_programs(1) - 1)
    def _():
        hn_ref[...] = _norm_rows(o_ref[...], g_ref[...]).astype(hn_ref.dtype)


def _conv_ffn(h, x, w_gate, w_up, w_down, conv_w, conv_b, g, seq, tm=512):
    m, d = h.shape
    ff = w_gate.shape[1]
    tf = FFN_TF
    assert seq % tm == 0
    hb = tm // FFN_HALO
    return pl.pallas_call(
        functools.partial(_ffn_kernel, tiles_per_seq=seq // tm),
        name="conv_ffn",
        grid=(m // tm, ff // tf),
        in_specs=[pl.BlockSpec((tm, d), lambda i, f: (i, 0)),
                  pl.BlockSpec((FFN_HALO, d), lambda i, f: (jnp.maximum(i * hb - 1, 0), 0)),
                  pl.BlockSpec((tm, d), lambda i, f: (i, 0)),
                  pl.BlockSpec((d, tf), lambda i, f: (0, f)),
                  pl.BlockSpec((d, tf), lambda i, f: (0, f)),
                  pl.BlockSpec((tf, d), lambda i, f: (f, 0)),
                  pl.BlockSpec((CONV_WIDTH, tf), lambda i, f: (0, f)),
                  pl.BlockSpec((1, tf), lambda i, f: (0, f)),
                  pl.BlockSpec((1, d), lambda i, f: (0, 0))],
        out_specs=[pl.BlockSpec((tm, d), lambda i, f: (i, 0)), pl.BlockSpec((tm, d), lambda i, f: (i, 0))],
        out_shape=[jax.ShapeDtypeStruct((m, d), F32), jax.ShapeDtypeStruct((m, d), MXU_DTYPE)],
        scratch_shapes=[pltpu.VMEM((tm + FFN_HALO, d), MXU_DTYPE)],
        compiler_params=_cparams(("parallel", "arbitrary")),
    )(h, h, x, w_gate, w_up, w_down, conv_w, conv_b.reshape(1, ff), g.reshape(1, d))


def _ple_kernel(h_ref, p_ref, x_ref, wg_ref, wp_ref, g_ref, o_ref, hn_ref):
    gate = jax.nn.sigmoid(_dot(h_ref[...], wg_ref[...]))
    x = x_ref[...] + gate * _dot(p_ref[...].astype(MXU_DTYPE), wp_ref[...])
    o_ref[...] = x
    hn_ref[...] = _norm_rows(x, g_ref[...]).astype(hn_ref.dtype)


def _ple(h, p, x, w_gate, w_proj, g, norm_dtype, tm=512):
    m, d = h.shape
    row = lambda i: (i, 0)
    fixed = lambda i: (0, 0)
    once = pl.Buffered(1)
    return pl.pallas_call(
        _ple_kernel,
        name="ple",
        grid=(m // tm,),
        in_specs=[pl.BlockSpec((tm, d), row),
                  pl.BlockSpec((tm, p.shape[1]), row),
                  pl.BlockSpec((tm, d), row),
                  pl.BlockSpec((d, d), fixed, pipeline_mode=once),
                  pl.BlockSpec((p.shape[1], d), fixed, pipeline_mode=once),
                  pl.BlockSpec((1, d), fixed)],
        out_specs=[pl.BlockSpec((tm, d), row), pl.BlockSpec((tm, d), row)],
        out_shape=[jax.ShapeDtypeStruct((m, d), F32), jax.ShapeDtypeStruct((m, d), norm_dtype)],
        compiler_params=_cparams(("parallel",)),
    )(h, p, x, w_gate, w_proj, g.reshape(1, d))


def _compress_kernel(x_ref, w1_ref, w2_ref, pe_ref, o_ref):
    x = x_ref[0, 0]
    pe = pe_ref[0]
    half = x.shape[1]
    lo = _dot((x + pe[0:1]).astype(MXU_DTYPE), w1_ref[0, :half, :])
    hi = _dot((x + pe[1:2]).astype(MXU_DTYPE), w1_ref[0, half:, :])
    n = x.shape[0]
    hid = lo + pltpu.roll(hi, n - 1, 0)
    o_ref[0, 0, 0] = _dot(jax.nn.gelu(hid).astype(MXU_DTYPE), w2_ref[0]).astype(o_ref.dtype)


def _compress(kv_hm, w1, w2, pe):
    b, _, s, d = kv_hm.shape
    nrow = s // CMP_STRIDE
    x = kv_hm.reshape(b, 2 * NSA_GROUPS, nrow, CMP_STRIDE * d)
    return pl.pallas_call(
        _compress_kernel,
        name="nsa_compress",
        grid=(b, 2, NSA_GROUPS),
        in_specs=[pl.BlockSpec((1, 1, nrow, CMP_STRIDE * d), lambda bi, kv, g: (bi, kv * NSA_GROUPS + g, 0, 0)),
                  pl.BlockSpec((1,) + w1.shape[1:], lambda bi, kv, g: (kv, 0, 0)),
                  pl.BlockSpec((1,) + w2.shape[1:], lambda bi, kv, g: (kv, 0, 0)),
                  pl.BlockSpec((1, 2, CMP_STRIDE * d), lambda bi, kv, g: (kv, 0, 0))],
        out_specs=pl.BlockSpec((1, 1, 1, nrow, d), lambda bi, kv, g: (bi, kv, g, 0, 0)),
        out_shape=jax.ShapeDtypeStruct((b, 2, NSA_GROUPS, nrow, d), MXU_DTYPE),
        compiler_params=_cparams(("parallel", "parallel", "parallel")),
    )(x, w1, w2, pe)


def _softmax_pv(lg_ref, mx_ref, acc_ref, v_ref, v_index, n_tiles):
    rows = lg_ref.shape[0]
    m = jnp.max(mx_ref[...], axis=1, keepdims=True)
    mx_ref[...] = jnp.zeros_like(mx_ref)
    acc_ref[...] = jnp.zeros_like(acc_ref)

    def body(kt, carry):
        k0 = pl.multiple_of(kt * KT, KT)
        p = jnp.exp(lg_ref[:, pl.ds(k0, KT)] - m)
        part = p[:, 0:128]
        for c in range(1, KT // 128):
            part = part + p[:, c * 128:(c + 1) * 128]
        mx_ref[...] += part
        acc_ref[...] += _dot(p.astype(MXU_DTYPE), v_ref[v_index + (pl.ds(k0, KT), slice(None))])
        return carry

    lax.fori_loop(0, n_tiles, body, 0)
    l = jnp.sum(mx_ref[...], axis=1, keepdims=True)
    return acc_ref[...] / l


def _lane_max(s):
    part = s[:, 0:128]
    for c in range(1, s.shape[1] // 128):
        part = jnp.maximum(part, s[:, c * 128:(c + 1) * 128])
    return part


def _nsa_kernel(q_ref, kc_ref, vc_ref, kslc_ref, vslc_ref, kwin_ref, vwin_ref, gate_ref,
                wslc_ref, wwin_ref, tcmp_ref, ovl_ref, o_ref,
                kslc_aug, kwin_aug, vwin_pad, qaug, lg_ref, mx_ref, acc_ref, score_ref):
    i = pl.program_id(2)
    seq = kslc_ref.shape[2]
    d = HEAD_DIM
    rows = NSA_REP * TQ
    scale = d ** -0.5
    t0 = i * TQ

    @pl.when(i == 0)
    def _():
        kslc_aug[:, 0:d] = kslc_ref[0, 0]
        srow = lax.broadcasted_iota(jnp.int32, (seq, d), 0)
        lane = lax.broadcasted_iota(jnp.int32, (seq, d), 1)
        kslc_aug[:, d:2 * d] = jnp.where((srow >> 6) == lane, 1.0, 0.0).astype(kslc_aug.dtype)
        kwin_aug[0:NSA_WINDOW, 0:d] = jnp.zeros((NSA_WINDOW, d), kwin_aug.dtype)
        kwin_aug[NSA_WINDOW:, 0:d] = kwin_ref[0, 0]
        prow = lax.broadcasted_iota(jnp.int32, (seq + NSA_WINDOW, d), 0)
        plane = lax.broadcasted_iota(jnp.int32, (seq + NSA_WINDOW, d), 1)
        flag = jnp.where(prow < NSA_WINDOW, jnp.where(plane == MAX_SLC_BLOCKS, UNSELECTED, 0.0), 0.0)
        kwin_aug[:, d:2 * d] = flag.astype(kwin_aug.dtype)
        vwin_pad[0:NSA_WINDOW, :] = jnp.zeros((NSA_WINDOW, d), vwin_pad.dtype)
        vwin_pad[NSA_WINDOW:, :] = vwin_ref[0, 0]

    q4 = q_ref[0].reshape(rows, d)

    ncmp = kc_ref.shape[3]
    kc = kc_ref[0, 0, 0]
    vc = vc_ref[0, 0, 0]
    trow = t0 + lax.broadcasted_iota(jnp.int32, (TQ, ncmp), 0)
    cend = lax.broadcasted_iota(jnp.int32, (TQ, ncmp), 1) * CMP_STRIDE + (CMP_BLOCK - 1)
    valid_c = cend <= trow
    sc_all = _dot_nt(q4, kc) * scale
    c0 = i * (TQ // CMP_STRIDE)
    bias_start = pl.multiple_of(ncmp - 128 - 128 * (c0 // 128), 128)
    pb = []
    for r in range(NSA_REP):
        bias = tcmp_ref[0, r, :, pl.ds(bias_start, ncmp)]
        l = jnp.where(valid_c, sc_all[r * TQ:(r + 1) * TQ] + bias, MASKED)
        m = jnp.max(l, axis=1, keepdims=True)
        e = jnp.where(valid_c, jnp.exp(l - m), 0.0)
        p = e / jnp.maximum(jnp.sum(e, axis=1, keepdims=True), 1e-30)
        pb.append(p.astype(MXU_DTYPE))
    o_cmp = [_dot(pb[r], vc) for r in range(NSA_REP)]
    imp_t = _dot_nt(ovl_ref[...], pb[0])
    for r in range(1, NSA_REP):
        imp_t = imp_t + _dot_nt(ovl_ref[...], pb[r])

    span = NSA_WINDOW + TQ
    w0 = pl.multiple_of(t0, TQ)
    one_lane = lax.broadcasted_iota(jnp.int32, (rows, d), 1) == MAX_SLC_BLOCKS
    q_win = jnp.concatenate([q4, jnp.where(one_lane, 1.0, 0.0).astype(q4.dtype)], axis=1)
    s = _dot_nt(q_win, kwin_aug[pl.ds(w0, span), :]) * scale + wwin_ref[...].reshape(rows, span)
    m = jnp.max(s, axis=1, keepdims=True)
    p = jnp.exp(s - m)
    l = jnp.sum(p, axis=1, keepdims=True)
    o_win = _dot(p.astype(MXU_DTYPE), vwin_pad[pl.ds(w0, span), :]) / l

    nblk = MAX_SLC_BLOCKS
    jt = lax.broadcasted_iota(jnp.int32, (nblk, TQ), 0)
    tt = t0 + lax.broadcasted_iota(jnp.int32, (nblk, TQ), 1)
    cur = tt >> 6
    imp_t = imp_t[0:nblk]
    score = jnp.where(jt == 0, 1e9, jnp.where(jt == cur, 1e9, jnp.where(jt == cur - 1, 1e9, imp_t)))
    score = jnp.where(jt * SLC_BLOCK <= tt, score, -jnp.inf)
    score_ref[...] = score
    sub = lax.broadcasted_iota(jnp.int32, (8, TQ), 0)
    sv = [score[8 * v:8 * v + 8] for v in range(nblk // 8)]
    beaten = [jnp.zeros((8, TQ), F32) for _ in sv]
    for jp in range(nblk):
        other = score_ref[jp:jp + 1, :]
        for v in range(nblk // 8):
            if 8 * v > jp:
                hit = other >= sv[v]
            elif 8 * v + 7 <= jp:
                hit = other > sv[v]
            else:
                tie_loses = jnp.where(sub > jp - 8 * v, 1.0, 0.0)
                beaten[v] = beaten[v] + jnp.where(other == sv[v], tie_loses, 0.0)
                hit = other > sv[v]
            beaten[v] = beaten[v] + jnp.where(hit, 1.0, 0.0)
    aug_t = jnp.concatenate([jnp.where(b < SLC_TOPN, 0.0, UNSELECTED) for b in beaten], axis=0)
    row = lax.broadcasted_iota(jnp.int32, (d - nblk, TQ), 0)
    aug_t = jnp.concatenate([aug_t, jnp.where(row == 0, 1.0, 0.0)], axis=0)
    aug = aug_t.T
    qaug[:, 0:d] = q4
    for r in range(NSA_REP):
        qaug[r * TQ:(r + 1) * TQ, d:2 * d] = aug.astype(qaug.dtype)
    qa = qaug[...]

    n_tiles = (t0 + TQ + KT - 1) // KT
    off = t0 - (n_tiles - 1) * KT
    mx_ref[...] = jnp.full(mx_ref.shape, MASKED, F32)

    def slc_tile(kt, tmpl_start):
        k0 = pl.multiple_of(kt * KT, KT)
        s = _dot_nt(qa, kslc_aug[pl.ds(k0, KT), :]) * scale
        if tmpl_start is not None:
            ts = pl.multiple_of(tmpl_start, 128)
            s = s + wslc_ref[:, :, pl.ds(ts, KT)].reshape(rows, KT)
        lg_ref[:, pl.ds(k0, KT)] = s
        mx_ref[...] = jnp.maximum(mx_ref[...], _lane_max(s))

    def far_body(kt, carry):
        slc_tile(kt, None)
        return carry

    lax.fori_loop(0, jnp.maximum(n_tiles - 2, 0), far_body, 0)

    @pl.when(n_tiles >= 2)
    def _():
        slc_tile(n_tiles - 2, TMPL_C0 - off - KT)

    slc_tile(n_tiles - 1, TMPL_C0 - off)
    o_slc = _softmax_pv(lg_ref, mx_ref, acc_ref, vslc_ref, (0, 0), n_tiles)

    g = jax.nn.sigmoid(gate_ref[...])
    for r in range(NSA_REP):
        rs = slice(r * TQ, (r + 1) * TQ)
        o = (g[:, 3 * r:3 * r + 1] * o_cmp[r] + g[:, 3 * r + 1:3 * r + 2] * o_slc[rs]
             + g[:, 3 * r + 2:3 * r + 3] * o_win[rs])
        o_ref[:, r * d:(r + 1) * d] = o.astype(o_ref.dtype)


def _nsa(hm, cmp_kv, misc, wslc, wwin, tcmp, ovl):
    b, _, seq, d = hm.shape
    nq = seq // TQ
    rows = NSA_REP * TQ
    ncmp = cmp_kv.shape[3]
    kv_spec = lambda head: pl.BlockSpec((1, 1, seq, d), lambda bi, g, i: (bi, head + g, 0, 0))
    once = pl.Buffered(1)
    return pl.pallas_call(
        _nsa_kernel,
        name="nsa",
        grid=(b, NSA_GROUPS, nq),
        in_specs=[pl.BlockSpec((1, NSA_REP, TQ, d), lambda bi, g, i: (bi, g, i, 0)),
                  pl.BlockSpec((1, 1, 1, ncmp, d), lambda bi, g, i: (bi, 0, g, 0, 0)),
                  pl.BlockSpec((1, 1, 1, ncmp, d), lambda bi, g, i: (bi, 1, g, 0, 0)),
                  kv_spec(HM_K_SLC), kv_spec(HM_V_SLC), kv_spec(HM_K_WIN), kv_spec(HM_V_WIN),
                  pl.BlockSpec((TQ, 128), lambda bi, g, i: (bi * nq + i, MISC_GATE + g)),
                  pl.BlockSpec((NSA_REP, TQ, TMPL_W), lambda bi, g, i: (g, 0, 0), pipeline_mode=once),
                  pl.BlockSpec((NSA_REP, TQ, NSA_WINDOW + TQ), lambda bi, g, i: (g, 0, 0), pipeline_mode=once),
                  pl.BlockSpec((1, NSA_REP) + tcmp.shape[2:], lambda bi, g, i: (i % tcmp.shape[0], g, 0, 0)),
                  pl.BlockSpec(ovl.shape, lambda bi, g, i: (0, 0), pipeline_mode=once)],
        out_specs=pl.BlockSpec((TQ, NSA_REP * d), lambda bi, g, i: (bi * nq + i, g)),
        out_shape=jax.ShapeDtypeStruct((b * seq, NSA_HEADS * d), MXU_DTYPE),
        scratch_shapes=[pltpu.VMEM((seq, 2 * d), MXU_DTYPE),
                        pltpu.VMEM((seq + NSA_WINDOW, 2 * d), MXU_DTYPE),
                        pltpu.VMEM((seq + NSA_WINDOW, d), MXU_DTYPE),
                        pltpu.VMEM((rows, 2 * d), MXU_DTYPE),
                        pltpu.VMEM((rows, seq), F32),
                        pltpu.VMEM((rows, 128), F32),
                        pltpu.VMEM((rows, d), F32),
                        pltpu.VMEM((MAX_SLC_BLOCKS, TQ), F32)],
        compiler_params=_cparams(("parallel", "parallel", "arbitrary")),
    )(hm, cmp_kv, cmp_kv, hm, hm, hm, hm, misc, wslc, wwin, tcmp, ovl)


def _sb_kernel(q_ref, k_ref, v_ref, upper_ref, o_ref, acc_ref, carry_ref):
    i = pl.program_id(1)
    t = SB_T
    d = HEAD_DIM
    scale = d ** -0.5
    upper = upper_ref[...]
    acc_ref[...] = jnp.zeros_like(acc_ref)
    carry_ref[...] = jnp.zeros_like(carry_ref)

    heads = range(SB_HEADS)

    def tile(kt, mask):
        k0 = pl.multiple_of(kt * t, t)
        z = [_dot_nt(q_ref[0, h], k_ref[0, h, pl.ds(k0, t), :]) * scale for h in heads]
        ls = [jnp.minimum(z[h], 0.0) - jnp.log(1.0 + jnp.exp(-jnp.abs(z[h]))) for h in heads]
        lk = [ls[h] - z[h] for h in heads]
        if mask is not None:
            lk = [jnp.where(mask, lk[h], 0.0) for h in heads]
        hi = [lk[h].astype(MXU_DTYPE) for h in heads]
        lo = [(lk[h] - hi[h].astype(F32)).astype(MXU_DTYPE) for h in heads]
        after = [carry_ref[h] + (_dot(hi[h], upper) + _dot(lo[h], upper)) for h in heads]
        w = [jnp.exp(ls[h] + after[h]) for h in heads]
        if mask is not None:
            w = [jnp.where(mask, w[h], 0.0) for h in heads]
        for h in heads:
            acc_ref[h] += _dot(w[h].astype(MXU_DTYPE), v_ref[0, h, pl.ds(k0, t), :])
            carry_ref[h] += jnp.sum(lk[h], axis=1, keepdims=True)

    strict = lax.broadcasted_iota(jnp.int32, (t, t), 1) < lax.broadcasted_iota(jnp.int32, (t, t), 0)
    tile(i, strict)

    def body(n, carry):
        tile(i - 1 - n, None)
        return carry

    lax.fori_loop(0, i, body, 0)
    for h in range(SB_HEADS):
        o_ref[:, h * d:(h + 1) * d] = acc_ref[h].astype(o_ref.dtype)


def _stick_breaking(hm, upper):
    b, _, seq, d = hm.shape
    nq = seq // SB_T
    return pl.pallas_call(
        _sb_kernel,
        grid=(b, nq),
        in_specs=[pl.BlockSpec((1, SB_HEADS, SB_T, d), lambda bi, i: (bi, HM_SB_Q // SB_HEADS, i, 0)),
                  pl.BlockSpec((1, SB_HEADS, seq, d), lambda bi, i: (bi, HM_SB_K // SB_HEADS, 0, 0)),
                  pl.BlockSpec((1, SB_HEADS, seq, d), lambda bi, i: (bi, HM_SB_V // SB_HEADS, 0, 0)),
                  pl.BlockSpec(upper.shape, lambda bi, i: (0, 0))],
        out_specs=pl.BlockSpec((SB_T, SB_HEADS * d), lambda bi, i: (bi * nq + i, 0)),
        out_shape=jax.ShapeDtypeStruct((b * seq, SB_HEADS * d), MXU_DTYPE),
        scratch_shapes=[pltpu.VMEM((SB_HEADS, SB_T, d), F32), pltpu.VMEM((SB_HEADS, SB_T, 1), F32)],
        name="stick_breaking",
        compiler_params=_cparams(("parallel", "parallel")),
    )(hm, hm, hm, upper)


def _dsa_kv_kernel(c_ref, g_ref, wk_ref, wv_ref, k_ref, v_ref):
    c = c_ref[...]
    y = c * lax.rsqrt(jnp.mean(c * c, axis=-1, keepdims=True) + EPS)
    y = (y * g_ref[...]).astype(MXU_DTYPE)
    k_ref[...] = _dot(y, wk_ref[...]).astype(k_ref.dtype)
    v_ref[...] = _dot(y, wv_ref[...]).astype(v_ref.dtype)


def _dsa_kv(misc, kv_norm, w_uk, w_uv, tm=512):
    m = misc.shape[0]
    r = DSA_KV_RANK
    out = jax.ShapeDtypeStruct((m, HEAD_DIM), MXU_DTYPE)
    return pl.pallas_call(
        _dsa_kv_kernel,
        name="dsa_kv",
        grid=(m // tm,),
        in_specs=[pl.BlockSpec((tm, r), lambda i: (i, MISC_CKV * 128 // r)),
                  pl.BlockSpec((1, r), lambda i: (0, 0)),
                  pl.BlockSpec((r, HEAD_DIM), lambda i: (0, 0)),
                  pl.BlockSpec((r, HEAD_DIM), lambda i: (0, 0))],
        out_specs=[pl.BlockSpec((tm, HEAD_DIM), lambda i: (i, 0)), pl.BlockSpec((tm, HEAD_DIM), lambda i: (i, 0))],
        out_shape=[out, out],
        compiler_params=_cparams(("parallel",)),
    )(misc, kv_norm.reshape(1, r), w_uk, w_uv)


PLANE_GROUPS_PER_TILE = KT // 256


def _bit_planes(words):
    a = list(words)
    j, m = 16, 0x0000FFFF
    while j:
        sh = jnp.full(a[0].shape, j, jnp.int32)
        for k in range(32):
            if not k & j:
                t = (a[k] ^ lax.shift_right_logical(a[k + j], sh)) & m
                a[k] = a[k] ^ t
                a[k + j] = a[k + j] ^ (t << j)
        j >>= 1
        m = (m ^ (m << j)) & 0xFFFFFFFF
    return a


def _dsa_kernel(q_ref, k_ref, v_ref, iq_ref, ik_ref, iw_ref, wd_ref, low_ref, o_ref,
                key_ref, plane_ref, alive_ref, add_ref, lg_ref, mx_ref, acc_ref, *, n_keep):
    i = pl.program_id(1)
    d = HEAD_DIM
    tq = DSA_TQ
    rows = DSA_HEADS * tq
    scale = d ** -0.5
    t0 = i * tq
    n_tiles = (t0 + tq + KT - 1) // KT
    off = t0 - (n_tiles - 1) * KT

    iq = iq_ref[...].astype(MXU_DTYPE)
    qh = [iq[:, h * IDX_DIM:(h + 1) * IDX_DIM] for h in range(IDX_HEADS)]
    wi_t = iw_ref[...].T * (IDX_HEADS ** -0.5) * (IDX_DIM ** -0.5)
    wh = [wi_t[IDX_DIM + h:IDX_DIM + h + 1, :] for h in range(IDX_HEADS)]
    tq_row = t0 + lax.broadcasted_iota(jnp.int32, (KT, tq), 1)
    krow = lax.broadcasted_iota(jnp.int32, (KT, tq), 0)

    def column_sum(a):
        return jnp.sum(a.reshape(KT // 8, 8, tq), axis=0)

    def score_tile(kt, causal):
        k0 = pl.multiple_of(kt * KT, KT)
        ki = ik_ref[pl.ds(k0, KT), 0:IDX_DIM].astype(MXU_DTYPE)
        dots = [_dot_nt(ki, qh[h]) for h in range(IDX_HEADS)]
        sc = wh[0] * jnp.maximum(dots[0], 0.0)
        for h in range(1, IDX_HEADS):
            sc = sc + wh[h] * jnp.maximum(dots[h], 0.0)
        sc = sc + 0.0
        if causal:
            sc = jnp.where(k0 + krow <= tq_row, sc, -jnp.inf)
        bits = lax.bitcast_convert_type(sc, jnp.int32)
        key = bits ^ ((bits >> 31) & 0x7FFFFFFF)
        key_ref[pl.ds(k0, KT), :] = key
        ukey = key ^ INT_MIN
        for g in range(PLANE_GROUPS_PER_TILE):
            words = [ukey[(32 * g + w) * 8:(32 * g + w + 1) * 8] for w in range(32)]
            for x, plane in enumerate(_bit_planes(words)):
                plane_ref[x, PLANE_GROUPS_PER_TILE * kt + g] = plane

    @pl.when((pl.program_id(0) == 0) & (i == 0))
    def _():
        plane_ref[...] = jnp.zeros_like(plane_ref)

    def score_body(kt, carry):
        score_tile(kt, False)
        return carry

    lax.fori_loop(0, n_tiles - 1, score_body, 0)
    score_tile(n_tiles - 1, True)

    ngrp = alive_ref.shape[0]
    for g in range(ngrp):
        alive_ref[g] = jnp.where(g < PLANE_GROUPS_PER_TILE * n_tiles, -1, 0) + jnp.zeros((8, tq), jnp.int32)

    def bit_body(x, carry):
        thr_u, remaining = carry
        ones = [alive_ref[g] & plane_ref[x, g] for g in range(ngrp)]
        c = lax.population_count(ones[0])
        for g in range(1, ngrp):
            c = c + lax.population_count(ones[g])
        c = jnp.sum(c, axis=0, keepdims=True)
        take = c >= remaining
        for g in range(ngrp):
            alive_ref[g] = jnp.where(take, ones[g], alive_ref[g] ^ ones[g])
        bit = jnp.int32(1) << (31 - x)
        return jnp.where(take, thr_u | bit, thr_u), jnp.where(take, remaining, remaining - c)

    thr_u, need = lax.fori_loop(0, 32, bit_body,
                                (jnp.zeros((1, tq), jnp.int32), jnp.full((1, tq), n_keep, jnp.int32)))
    thr = thr_u ^ INT_MIN
    n_equal = lax.population_count(alive_ref[0])
    for g in range(1, ngrp):
        n_equal = n_equal + lax.population_count(alive_ref[g])
    n_equal = jnp.sum(n_equal, axis=0, keepdims=True)
    need = need.astype(F32)
    surplus = jnp.max(n_equal.astype(F32) - need)

    @pl.when(surplus <= 0)
    def _():
        def mask_body(kt, carry):
            k0 = pl.multiple_of(kt * KT, KT)
            add_ref[:, pl.ds(k0, KT)] = jnp.where(key_ref[pl.ds(k0, KT), :] >= thr, 0.0, MASKED).T
            return carry

        lax.fori_loop(0, n_tiles, mask_body, 0)

    @pl.when(surplus > 0)
    def _():
        def mask_body(kt, seen):
            k0 = pl.multiple_of(kt * KT, KT)
            kk = key_ref[pl.ds(k0, KT), :]
            eq = jnp.where(kk == thr, 1.0, 0.0)
            before = seen + _dot(low_ref[...], eq.astype(MXU_DTYPE))
            tie = jnp.where(before < need, 0.0, MASKED)
            add_t = jnp.where(kk > thr, 0.0, jnp.where(kk == thr, tie, MASKED))
            add_ref[:, pl.ds(k0, KT)] = add_t.T
            return seen + jnp.sum(column_sum(eq), axis=0, keepdims=True)

        lax.fori_loop(0, n_tiles, mask_body, jnp.zeros((1, tq), F32))

    q4 = q_ref[0].reshape(rows, d)
    mx_ref[...] = jnp.full(mx_ref.shape, MASKED, F32)

    def att_tile(kt, tmpl_start):
        k0 = pl.multiple_of(kt * KT, KT)
        s = _dot_nt(q4, k_ref[0, pl.ds(k0, KT), :]) * scale
        addm = add_ref[:, pl.ds(k0, KT)]
        s = s + jnp.concatenate([addm] * DSA_HEADS, axis=0)
        if tmpl_start is not None:
            ts = pl.multiple_of(tmpl_start, 128)
            s = s + wd_ref[:, :, pl.ds(ts, KT)].reshape(rows, KT)
        lg_ref[:, pl.ds(k0, KT)] = s
        mx_ref[...] = jnp.maximum(mx_ref[...], _lane_max(s))

    def far_body(kt, carry):
        att_tile(kt, None)
        return carry

    lax.fori_loop(0, jnp.maximum(n_tiles - 2, 0), far_body, 0)

    @pl.when(n_tiles >= 2)
    def _():
        att_tile(n_tiles - 2, TMPL_C0 - off - KT)

    att_tile(n_tiles - 1, TMPL_C0 - off)
    o = _softmax_pv(lg_ref, mx_ref, acc_ref, v_ref, (0,), n_tiles)
    for r in range(DSA_HEADS):
        o_ref[:, r * d:(r + 1) * d] = o[r * tq:(r + 1) * tq].astype(o_ref.dtype)


def _dsa(hm, k, v, misc, wdsa, low):
    b, _, seq, d = hm.shape
    tq = DSA_TQ
    nq = seq // tq
    rows = DSA_HEADS * tq
    k = k.reshape(b, seq, d)
    v = v.reshape(b, seq, d)
    n_keep = min(DSA_TOPK, seq // 4)
    ngrp = PLANE_GROUPS_PER_TILE * (seq // KT)
    return pl.pallas_call(
        functools.partial(_dsa_kernel, n_keep=n_keep),
        name="dsa",
        grid=(b, nq),
        in_specs=[pl.BlockSpec((1, DSA_HEADS, tq, d), lambda bi, i: (bi, HM_DSA_Q // DSA_HEADS, i, 0)),
                  pl.BlockSpec((1, seq, d), lambda bi, i: (bi, 0, 0)),
                  pl.BlockSpec((1, seq, d), lambda bi, i: (bi, 0, 0)),
                  pl.BlockSpec((tq, IDX_HEADS * IDX_DIM), lambda bi, i: (bi * nq + i, MISC_IDXQ)),
                  pl.BlockSpec((seq, 128), lambda bi, i: (bi, MISC_IDXK)),
                  pl.BlockSpec((tq, 128), lambda bi, i: (bi * nq + i, MISC_IDXK)),
                  pl.BlockSpec((DSA_HEADS, tq, TMPL_W), lambda bi, i: (0, 0, 0), pipeline_mode=pl.Buffered(1)),
                  pl.BlockSpec(low.shape, lambda bi, i: (0, 0), pipeline_mode=pl.Buffered(1))],
        out_specs=pl.BlockSpec((tq, DSA_HEADS * d), lambda bi, i: (bi * nq + i, 0)),
        out_shape=jax.ShapeDtypeStruct((b * seq, DSA_HEADS * d), MXU_DTYPE),
        scratch_shapes=[pltpu.VMEM((seq, tq), jnp.int32),
                        pltpu.VMEM((32, ngrp, 8, tq), jnp.int32),
                        pltpu.VMEM((ngrp, 8, tq), jnp.int32),
                        pltpu.VMEM((tq, seq), F32),
                        pltpu.VMEM((rows, seq), F32),
                        pltpu.VMEM((rows, 128), F32),
                        pltpu.VMEM((rows, d), F32)],
        compiler_params=_cparams(("arbitrary", "arbitrary")),
    )(hm, k, v, misc, misc, misc, wdsa, low)


def _t5_bucket(dist):
    n = jnp.maximum(dist, 0)
    max_exact = REL_BUCKETS // 2
    nf = jnp.maximum(n, 1).astype(F32)
    large = max_exact + (jnp.log(nf / max_exact) / math.log(REL_MAX_DIST / max_exact)
                         * (REL_BUCKETS - max_exact)).astype(jnp.int32)
    large = jnp.minimum(large, REL_BUCKETS - 1)
    return jnp.where(n < max_exact, n, large)


def _bias_templates(rel_tab, ncmp):
    far = REL_MAX_DIST
    by_dist = rel_tab[_t5_bucket(jnp.arange(far + 1))] - rel_tab[REL_BUCKETS - 1][None, :]
    by_dist = by_dist.T

    def build(dist, valid, fill=MASKED):
        t = by_dist[:, np.clip(dist, 0, far)]
        return jnp.where(valid[None], t, fill).astype(F32)

    def toeplitz(u, nrows, width):
        nh, l = u.shape
        return jnp.tile(u, (1, nrows))[:, :nrows * (l - 1)].reshape(nh, nrows, l - 1)[:, :, :width]

    def diagonals(nrows, width):
        l = width + nrows
        k = np.arange(l)
        return np.where(k < width, k, k - l)

    def causal(heads, nrows):
        dist = TMPL_C0 - diagonals(nrows, TMPL_W)
        return toeplitz(build(dist, dist >= 0)[heads], nrows, TMPL_W)

    nsa = slice(0, NSA_HEADS)
    span = NSA_WINDOW + TQ
    dist = NSA_WINDOW - diagonals(TQ, span)
    window = toeplitz(build(dist, (dist >= 0) & (dist < NSA_WINDOW))[nsa], TQ, span)
    cc = np.arange(CMP_BAND)[None, :] - CMP_BAND // 2
    dist = np.arange(TQ)[:, None] - CMP_STRIDE * cc - (CMP_BLOCK - 1)
    band = build(dist, dist >= 0, 0.0)[nsa]
    width = 2 * ncmp - 128
    step = TQ // CMP_STRIDE
    cmp = []
    for v in range(128 // step):
        left = ncmp - 128 + step * v
        canvas = jnp.pad(band, ((0, 0), (0, 0), (left, width - left)))
        cmp.append(canvas[:, :, CMP_BAND // 2:CMP_BAND // 2 + width])
    return causal(nsa, TQ), causal(slice(NSA_HEADS, None), DSA_TQ), window, jnp.stack(cmp)


def _pack_w_in(w_in):
    d3 = 3 * w_in.shape[1]
    kv = NSA_GROUPS * HEAD_DIM
    o_q = d3
    o_kc = o_q + NSA_HEADS * HEAD_DIM
    o_vc, o_ks, o_vs, o_kw, o_vw = (o_kc + j * kv for j in range(1, 6))
    o_g = o_vw + kv
    o_sbq = o_g + 3 * NSA_HEADS
    o_sbk = o_sbq + SB_HEADS * HEAD_DIM
    o_sbv = o_sbk + SB_HEADS * HEAD_DIM
    o_dq = o_sbv + SB_HEADS * HEAD_DIM
    o_ckv = o_dq + DSA_HEADS * HEAD_DIM
    o_iq = o_ckv + DSA_KV_RANK
    o_ik = o_iq + IDX_HEADS * IDX_DIM
    o_iw = o_ik + IDX_DIM
    w = w_in.astype(MXU_DTYPE)
    c = lambda a, n: w[:, :, a:a + n]
    zeros = lambda n: jnp.zeros(w.shape[:2] + (n,), w.dtype)
    tn = PROJ_TN
    assert all(o % tn == 0 for o in (d3, o_q, o_kc, o_ks, o_kw)) and 2 * kv == tn and NSA_HEADS * HEAD_DIM == 2 * tn
    blocks_a = (o_q // tn, o_q // tn + 1, o_ks // tn, o_kw // tn)
    w_b = c(o_sbq, (3 * SB_HEADS + DSA_HEADS) * HEAD_DIM)
    gw = 3 * NSA_REP
    w_misc = jnp.concatenate([c(o_iq, IDX_HEADS * IDX_DIM), c(o_ckv, DSA_KV_RANK),
                              c(o_ik, IDX_DIM), c(o_iw, IDX_HEADS), zeros(128 - IDX_DIM - IDX_HEADS),
                              c(o_g, gw), zeros(128 - gw), c(o_g + gw, gw), zeros(128 - gw)], axis=2)
    return w, blocks_a, o_kc // tn, w_b, w_misc


def kernel(x, p, w_in, norm_mix, norm_ffn, norm_ple, norm_final, w_proj_a, w_proj_b, w_proj_c, w_out,
           cmp_k_w1, cmp_k_w2, cmp_k_pe, cmp_v_w1, cmp_v_w2, cmp_v_pe, dsa_kv_norm, dsa_w_uk, dsa_w_uv,
           rel_bias_table, ffn_w_gate, ffn_w_up, ffn_w_down, ffn_conv_w, ffn_conv_b, ple_w_gate, ple_w_proj):
    batch, seq, d_model = x.shape
    depth = w_in.shape[0]
    m = batch * seq
    assert seq % KT == 0 and seq % SB_T == 0 and seq // SLC_BLOCK <= MAX_SLC_BLOCKS
    ncmp = seq // CMP_STRIDE
    bf = lambda w: w.astype(MXU_DTYPE)

    w_all, blocks_a, block_cmp, w_hm_b, w_misc = _pack_w_in(w_in)
    a0, a1, a2, a3 = blocks_a
    nsa_block = lambda j: jnp.where(j == 0, a0, jnp.where(j == 1, a1, jnp.where(j == 2, a2, a3)))
    w_a, w_b, w_c, w_o = bf(w_proj_a), bf(w_proj_b), bf(w_proj_c), bf(w_out)
    cmp_w1 = bf(jnp.stack([cmp_k_w1, cmp_v_w1], axis=1))
    cmp_w2 = bf(jnp.stack([cmp_k_w2, cmp_v_w2], axis=1))
    cmp_pe = jnp.stack([cmp_k_pe, cmp_v_pe], axis=1).reshape(depth, 2, 2, CMP_STRIDE * HEAD_DIM)
    w_uk, w_uv = bf(dsa_w_uk), bf(dsa_w_uv)
    f_gate, f_up, f_down = bf(ffn_w_gate), bf(ffn_w_up), bf(ffn_w_down)
    pl_gate, pl_proj = bf(ple_w_gate), bf(ple_w_proj)

    wslc, wdsa, wwin, tcmp = _bias_templates(rel_bias_table, ncmp)
    cc = np.arange(ncmp)[None, :]
    jj = np.arange(128)[:, None]
    per = SLC_BLOCK // CMP_STRIDE
    ovl = ((cc >= per * jj - (CMP_BLOCK // CMP_STRIDE - 1)) & (cc <= per * jj + per - 1)
           & (cc < ncmp - 1) & (jj < seq // SLC_BLOCK))
    ovl = jnp.asarray(ovl, MXU_DTYPE)
    low = jnp.asarray(np.arange(KT)[:, None] > np.arange(KT)[None, :], MXU_DTYPE)
    upper = jnp.asarray(np.arange(SB_T)[:, None] > np.arange(SB_T)[None, :], MXU_DTYPE)

    x = x.reshape(m, d_model)
    p = p.reshape(depth, m, p.shape[-1])
    h = _rmsnorm(x, norm_mix[0], MXU_DTYPE)
    for i in range(depth):
        hm_a = _matmul_heads(h, w_all, i, nsa_block, len(blocks_a), batch, MXU_DTYPE, "in_proj_nsa")
        hm_b = _matmul_heads(h, w_hm_b, i, lambda j: j, w_hm_b.shape[2] // PROJ_TN, batch, MXU_DTYPE, "in_proj_sb_dsa")
        cmp_in = _matmul_heads(h, w_all, i, lambda j: block_cmp, 1, batch, F32, "in_proj_cmp")
        misc = _matmul(h, w_misc[i], F32, 1024, MISC_COLS // 3, "in_proj_misc")
        cmp_kv = _compress(cmp_in, cmp_w1[i], cmp_w2[i], cmp_pe[i])
        o_a = _nsa(hm_a, cmp_kv, misc, wslc, wwin, tcmp, ovl)
        o_b = _stick_breaking(hm_b, upper)
        dk, dv = _dsa_kv(misc, dsa_kv_norm[i], w_uk[i], w_uv[i])
        o_c = _dsa(hm_b, dk, dv, misc, wdsa, low)
        y = _merge(h, o_a, o_b, o_c, w_all, i, w_a[i], w_b[i], w_c[i])
        x, h = _matmul_residual(y, w_o[i], x, norm_ffn[i])
        x, h = _conv_ffn(h, x, f_gate[i], f_up[i], f_down[i], ffn_conv_w[i], ffn_conv_b[i], norm_ple[i], seq)
        last = i == depth - 1
        x, h = _ple(h, p[i], x, pl_gate[i], pl_proj[i], norm_final if last else norm_mix[i + 1],
                    F32 if last else MXU_DTYPE)
    return h.reshape(batch, seq, d_model)
```

```python
import functools
import math

import numpy as np
import jax
import jax.numpy as jnp
from jax import lax
from jax.experimental import pallas as pl
from jax.experimental.pallas import tpu as pltpu

F32 = jnp.float32
MXU_DTYPE = jnp.bfloat16

HEAD_DIM = 128
NSA_HEADS = 8
NSA_GROUPS = 2
NSA_REP = NSA_HEADS // NSA_GROUPS
CMP_BLOCK = 32
CMP_STRIDE = 16
SLC_BLOCK = 64
MAX_SLC_BLOCKS = 64
SLC_TOPN = 16
NSA_WINDOW = 512
SB_HEADS = 4
DSA_HEADS = 4
DSA_KV_RANK = 256
IDX_HEADS = 8
IDX_DIM = 64
DSA_TOPK = 256
REL_BUCKETS = 32
REL_MAX_DIST = 128
CONV_WIDTH = 3
EPS = 1e-6

MASKED = -1e30
UNSELECTED = -1e9
INT_MIN = -2 ** 31
VMEM_LIMIT = 56 * 1024 * 1024

TQ = 256
DSA_TQ = 256
KT = 512
SB_T = 256
TMPL_C0 = 896
TMPL_W = TMPL_C0 + KT
PROJ_TN = 512
FFN_TF = 512
CMP_BAND = 32

HM_NSA_Q, HM_K_SLC, HM_V_SLC, HM_K_WIN, HM_V_WIN = 0, 8, 10, 12, 14
HM_SB_Q, HM_SB_K, HM_SB_V, HM_DSA_Q = 0, 4, 8, 12
MISC_IDXQ, MISC_CKV, MISC_IDXK, MISC_GATE, MISC_COLS = 0, 4, 6, 7, 9 * 128


def _cparams(sem):
    return pltpu.CompilerParams(dimension_semantics=sem, vmem_limit_bytes=VMEM_LIMIT)


def _dot(a, b):
    return jnp.dot(a, b, preferred_element_type=F32)


def _dot_nt(a, b):
    return lax.dot_general(a, b, (((1,), (1,)), ((), ())), preferred_element_type=F32)


def _rmsnorm_kernel(x_ref, g_ref, o_ref):
    x = x_ref[...]
    y = x * lax.rsqrt(jnp.mean(x * x, axis=-1, keepdims=True) + EPS)
    o_ref[...] = (y * g_ref[...]).astype(o_ref.dtype)


def _rmsnorm(x, g, out_dtype, tm=512):
    m, d = x.shape
    return pl.pallas_call(
        _rmsnorm_kernel,
        grid=(m // tm,),
        in_specs=[pl.BlockSpec((tm, d), lambda i: (i, 0)), pl.BlockSpec((1, d), lambda i: (0, 0))],
        out_specs=pl.BlockSpec((tm, d), lambda i: (i, 0)),
        out_shape=jax.ShapeDtypeStruct((m, d), out_dtype),
        name="rmsnorm",
        compiler_params=_cparams(("parallel",)),
    )(x, g.reshape(1, d))


def _mm_kernel(a_ref, w_ref, o_ref):
    o_ref[...] = _dot(a_ref[...], w_ref[...]).astype(o_ref.dtype)


def _matmul(a, w, out_dtype, tm, tn, name):
    m, k = a.shape
    n = w.shape[1]
    return pl.pallas_call(
        _mm_kernel,
        name=name,
        grid=(m // tm, n // tn),
        in_specs=[pl.BlockSpec((tm, k), lambda i, j: (i, 0)), pl.BlockSpec((k, tn), lambda i, j: (0, j))],
        out_specs=pl.BlockSpec((tm, tn), lambda i, j: (i, j)),
        out_shape=jax.ShapeDtypeStruct((m, n), out_dtype),
        compiler_params=_cparams(("parallel", "parallel")),
    )(a, w)


def _mm_heads_kernel(a_ref, w_ref, o_ref, *, hb):
    r = _dot(a_ref[...], w_ref[0])
    for j in range(hb):
        o_ref[0, j] = r[:, j * HEAD_DIM:(j + 1) * HEAD_DIM].astype(o_ref.dtype)


def _matmul_heads(a, w, layer, col_block, nb, batch, out_dtype, name, tm=1024):
    m, k = a.shape
    s = m // batch
    tn = PROJ_TN
    hb = tn // HEAD_DIM
    spb = s // tm
    return pl.pallas_call(
        functools.partial(_mm_heads_kernel, hb=hb),
        name=name,
        grid=(batch, spb, nb),
        in_specs=[pl.BlockSpec((tm, k), lambda b, i, j: (b * spb + i, 0)),
                  pl.BlockSpec((1, k, tn), lambda b, i, j: (layer, 0, col_block(j)))],
        out_specs=pl.BlockSpec((1, hb, tm, HEAD_DIM), lambda b, i, j: (b, j, i, 0)),
        out_shape=jax.ShapeDtypeStruct((batch, nb * hb, s, HEAD_DIM), out_dtype),
        compiler_params=_cparams(("parallel", "parallel", "parallel")),
    )(a, w)


def _norm_rows(x, g):
    return x * lax.rsqrt(jnp.mean(x * x, axis=-1, keepdims=True) + EPS) * g


def _mm_res_kernel(a_ref, w_ref, x_ref, g_ref, o_ref, h_ref):
    x = x_ref[...] + _dot(a_ref[...], w_ref[...])
    o_ref[...] = x
    h_ref[...] = _norm_rows(x, g_ref[...]).astype(h_ref.dtype)


def _matmul_residual(a, w, x, g, tm=512):
    m, k = a.shape
    n = w.shape[1]
    row = lambda i: (i, 0)
    fixed = lambda i: (0, 0)
    return pl.pallas_call(
        _mm_res_kernel,
        name="out_proj_residual",
        grid=(m // tm,),
        in_specs=[pl.BlockSpec((tm, k), row),
                  pl.BlockSpec((k, n), fixed, pipeline_mode=pl.Buffered(1)),
                  pl.BlockSpec((tm, n), row),
                  pl.BlockSpec((1, n), fixed)],
        out_specs=[pl.BlockSpec((tm, n), row), pl.BlockSpec((tm, n), row)],
        out_shape=[jax.ShapeDtypeStruct((m, n), F32), jax.ShapeDtypeStruct((m, n), MXU_DTYPE)],
        compiler_params=_cparams(("parallel",)),
    )(a, w, x, g.reshape(1, n))


def _merge_kernel(h_ref, oa_ref, ob_ref, oc_ref, ga_ref, gb_ref, gc_ref, wa_ref, wb_ref, wc_ref, y_ref):
    h = h_ref[...]
    y = jax.nn.sigmoid(_dot(h, ga_ref[0])) * _dot(oa_ref[...], wa_ref[...])
    y += jax.nn.sigmoid(_dot(h, gb_ref[0])) * _dot(ob_ref[...], wb_ref[...])
    y += jax.nn.sigmoid(_dot(h, gc_ref[0])) * _dot(oc_ref[...], wc_ref[...])
    y_ref[...] = y.astype(y_ref.dtype)


def _merge(h, o_a, o_b, o_c, w_in, layer, w_a, w_b, w_c, tm=1024, tn=PROJ_TN):
    m, k = h.shape
    d = w_a.shape[1]
    nb = d // tn
    row = lambda i, j: (i, 0)
    col = lambda i, j: (0, j)
    gate = lambda branch: pl.BlockSpec((1, k, tn), lambda i, j: (layer, 0, branch * nb + j))
    return pl.pallas_call(
        _merge_kernel,
        name="branch_merge",
        grid=(m // tm, nb),
        in_specs=[pl.BlockSpec((tm, k), row),
                  pl.BlockSpec((tm, o_a.shape[1]), row), pl.BlockSpec((tm, o_b.shape[1]), row),
                  pl.BlockSpec((tm, o_c.shape[1]), row),
                  gate(0), gate(1), gate(2),
                  pl.BlockSpec((w_a.shape[0], tn), col), pl.BlockSpec((w_b.shape[0], tn), col),
                  pl.BlockSpec((w_c.shape[0], tn), col)],
        out_specs=pl.BlockSpec((tm, tn), lambda i, j: (i, j)),
        out_shape=jax.ShapeDtypeStruct((m, d), MXU_DTYPE),
        compiler_params=_cparams(("parallel", "parallel")),
    )(h, o_a, o_b, o_c, w_in, w_in, w_in, w_a, w_b, w_c)


FFN_HALO = 16


def _ffn_kernel(h_ref, hp_ref, x_ref, wg_ref, wu_ref, wd_ref, cw_ref, cb_ref, g_ref, o_ref, hn_ref, hext_ref,
                *, tiles_per_seq):
    i = pl.program_id(0)
    f = pl.program_id(1)
    tm = h_ref.shape[0]

    @pl.when(f == 0)
    def _():
        first = (i % tiles_per_seq) == 0
        hext_ref[0:FFN_HALO, :] = jnp.where(first, jnp.zeros_like(hp_ref[...]), hp_ref[...])
        hext_ref[FFN_HALO:, :] = h_ref[...]
        o_ref[...] = x_ref[...]

    tf = wg_ref.shape[1]
    halves = [slice(0, tf // 2), slice(tf // 2, tf)]
    a = [_dot(hext_ref[...], wg_ref[:, s]) for s in halves]
    u = [_dot(h_ref[...], wu_ref[:, s]) for s in halves]
    cw = cw_ref[...]
    cb = cb_ref[...]
    act = []
    for a_j, u_j, s in zip(a, u, halves):
        c = (cw[0:1, s] * a_j[FFN_HALO - 2:FFN_HALO - 2 + tm] + cw[1:2, s] * a_j[FFN_HALO - 1:FFN_HALO - 1 + tm]
             + cw[2:3, s] * a_j[FFN_HALO:FFN_HALO + tm]) + cb[:, s]
        act.append((jax.nn.gelu(c) * u_j).astype(MXU_DTYPE))
    for act_j, s in zip(act, halves):
        o_ref[...] += _dot(act_j, wd_ref[s, :])

    @pl.when(f == pl.num_programs(1) - 1)
    def _():
        hn_ref[...] = _norm_rows(o_ref[...], g_ref[...]).astype(hn_ref.dtype)


def _conv_ffn(h, x, w_gate, w_up, w_down, conv_w, conv_b, g, seq, tm=512):
    m, d = h.shape
    ff = w_gate.shape[1]
    tf = FFN_TF
    assert seq % tm == 0
    hb = tm // FFN_HALO
    return pl.pallas_call(
        functools.partial(_ffn_kernel, tiles_per_seq=seq // tm),
        name="conv_ffn",
        grid=(m // tm, ff // tf),
        in_specs=[pl.BlockSpec((tm, d), lambda i, f: (i, 0)),
                  pl.BlockSpec((FFN_HALO, d), lambda i, f: (jnp.maximum(i * hb - 1, 0), 0)),
                  pl.BlockSpec((tm, d), lambda i, f: (i, 0)),
                  pl.BlockSpec((d, tf), lambda i, f: (0, f)),
                  pl.BlockSpec((d, tf), lambda i, f: (0, f)),
                  pl.BlockSpec((tf, d), lambda i, f: (f, 0)),
                  pl.BlockSpec((CONV_WIDTH, tf), lambda i, f: (0, f)),
                  pl.BlockSpec((1, tf), lambda i, f: (0, f)),
                  pl.BlockSpec((1, d), lambda i, f: (0, 0))],
        out_specs=[pl.BlockSpec((tm, d), lambda i, f: (i, 0)), pl.BlockSpec((tm, d), lambda i, f: (i, 0))],
        out_shape=[jax.ShapeDtypeStruct((m, d), F32), jax.ShapeDtypeStruct((m, d), MXU_DTYPE)],
        scratch_shapes=[pltpu.VMEM((tm + FFN_HALO, d), MXU_DTYPE)],
        compiler_params=_cparams(("parallel", "arbitrary")),
    )(h, h, x, w_gate, w_up, w_down, conv_w, conv_b.reshape(1, ff), g.reshape(1, d))


def _ple_kernel(h_ref, p_ref, x_ref, wg_ref, wp_ref, g_ref, o_ref, hn_ref):
    gate = jax.nn.sigmoid(_dot(h_ref[...], wg_ref[...]))
    x = x_ref[...] + gate * _dot(p_ref[...].astype(MXU_DTYPE), wp_ref[...])
    o_ref[...] = x
    hn_ref[...] = _norm_rows(x, g_ref[...]).astype(hn_ref.dtype)


def _ple(h, p, x, w_gate, w_proj, g, norm_dtype, tm=512):
    m, d = h.shape
    row = lambda i: (i, 0)
    fixed = lambda i: (0, 0)
    once = pl.Buffered(1)
    return pl.pallas_call(
        _ple_kernel,
        name="ple",
        grid=(m // tm,),
        in_specs=[pl.BlockSpec((tm, d), row),
                  pl.BlockSpec((tm, p.shape[1]), row),
                  pl.BlockSpec((tm, d), row),
                  pl.BlockSpec((d, d), fixed, pipeline_mode=once),
                  pl.BlockSpec((p.shape[1], d), fixed, pipeline_mode=once),
                  pl.BlockSpec((1, d), fixed)],
        out_specs=[pl.BlockSpec((tm, d), row), pl.BlockSpec((tm, d), row)],
        out_shape=[jax.ShapeDtypeStruct((m, d), F32), jax.ShapeDtypeStruct((m, d), norm_dtype)],
        compiler_params=_cparams(("parallel",)),
    )(h, p, x, w_gate, w_proj, g.reshape(1, d))


def _compress_kernel(x_ref, w1_ref, w2_ref, pe_ref, o_ref):
    x = x_ref[0, 0]
    pe = pe_ref[0]
    half = x.shape[1]
    lo = _dot((x + pe[0:1]).astype(MXU_DTYPE), w1_ref[0, :half, :])
    hi = _dot((x + pe[1:2]).astype(MXU_DTYPE), w1_ref[0, half:, :])
    n = x.shape[0]
    hid = lo + pltpu.roll(hi, n - 1, 0)
    o_ref[0, 0, 0] = _dot(jax.nn.gelu(hid).astype(MXU_DTYPE), w2_ref[0]).astype(o_ref.dtype)


def _compress(kv_hm, w1, w2, pe):
    b, _, s, d = kv_hm.shape
    nrow = s // CMP_STRIDE
    x = kv_hm.reshape(b, 2 * NSA_GROUPS, nrow, CMP_STRIDE * d)
    return pl.pallas_call(
        _compress_kernel,
        name="nsa_compress",
        grid=(b, 2, NSA_GROUPS),
        in_specs=[pl.BlockSpec((1, 1, nrow, CMP_STRIDE * d), lambda bi, kv, g: (bi, kv * NSA_GROUPS + g, 0, 0)),
                  pl.BlockSpec((1,) + w1.shape[1:], lambda bi, kv, g: (kv, 0, 0)),
                  pl.BlockSpec((1,) + w2.shape[1:], lambda bi, kv, g: (kv, 0, 0)),
                  pl.BlockSpec((1, 2, CMP_STRIDE * d), lambda bi, kv, g: (kv, 0, 0))],
        out_specs=pl.BlockSpec((1, 1, 1, nrow, d), lambda bi, kv, g: (bi, kv, g, 0, 0)),
        out_shape=jax.ShapeDtypeStruct((b, 2, NSA_GROUPS, nrow, d), MXU_DTYPE),
        compiler_params=_cparams(("parallel", "parallel", "parallel")),
    )(x, w1, w2, pe)


def _softmax_pv(lg_ref, mx_ref, acc_ref, v_ref, v_index, n_tiles):
    rows = lg_ref.shape[0]
    m = jnp.max(mx_ref[...], axis=1, keepdims=True)
    mx_ref[...] = jnp.zeros_like(mx_ref)
    acc_ref[...] = jnp.zeros_like(acc_ref)

    def chunk(k0, width):
        p = jnp.exp(lg_ref[:, pl.ds(k0, width)] - m)
        part = p[:, 0:128]
        for c in range(1, width // 128):
            part = part + p[:, c * 128:(c + 1) * 128]
        mx_ref[...] += part
        acc_ref[...] += _dot(p.astype(MXU_DTYPE), v_ref[v_index + (pl.ds(k0, width), slice(None))])

    def body(j, carry):
        chunk(pl.multiple_of(j * (2 * KT), 2 * KT), 2 * KT)
        return carry

    lax.fori_loop(0, n_tiles // 2, body, 0)

    @pl.when(n_tiles % 2 == 1)
    def _():
        chunk(pl.multiple_of((n_tiles - 1) * KT, KT), KT)

    l = jnp.sum(mx_ref[...], axis=1, keepdims=True)
    return acc_ref[...] / l


def _lane_max(s):
    part = s[:, 0:128]
    for c in range(1, s.shape[1] // 128):
        part = jnp.maximum(part, s[:, c * 128:(c + 1) * 128])
    return part


def _nsa_kernel(q_ref, kc_ref, vc_ref, kslc_ref, vslc_ref, kwin_ref, vwin_ref, gate_ref,
                wslc_ref, wwin_ref, tcmp_ref, ovl_ref, o_ref,
                kslc_aug, kwin_aug, vwin_pad, qaug, lg_ref, mx_ref, acc_ref, score_ref):
    i = pl.program_id(2)
    seq = kslc_ref.shape[2]
    d = HEAD_DIM
    rows = NSA_REP * TQ
    scale = d ** -0.5
    t0 = i * TQ

    @pl.when(i == 0)
    def _():
        kslc_aug[:, 0:d] = kslc_ref[0, 0]
        srow = lax.broadcasted_iota(jnp.int32, (seq, d), 0)
        lane = lax.broadcasted_iota(jnp.int32, (seq, d), 1)
        kslc_aug[:, d:2 * d] = jnp.where((srow >> 6) == lane, 1.0, 0.0).astype(kslc_aug.dtype)
        kwin_aug[0:NSA_WINDOW, 0:d] = jnp.zeros((NSA_WINDOW, d), kwin_aug.dtype)
        kwin_aug[NSA_WINDOW:, 0:d] = kwin_ref[0, 0]
        prow = lax.broadcasted_iota(jnp.int32, (seq + NSA_WINDOW, d), 0)
        plane = lax.broadcasted_iota(jnp.int32, (seq + NSA_WINDOW, d), 1)
        flag = jnp.where(prow < NSA_WINDOW, jnp.where(plane == MAX_SLC_BLOCKS, UNSELECTED, 0.0), 0.0)
        kwin_aug[:, d:2 * d] = flag.astype(kwin_aug.dtype)
        vwin_pad[0:NSA_WINDOW, :] = jnp.zeros((NSA_WINDOW, d), vwin_pad.dtype)
        vwin_pad[NSA_WINDOW:, :] = vwin_ref[0, 0]

    q4 = q_ref[0].reshape(rows, d)

    ncmp = kc_ref.shape[3]
    kc = kc_ref[0, 0, 0]
    vc = vc_ref[0, 0, 0]
    trow = t0 + lax.broadcasted_iota(jnp.int32, (TQ, ncmp), 0)
    cend = lax.broadcasted_iota(jnp.int32, (TQ, ncmp), 1) * CMP_STRIDE + (CMP_BLOCK - 1)
    valid_c = cend <= trow
    sc_all = _dot_nt(q4, kc) * scale
    c0 = i * (TQ // CMP_STRIDE)
    bias_start = pl.multiple_of(ncmp - 128 - 128 * (c0 // 128), 128)
    pb = []
    for r in range(NSA_REP):
        bias = tcmp_ref[0, r, :, pl.ds(bias_start, ncmp)]
        l = jnp.where(valid_c, sc_all[r * TQ:(r + 1) * TQ] + bias, MASKED)
        m = jnp.max(l, axis=1, keepdims=True)
        e = jnp.where(valid_c, jnp.exp(l - m), 0.0)
        p = e / jnp.maximum(jnp.sum(e, axis=1, keepdims=True), 1e-30)
        pb.append(p.astype(MXU_DTYPE))
    o_cmp = [_dot(pb[r], vc) for r in range(NSA_REP)]
    imp_t = _dot_nt(ovl_ref[...], pb[0])
    for r in range(1, NSA_REP):
        imp_t = imp_t + _dot_nt(ovl_ref[...], pb[r])

    span = NSA_WINDOW + TQ
    w0 = pl.multiple_of(t0, TQ)
    one_lane = lax.broadcasted_iota(jnp.int32, (rows, d), 1) == MAX_SLC_BLOCKS
    q_win = jnp.concatenate([q4, jnp.where(one_lane, 1.0, 0.0).astype(q4.dtype)], axis=1)
    s = _dot_nt(q_win, kwin_aug[pl.ds(w0, span), :]) * scale + wwin_ref[...].reshape(rows, span)
    m = jnp.max(s, axis=1, keepdims=True)
    p = jnp.exp(s - m)
    l = jnp.sum(p, axis=1, keepdims=True)
    o_win = _dot(p.astype(MXU_DTYPE), vwin_pad[pl.ds(w0, span), :]) / l

    nblk = MAX_SLC_BLOCKS
    jt = lax.broadcasted_iota(jnp.int32, (nblk, TQ), 0)
    tt = t0 + lax.broadcasted_iota(jnp.int32, (nblk, TQ), 1)
    cur = tt >> 6
    imp_t = imp_t[0:nblk]
    score = jnp.where(jt == 0, 1e9, jnp.where(jt == cur, 1e9, jnp.where(jt == cur - 1, 1e9, imp_t)))
    score = jnp.where(jt * SLC_BLOCK <= tt, score, -jnp.inf)
    score_ref[...] = score
    sub = lax.broadcasted_iota(jnp.int32, (8, TQ), 0)
    sv = [score[8 * v:8 * v + 8] for v in range(nblk // 8)]
    beaten = [jnp.zeros((8, TQ), F32) for _ in sv]
    for jp in range(nblk):
        other = score_ref[jp:jp + 1, :]
        for v in range(nblk // 8):
            if 8 * v > jp:
                hit = other >= sv[v]
            elif 8 * v + 7 <= jp:
                hit = other > sv[v]
            else:
                tie_loses = jnp.where(sub > jp - 8 * v, 1.0, 0.0)
                beaten[v] = beaten[v] + jnp.where(other == sv[v], tie_loses, 0.0)
                hit = other > sv[v]
            beaten[v] = beaten[v] + jnp.where(hit, 1.0, 0.0)
    aug_t = jnp.concatenate([jnp.where(b < SLC_TOPN, 0.0, UNSELECTED) for b in beaten], axis=0)
    row = lax.broadcasted_iota(jnp.int32, (d - nblk, TQ), 0)
    aug_t = jnp.concatenate([aug_t, jnp.where(row == 0, 1.0, 0.0)], axis=0)
    aug = aug_t.T
    qaug[:, 0:d] = q4
    for r in range(NSA_REP):
        qaug[r * TQ:(r + 1) * TQ, d:2 * d] = aug.astype(qaug.dtype)
    qa = qaug[...]

    n_tiles = (t0 + TQ + KT - 1) // KT
    off = t0 - (n_tiles - 1) * KT
    mx_ref[...] = jnp.full(mx_ref.shape, MASKED, F32)

    def slc_tile(kt, tmpl_start, width=KT):
        k0 = pl.multiple_of(kt * KT, KT)
        s = _dot_nt(qa, kslc_aug[pl.ds(k0, width), :]) * scale
        if tmpl_start is not None:
            ts = pl.multiple_of(tmpl_start, 128)
            s = s + wslc_ref[:, :, pl.ds(ts, width)].reshape(rows, width)
        lg_ref[:, pl.ds(k0, width)] = s
        mx_ref[...] = jnp.maximum(mx_ref[...], _lane_max(s))

    n_far = jnp.maximum(n_tiles - 2, 0)

    def far_body(j, carry):
        slc_tile(2 * j, None, 2 * KT)
        return carry

    lax.fori_loop(0, n_far // 2, far_body, 0)

    @pl.when(n_far % 2 == 1)
    def _():
        slc_tile(n_far - 1, None)

    @pl.when(n_tiles >= 2)
    def _():
        slc_tile(n_tiles - 2, TMPL_C0 - off - KT)

    slc_tile(n_tiles - 1, TMPL_C0 - off)
    o_slc = _softmax_pv(lg_ref, mx_ref, acc_ref, vslc_ref, (0, 0), n_tiles)

    g = jax.nn.sigmoid(gate_ref[...])
    for r in range(NSA_REP):
        rs = slice(r * TQ, (r + 1) * TQ)
        o = (g[:, 3 * r:3 * r + 1] * o_cmp[r] + g[:, 3 * r + 1:3 * r + 2] * o_slc[rs]
             + g[:, 3 * r + 2:3 * r + 3] * o_win[rs])
        o_ref[:, r * d:(r + 1) * d] = o.astype(o_ref.dtype)


def _nsa(hm, cmp_kv, misc, wslc, wwin, tcmp, ovl):
    b, _, seq, d = hm.shape
    nq = seq // TQ
    rows = NSA_REP * TQ
    ncmp = cmp_kv.shape[3]
    kv_spec = lambda head: pl.BlockSpec((1, 1, seq, d), lambda bi, g, i: (bi, head + g, 0, 0))
    once = pl.Buffered(1)
    return pl.pallas_call(
        _nsa_kernel,
        name="nsa",
        grid=(b, NSA_GROUPS, nq),
        in_specs=[pl.BlockSpec((1, NSA_REP, TQ, d), lambda bi, g, i: (bi, g, i, 0)),
                  pl.BlockSpec((1, 1, 1, ncmp, d), lambda bi, g, i: (bi, 0, g, 0, 0)),
                  pl.BlockSpec((1, 1, 1, ncmp, d), lambda bi, g, i: (bi, 1, g, 0, 0)),
                  kv_spec(HM_K_SLC), kv_spec(HM_V_SLC), kv_spec(HM_K_WIN), kv_spec(HM_V_WIN),
                  pl.BlockSpec((TQ, 128), lambda bi, g, i: (bi * nq + i, MISC_GATE + g)),
                  pl.BlockSpec((NSA_REP, TQ, TMPL_W), lambda bi, g, i: (g, 0, 0), pipeline_mode=once),
                  pl.BlockSpec((NSA_REP, TQ, NSA_WINDOW + TQ), lambda bi, g, i: (g, 0, 0), pipeline_mode=once),
                  pl.BlockSpec((1, NSA_REP) + tcmp.shape[2:], lambda bi, g, i: (i % tcmp.shape[0], g, 0, 0)),
                  pl.BlockSpec(ovl.shape, lambda bi, g, i: (0, 0), pipeline_mode=once)],
        out_specs=pl.BlockSpec((TQ, NSA_REP * d), lambda bi, g, i: (bi * nq + i, g)),
        out_shape=jax.ShapeDtypeStruct((b * seq, NSA_HEADS * d), MXU_DTYPE),
        scratch_shapes=[pltpu.VMEM((seq, 2 * d), MXU_DTYPE),
                        pltpu.VMEM((seq + NSA_WINDOW, 2 * d), MXU_DTYPE),
                        pltpu.VMEM((seq + NSA_WINDOW, d), MXU_DTYPE),
                        pltpu.VMEM((rows, 2 * d), MXU_DTYPE),
                        pltpu.VMEM((rows, seq), F32),
                        pltpu.VMEM((rows, 128), F32),
                        pltpu.VMEM((rows, d), F32),
                        pltpu.VMEM((MAX_SLC_BLOCKS, TQ), F32)],
        compiler_params=_cparams(("parallel", "parallel", "arbitrary")),
    )(hm, cmp_kv, cmp_kv, hm, hm, hm, hm, misc, wslc, wwin, tcmp, ovl)


def _sb_kernel(q_ref, k_ref, v_ref, upper_ref, o_ref, acc_ref, carry_ref):
    i = pl.program_id(1)
    t = SB_T
    d = HEAD_DIM
    scale = d ** -0.5
    upper = upper_ref[...]
    acc_ref[...] = jnp.zeros_like(acc_ref)
    carry_ref[...] = jnp.zeros_like(carry_ref)

    heads = range(SB_HEADS)

    def tiles(kts, mask):
        k0 = [pl.multiple_of(kt * t, t) for kt in kts]
        chains = [(j, h) for j in range(len(kts)) for h in heads]
        z = {c: _dot_nt(q_ref[0, c[1]], k_ref[0, c[1], pl.ds(k0[c[0]], t), :]) * scale for c in chains}
        ls = {c: jnp.minimum(z[c], 0.0) - jnp.log(1.0 + jnp.exp(-jnp.abs(z[c]))) for c in chains}
        lk = {c: ls[c] - z[c] for c in chains}
        if mask is not None:
            lk = {c: jnp.where(mask, lk[c], 0.0) for c in chains}
        carry = {}
        for h in heads:
            run = carry_ref[h]
            for j in range(len(kts)):
                carry[(j, h)] = run
                run = run + jnp.sum(lk[(j, h)], axis=1, keepdims=True)
            carry_ref[h] = run
        hi = {c: lk[c].astype(MXU_DTYPE) for c in chains}
        lo = {c: (lk[c] - hi[c].astype(F32)).astype(MXU_DTYPE) for c in chains}
        after = {c: carry[c] + (_dot(hi[c], upper) + _dot(lo[c], upper)) for c in chains}
        w = {c: jnp.exp(ls[c] + after[c]) for c in chains}
        if mask is not None:
            w = {c: jnp.where(mask, w[c], 0.0) for c in chains}
        for j, h in chains:
            acc_ref[h] += _dot(w[(j, h)].astype(MXU_DTYPE), v_ref[0, h, pl.ds(k0[j], t), :])

    strict = lax.broadcasted_iota(jnp.int32, (t, t), 1) < lax.broadcasted_iota(jnp.int32, (t, t), 0)
    tiles([i], strict)

    def body(n, carry):
        tiles([i - 1 - 2 * n, i - 2 - 2 * n], None)
        return carry

    lax.fori_loop(0, i // 2, body, 0)

    @pl.when(i % 2 == 1)
    def _():
        tiles([0], None)

    for h in range(SB_HEADS):
        o_ref[:, h * d:(h + 1) * d] = acc_ref[h].astype(o_ref.dtype)


def _stick_breaking(hm, upper):
    b, _, seq, d = hm.shape
    nq = seq // SB_T
    return pl.pallas_call(
        _sb_kernel,
        grid=(b, nq),
        in_specs=[pl.BlockSpec((1, SB_HEADS, SB_T, d), lambda bi, i: (bi, HM_SB_Q // SB_HEADS, i, 0)),
                  pl.BlockSpec((1, SB_HEADS, seq, d), lambda bi, i: (bi, HM_SB_K // SB_HEADS, 0, 0)),
                  pl.BlockSpec((1, SB_HEADS, seq, d), lambda bi, i: (bi, HM_SB_V // SB_HEADS, 0, 0)),
                  pl.BlockSpec(upper.shape, lambda bi, i: (0, 0))],
        out_specs=pl.BlockSpec((SB_T, SB_HEADS * d), lambda bi, i: (bi * nq + i, 0)),
        out_shape=jax.ShapeDtypeStruct((b * seq, SB_HEADS * d), MXU_DTYPE),
        scratch_shapes=[pltpu.VMEM((SB_HEADS, SB_T, d), F32), pltpu.VMEM((SB_HEADS, SB_T, 1), F32)],
        name="stick_breaking",
        compiler_params=_cparams(("parallel", "parallel")),
    )(hm, hm, hm, upper)


def _dsa_kv_kernel(c_ref, g_ref, wk_ref, wv_ref, k_ref, v_ref):
    c = c_ref[...]
    y = c * lax.rsqrt(jnp.mean(c * c, axis=-1, keepdims=True) + EPS)
    y = (y * g_ref[...]).astype(MXU_DTYPE)
    k_ref[...] = _dot(y, wk_ref[...]).astype(k_ref.dtype)
    v_ref[...] = _dot(y, wv_ref[...]).astype(v_ref.dtype)


def _dsa_kv(misc, kv_norm, w_uk, w_uv, tm=512):
    m = misc.shape[0]
    r = DSA_KV_RANK
    out = jax.ShapeDtypeStruct((m, HEAD_DIM), MXU_DTYPE)
    return pl.pallas_call(
        _dsa_kv_kernel,
        name="dsa_kv",
        grid=(m // tm,),
        in_specs=[pl.BlockSpec((tm, r), lambda i: (i, MISC_CKV * 128 // r)),
                  pl.BlockSpec((1, r), lambda i: (0, 0)),
                  pl.BlockSpec((r, HEAD_DIM), lambda i: (0, 0)),
                  pl.BlockSpec((r, HEAD_DIM), lambda i: (0, 0))],
        out_specs=[pl.BlockSpec((tm, HEAD_DIM), lambda i: (i, 0)), pl.BlockSpec((tm, HEAD_DIM), lambda i: (i, 0))],
        out_shape=[out, out],
        compiler_params=_cparams(("parallel",)),
    )(misc, kv_norm.reshape(1, r), w_uk, w_uv)


PLANE_GROUPS_PER_TILE = KT // 256


def _bit_planes(words):
    a = list(words)
    j, m = 16, 0x0000FFFF
    while j:
        sh = jnp.full(a[0].shape, j, jnp.int32)
        for k in range(32):
            if not k & j:
                t = (a[k] ^ lax.shift_right_logical(a[k + j], sh)) & m
                a[k] = a[k] ^ t
                a[k + j] = a[k + j] ^ (t << j)
        j >>= 1
        m = (m ^ (m << j)) & 0xFFFFFFFF
    return a


def _dsa_kernel(q_ref, k_ref, v_ref, iq_ref, ik_ref, iw_ref, wd_ref, low_ref, o_ref,
                key_ref, plane_ref, alive_ref, add_ref, lg_ref, mx_ref, acc_ref, *, n_keep):
    i = pl.program_id(1)
    d = HEAD_DIM
    tq = DSA_TQ
    rows = DSA_HEADS * tq
    scale = d ** -0.5
    t0 = i * tq
    n_tiles = (t0 + tq + KT - 1) // KT
    off = t0 - (n_tiles - 1) * KT

    iq = iq_ref[...].astype(MXU_DTYPE)
    qh = [iq[:, h * IDX_DIM:(h + 1) * IDX_DIM] for h in range(IDX_HEADS)]
    wi_t = iw_ref[...].T * (IDX_HEADS ** -0.5) * (IDX_DIM ** -0.5)
    wh = [wi_t[IDX_DIM + h:IDX_DIM + h + 1, :] for h in range(IDX_HEADS)]
    tq_row = t0 + lax.broadcasted_iota(jnp.int32, (KT, tq), 1)
    krow = lax.broadcasted_iota(jnp.int32, (KT, tq), 0)

    def column_sum(a):
        return jnp.sum(a.reshape(KT // 8, 8, tq), axis=0)

    def score_tile(kt, causal):
        k0 = pl.multiple_of(kt * KT, KT)
        ki = ik_ref[pl.ds(k0, KT), 0:IDX_DIM].astype(MXU_DTYPE)
        dots = [_dot_nt(ki, qh[h]) for h in range(IDX_HEADS)]
        sc = wh[0] * jnp.maximum(dots[0], 0.0)
        for h in range(1, IDX_HEADS):
            sc = sc + wh[h] * jnp.maximum(dots[h], 0.0)
        sc = sc + 0.0
        if causal:
            sc = jnp.where(k0 + krow <= tq_row, sc, -jnp.inf)
        bits = lax.bitcast_convert_type(sc, jnp.int32)
        key = bits ^ ((bits >> 31) & 0x7FFFFFFF)
        key_ref[pl.ds(k0, KT), :] = key
        ukey = key ^ INT_MIN
        for g in range(PLANE_GROUPS_PER_TILE):
            words = [ukey[(32 * g + w) * 8:(32 * g + w + 1) * 8] for w in range(32)]
            for x, plane in enumerate(_bit_planes(words)):
                plane_ref[x, PLANE_GROUPS_PER_TILE * kt + g] = plane

    @pl.when((pl.program_id(0) == 0) & (i == 0))
    def _():
        plane_ref[...] = jnp.zeros_like(plane_ref)

    def score_body(kt, carry):
        score_tile(kt, False)
        return carry

    lax.fori_loop(0, n_tiles - 1, score_body, 0)
    score_tile(n_tiles - 1, True)

    ngrp = alive_ref.shape[0]
    for g in range(ngrp):
        alive_ref[g] = jnp.where(g < PLANE_GROUPS_PER_TILE * n_tiles, -1, 0) + jnp.zeros((8, tq), jnp.int32)

    def bit_body(x, carry):
        thr_u, remaining = carry
        ones = [alive_ref[g] & plane_ref[x, g] for g in range(ngrp)]
        c = lax.population_count(ones[0])
        for g in range(1, ngrp):
            c = c + lax.population_count(ones[g])
        c = jnp.sum(c, axis=0, keepdims=True)
        take = c >= remaining
        for g in range(ngrp):
            alive_ref[g] = jnp.where(take, ones[g], alive_ref[g] ^ ones[g])
        bit = jnp.int32(1) << (31 - x)
        return jnp.where(take, thr_u | bit, thr_u), jnp.where(take, remaining, remaining - c)

    thr_u, need = lax.fori_loop(0, 32, bit_body,
                                (jnp.zeros((1, tq), jnp.int32), jnp.full((1, tq), n_keep, jnp.int32)))
    thr = thr_u ^ INT_MIN
    n_equal = lax.population_count(alive_ref[0])
    for g in range(1, ngrp):
        n_equal = n_equal + lax.population_count(alive_ref[g])
    n_equal = jnp.sum(n_equal, axis=0, keepdims=True)
    need = need.astype(F32)
    surplus = jnp.max(n_equal.astype(F32) - need)

    @pl.when(surplus <= 0)
    def _():
        def mask_body(kt, carry):
            k0 = pl.multiple_of(kt * KT, KT)
            add_ref[:, pl.ds(k0, KT)] = jnp.where(key_ref[pl.ds(k0, KT), :] >= thr, 0.0, MASKED).T
            return carry

        lax.fori_loop(0, n_tiles, mask_body, 0)

    @pl.when(surplus > 0)
    def _():
        def mask_body(kt, seen):
            k0 = pl.multiple_of(kt * KT, KT)
            kk = key_ref[pl.ds(k0, KT), :]
            eq = jnp.where(kk == thr, 1.0, 0.0)
            before = seen + _dot(low_ref[...], eq.astype(MXU_DTYPE))
            tie = jnp.where(before < need, 0.0, MASKED)
            add_t = jnp.where(kk > thr, 0.0, jnp.where(kk == thr, tie, MASKED))
            add_ref[:, pl.ds(k0, KT)] = add_t.T
            return seen + jnp.sum(column_sum(eq), axis=0, keepdims=True)

        lax.fori_loop(0, n_tiles, mask_body, jnp.zeros((1, tq), F32))

    q4 = q_ref[0].reshape(rows, d)
    mx_ref[...] = jnp.full(mx_ref.shape, MASKED, F32)

    def att_tile(kt, tmpl_start, width=KT):
        k0 = pl.multiple_of(kt * KT, KT)
        s = _dot_nt(q4, k_ref[0, pl.ds(k0, width), :]) * scale
        addm = add_ref[:, pl.ds(k0, width)]
        s = s + jnp.concatenate([addm] * DSA_HEADS, axis=0)
        if tmpl_start is not None:
            ts = pl.multiple_of(tmpl_start, 128)
            s = s + wd_ref[:, :, pl.ds(ts, width)].reshape(rows, width)
        lg_ref[:, pl.ds(k0, width)] = s
        mx_ref[...] = jnp.maximum(mx_ref[...], _lane_max(s))

    n_far = jnp.maximum(n_tiles - 2, 0)

    def far_body(j, carry):
        att_tile(2 * j, None, 2 * KT)
        return carry

    lax.fori_loop(0, n_far // 2, far_body, 0)

    @pl.when(n_far % 2 == 1)
    def _():
        att_tile(n_far - 1, None)

    @pl.when(n_tiles >= 2)
    def _():
        att_tile(n_tiles - 2, TMPL_C0 - off - KT)

    att_tile(n_tiles - 1, TMPL_C0 - off)
    o = _softmax_pv(lg_ref, mx_ref, acc_ref, v_ref, (0,), n_tiles)
    for r in range(DSA_HEADS):
        o_ref[:, r * d:(r + 1) * d] = o[r * tq:(r + 1) * tq].astype(o_ref.dtype)


def _dsa(hm, k, v, misc, wdsa, low):
    b, _, seq, d = hm.shape
    tq = DSA_TQ
    nq = seq // tq
    rows = DSA_HEADS * tq
    k = k.reshape(b, seq, d)
    v = v.reshape(b, seq, d)
    n_keep = min(DSA_TOPK, seq // 4)
    ngrp = PLANE_GROUPS_PER_TILE * (seq // KT)
    return pl.pallas_call(
        functools.partial(_dsa_kernel, n_keep=n_keep),
        name="dsa",
        grid=(b, nq),
        in_specs=[pl.BlockSpec((1, DSA_HEADS, tq, d), lambda bi, i: (bi, HM_DSA_Q // DSA_HEADS, i, 0)),
                  pl.BlockSpec((1, seq, d), lambda bi, i: (bi, 0, 0)),
                  pl.BlockSpec((1, seq, d), lambda bi, i: (bi, 0, 0)),
                  pl.BlockSpec((tq, IDX_HEADS * IDX_DIM), lambda bi, i: (bi * nq + i, MISC_IDXQ)),
                  pl.BlockSpec((seq, 128), lambda bi, i: (bi, MISC_IDXK)),
                  pl.BlockSpec((tq, 128), lambda bi, i: (bi * nq + i, MISC_IDXK)),
                  pl.BlockSpec((DSA_HEADS, tq, TMPL_W), lambda bi, i: (0, 0, 0), pipeline_mode=pl.Buffered(1)),
                  pl.BlockSpec(low.shape, lambda bi, i: (0, 0), pipeline_mode=pl.Buffered(1))],
        out_specs=pl.BlockSpec((tq, DSA_HEADS * d), lambda bi, i: (bi * nq + i, 0)),
        out_shape=jax.ShapeDtypeStruct((b * seq, DSA_HEADS * d), MXU_DTYPE),
        scratch_shapes=[pltpu.VMEM((seq, tq), jnp.int32),
                        pltpu.VMEM((32, ngrp, 8, tq), jnp.int32),
                        pltpu.VMEM((ngrp, 8, tq), jnp.int32),
                        pltpu.VMEM((tq, seq), F32),
                        pltpu.VMEM((rows, seq), F32),
                        pltpu.VMEM((rows, 128), F32),
                        pltpu.VMEM((rows, d), F32)],
        compiler_params=_cparams(("arbitrary", "arbitrary")),
    )(hm, k, v, misc, misc, misc, wdsa, low)


def _t5_bucket(dist):
    n = jnp.maximum(dist, 0)
    max_exact = REL_BUCKETS // 2
    nf = jnp.maximum(n, 1).astype(F32)
    large = max_exact + (jnp.log(nf / max_exact) / math.log(REL_MAX_DIST / max_exact)
                         * (REL_BUCKETS - max_exact)).astype(jnp.int32)
    large = jnp.minimum(large, REL_BUCKETS - 1)
    return jnp.where(n < max_exact, n, large)


def _bias_templates(rel_tab, ncmp):
    far = REL_MAX_DIST
    by_dist = rel_tab[_t5_bucket(jnp.arange(far + 1))] - rel_tab[REL_BUCKETS - 1][None, :]
    by_dist = by_dist.T

    def build(dist, valid, fill=MASKED):
        t = by_dist[:, np.clip(dist, 0, far)]
        return jnp.where(valid[None], t, fill).astype(F32)

    def toeplitz(u, nrows, width):
        nh, l = u.shape
        return jnp.tile(u, (1, nrows))[:, :nrows * (l - 1)].reshape(nh, nrows, l - 1)[:, :, :width]

    def diagonals(nrows, width):
        l = width + nrows
        k = np.arange(l)
        return np.where(k < width, k, k - l)

    def causal(heads, nrows):
        dist = TMPL_C0 - diagonals(nrows, TMPL_W)
        return toeplitz(build(dist, dist >= 0)[heads], nrows, TMPL_W)

    nsa = slice(0, NSA_HEADS)
    span = NSA_WINDOW + TQ
    dist = NSA_WINDOW - diagonals(TQ, span)
    window = toeplitz(build(dist, (dist >= 0) & (dist < NSA_WINDOW))[nsa], TQ, span)
    cc = np.arange(CMP_BAND)[None, :] - CMP_BAND // 2
    dist = np.arange(TQ)[:, None] - CMP_STRIDE * cc - (CMP_BLOCK - 1)
    band = build(dist, dist >= 0, 0.0)[nsa]
    width = 2 * ncmp - 128
    step = TQ // CMP_STRIDE
    cmp = []
    for v in range(128 // step):
        left = ncmp - 128 + step * v
        canvas = jnp.pad(band, ((0, 0), (0, 0), (left, width - left)))
        cmp.append(canvas[:, :, CMP_BAND // 2:CMP_BAND // 2 + width])
    return causal(nsa, TQ), causal(slice(NSA_HEADS, None), DSA_TQ), window, jnp.stack(cmp)


def _pack_w_in(w_in):
    d3 = 3 * w_in.shape[1]
    kv = NSA_GROUPS * HEAD_DIM
    o_q = d3
    o_kc = o_q + NSA_HEADS * HEAD_DIM
    o_vc, o_ks, o_vs, o_kw, o_vw = (o_kc + j * kv for j in range(1, 6))
    o_g = o_vw + kv
    o_sbq = o_g + 3 * NSA_HEADS
    o_sbk = o_sbq + SB_HEADS * HEAD_DIM
    o_sbv = o_sbk + SB_HEADS * HEAD_DIM
    o_dq = o_sbv + SB_HEADS * HEAD_DIM
    o_ckv = o_dq + DSA_HEADS * HEAD_DIM
    o_iq = o_ckv + DSA_KV_RANK
    o_ik = o_iq + IDX_HEADS * IDX_DIM
    o_iw = o_ik + IDX_DIM
    w = w_in.astype(MXU_DTYPE)
    c = lambda a, n: w[:, :, a:a + n]
    zeros = lambda n: jnp.zeros(w.shape[:2] + (n,), w.dtype)
    tn = PROJ_TN
    assert all(o % tn == 0 for o in (d3, o_q, o_kc, o_ks, o_kw)) and 2 * kv == tn and NSA_HEADS * HEAD_DIM == 2 * tn
    blocks_a = (o_q // tn, o_q // tn + 1, o_ks // tn, o_kw // tn)
    w_b = c(o_sbq, (3 * SB_HEADS + DSA_HEADS) * HEAD_DIM)
    gw = 3 * NSA_REP
    w_misc = jnp.concatenate([c(o_iq, IDX_HEADS * IDX_DIM), c(o_ckv, DSA_KV_RANK),
                              c(o_ik, IDX_DIM), c(o_iw, IDX_HEADS), zeros(128 - IDX_DIM - IDX_HEADS),
                              c(o_g, gw), zeros(128 - gw), c(o_g + gw, gw), zeros(128 - gw)], axis=2)
    return w, blocks_a, o_kc // tn, w_b, w_misc


def kernel(x, p, w_in, norm_mix, norm_ffn, norm_ple, norm_final, w_proj_a, w_proj_b, w_proj_c, w_out,
           cmp_k_w1, cmp_k_w2, cmp_k_pe, cmp_v_w1, cmp_v_w2, cmp_v_pe, dsa_kv_norm, dsa_w_uk, dsa_w_uv,
           rel_bias_table, ffn_w_gate, ffn_w_up, ffn_w_down, ffn_conv_w, ffn_conv_b, ple_w_gate, ple_w_proj):
    batch, seq, d_model = x.shape
    depth = w_in.shape[0]
    m = batch * seq
    assert seq % KT == 0 and seq % SB_T == 0 and seq // SLC_BLOCK <= MAX_SLC_BLOCKS
    ncmp = seq // CMP_STRIDE
    bf = lambda w: w.astype(MXU_DTYPE)

    w_all, blocks_a, block_cmp, w_hm_b, w_misc = _pack_w_in(w_in)
    a0, a1, a2, a3 = blocks_a
    nsa_block = lambda j: jnp.where(j == 0, a0, jnp.where(j == 1, a1, jnp.where(j == 2, a2, a3)))
    w_a, w_b, w_c, w_o = bf(w_proj_a), bf(w_proj_b), bf(w_proj_c), bf(w_out)
    cmp_w1 = bf(jnp.stack([cmp_k_w1, cmp_v_w1], axis=1))
    cmp_w2 = bf(jnp.stack([cmp_k_w2, cmp_v_w2], axis=1))
    cmp_pe = jnp.stack([cmp_k_pe, cmp_v_pe], axis=1).reshape(depth, 2, 2, CMP_STRIDE * HEAD_DIM)
    w_uk, w_uv = bf(dsa_w_uk), bf(dsa_w_uv)
    f_gate, f_up, f_down = bf(ffn_w_gate), bf(ffn_w_up), bf(ffn_w_down)
    pl_gate, pl_proj = bf(ple_w_gate), bf(ple_w_proj)

    wslc, wdsa, wwin, tcmp = _bias_templates(rel_bias_table, ncmp)
    cc = np.arange(ncmp)[None, :]
    jj = np.arange(128)[:, None]
    per = SLC_BLOCK // CMP_STRIDE
    ovl = ((cc >= per * jj - (CMP_BLOCK // CMP_STRIDE - 1)) & (cc <= per * jj + per - 1)
           & (cc < ncmp - 1) & (jj < seq // SLC_BLOCK))
    ovl = jnp.asarray(ovl, MXU_DTYPE)
    low = jnp.asarray(np.arange(KT)[:, None] > np.arange(KT)[None, :], MXU_DTYPE)
    upper = jnp.asarray(np.arange(SB_T)[:, None] > np.arange(SB_T)[None, :], MXU_DTYPE)

    x = x.reshape(m, d_model)
    p = p.reshape(depth, m, p.shape[-1])
    h = _rmsnorm(x, norm_mix[0], MXU_DTYPE)
    for i in range(depth):
        hm_a = _matmul_heads(h, w_all, i, nsa_block, len(blocks_a), batch, MXU_DTYPE, "in_proj_nsa")
        hm_b = _matmul_heads(h, w_hm_b, i, lambda j: j, w_hm_b.shape[2] // PROJ_TN, batch, MXU_DTYPE, "in_proj_sb_dsa")
        cmp_in = _matmul_heads(h, w_all, i, lambda j: block_cmp, 1, batch, F32, "in_proj_cmp")
        misc = _matmul(h, w_misc[i], F32, 1024, MISC_COLS, "in_proj_misc")
        cmp_kv = _compress(cmp_in, cmp_w1[i], cmp_w2[i], cmp_pe[i])
        o_a = _nsa(hm_a, cmp_kv, misc, wslc, wwin, tcmp, ovl)
        o_b = _stick_breaking(hm_b, upper)
        dk, dv = _dsa_kv(misc, dsa_kv_norm[i], w_uk[i], w_uv[i])
        o_c = _dsa(hm_b, dk, dv, misc, wdsa, low)
        y = _merge(h, o_a, o_b, o_c, w_all, i, w_a[i], w_b[i], w_c[i])
        x, h = _matmul_residual(y, w_o[i], x, norm_ffn[i])
        x, h = _conv_ffn(h, x, f_gate[i], f_up[i], f_down[i], ffn_conv_w[i], ffn_conv_b[i], norm_ple[i], seq)
        last = i == depth - 1
        x, h = _ple(h, p[i], x, pl_gate[i], pl_proj[i], norm_final if last else norm_mix[i + 1],
                    F32 if last else MXU_DTYPE)
    return h.reshape(batch, seq, d_model)
```

```python
import functools
import math

import numpy as np
import jax
import jax.numpy as jnp
from jax import lax
from jax.experimental import pallas as pl
from jax.experimental.pallas import tpu as pltpu

F32 = jnp.float32
MXU_DTYPE = jnp.bfloat16

HEAD_DIM = 128
NSA_HEADS = 8
NSA_GROUPS = 2
NSA_REP = NSA_HEADS // NSA_GROUPS
CMP_BLOCK = 32
CMP_STRIDE = 16
SLC_BLOCK = 64
MAX_SLC_BLOCKS = 64
SLC_TOPN = 16
NSA_WINDOW = 512
SB_HEADS = 4
DSA_HEADS = 4
DSA_KV_RANK = 256
IDX_HEADS = 8
IDX_DIM = 64
DSA_TOPK = 256
REL_BUCKETS = 32
REL_MAX_DIST = 128
CONV_WIDTH = 3
EPS = 1e-6

MASKED = -1e30
UNSELECTED = -1e9
INT_MIN = -2 ** 31
LOG2E = math.log2(math.e)
VMEM_LIMIT = 56 * 1024 * 1024

TQ = 256
DSA_TQ = 256
KT = 512
SB_T = 256
SB_HEAD_GROUPS = ((0, 1), (2, 3))
TMPL_C0 = 896
TMPL_W = TMPL_C0 + KT
PROJ_TN = 512
FFN_TF = 512
CMP_BAND = 32

HM_NSA_Q, HM_K_SLC, HM_V_SLC, HM_K_WIN, HM_V_WIN = 0, 8, 10, 12, 14
HM_SB_Q, HM_SB_K, HM_SB_V, HM_DSA_Q = 0, 4, 8, 12
MISC_IDXQ, MISC_CKV, MISC_IDXK, MISC_GATE, MISC_COLS = 0, 4, 6, 7, 9 * 128


def _cparams(sem):
    return pltpu.CompilerParams(dimension_semantics=sem, vmem_limit_bytes=VMEM_LIMIT)


def _dot(a, b):
    return jnp.dot(a, b, preferred_element_type=F32)


def _dot_nt(a, b):
    return lax.dot_general(a, b, (((1,), (1,)), ((), ())), preferred_element_type=F32)


def _rmsnorm_kernel(x_ref, g_ref, o_ref):
    x = x_ref[...]
    y = x * lax.rsqrt(jnp.mean(x * x, axis=-1, keepdims=True) + EPS)
    o_ref[...] = (y * g_ref[...]).astype(o_ref.dtype)


def _rmsnorm(x, g, out_dtype, tm=512):
    m, d = x.shape
    return pl.pallas_call(
        _rmsnorm_kernel,
        grid=(m // tm,),
        in_specs=[pl.BlockSpec((tm, d), lambda i: (i, 0)), pl.BlockSpec((1, d), lambda i: (0, 0))],
        out_specs=pl.BlockSpec((tm, d), lambda i: (i, 0)),
        out_shape=jax.ShapeDtypeStruct((m, d), out_dtype),
        name="rmsnorm",
        compiler_params=_cparams(("parallel",)),
    )(x, g.reshape(1, d))


def _mm_kernel(a_ref, w_ref, o_ref):
    o_ref[...] = _dot(a_ref[...], w_ref[...]).astype(o_ref.dtype)


def _matmul(a, w, out_dtype, tm, tn, name):
    m, k = a.shape
    n = w.shape[1]
    return pl.pallas_call(
        _mm_kernel,
        name=name,
        grid=(m // tm, n // tn),
        in_specs=[pl.BlockSpec((tm, k), lambda i, j: (i, 0)), pl.BlockSpec((k, tn), lambda i, j: (0, j))],
        out_specs=pl.BlockSpec((tm, tn), lambda i, j: (i, j)),
        out_shape=jax.ShapeDtypeStruct((m, n), out_dtype),
        compiler_params=_cparams(("parallel", "parallel")),
    )(a, w)


def _mm_heads_kernel(a_ref, w_ref, o_ref, *, hb):
    r = _dot(a_ref[...], w_ref[0])
    for j in range(hb):
        o_ref[0, j] = r[:, j * HEAD_DIM:(j + 1) * HEAD_DIM].astype(o_ref.dtype)


def _matmul_heads(a, w, layer, col_block, nb, batch, out_dtype, name, tm=1024):
    m, k = a.shape
    s = m // batch
    tn = PROJ_TN
    hb = tn // HEAD_DIM
    spb = s // tm
    return pl.pallas_call(
        functools.partial(_mm_heads_kernel, hb=hb),
        name=name,
        grid=(batch, spb, nb),
        in_specs=[pl.BlockSpec((tm, k), lambda b, i, j: (b * spb + i, 0)),
                  pl.BlockSpec((1, k, tn), lambda b, i, j: (layer, 0, col_block(j)))],
        out_specs=pl.BlockSpec((1, hb, tm, HEAD_DIM), lambda b, i, j: (b, j, i, 0)),
        out_shape=jax.ShapeDtypeStruct((batch, nb * hb, s, HEAD_DIM), out_dtype),
        compiler_params=_cparams(("parallel", "parallel", "parallel")),
    )(a, w)


def _norm_rows(x, g):
    return x * lax.rsqrt(jnp.mean(x * x, axis=-1, keepdims=True) + EPS) * g


def _mm_res_kernel(a_ref, w_ref, x_ref, g_ref, o_ref, h_ref):
    x = x_ref[...] + _dot(a_ref[...], w_ref[...])
    o_ref[...] = x
    h_ref[...] = _norm_rows(x, g_ref[...]).astype(h_ref.dtype)


def _matmul_residual(a, w, x, g, tm=512):
    m, k = a.shape
    n = w.shape[1]
    row = lambda i: (i, 0)
    fixed = lambda i: (0, 0)
    return pl.pallas_call(
        _mm_res_kernel,
        name="out_proj_residual",
        grid=(m // tm,),
        in_specs=[pl.BlockSpec((tm, k), row),
                  pl.BlockSpec((k, n), fixed, pipeline_mode=pl.Buffered(1)),
                  pl.BlockSpec((tm, n), row),
                  pl.BlockSpec((1, n), fixed)],
        out_specs=[pl.BlockSpec((tm, n), row), pl.BlockSpec((tm, n), row)],
        out_shape=[jax.ShapeDtypeStruct((m, n), F32), jax.ShapeDtypeStruct((m, n), MXU_DTYPE)],
        compiler_params=_cparams(("parallel",)),
    )(a, w, x, g.reshape(1, n))


def _merge_kernel(h_ref, oa_ref, ob_ref, oc_ref, ga_ref, gb_ref, gc_ref, wa_ref, wb_ref, wc_ref, y_ref):
    h = h_ref[...]
    y = jax.nn.sigmoid(_dot(h, ga_ref[0])) * _dot(oa_ref[...], wa_ref[...])
    y += jax.nn.sigmoid(_dot(h, gb_ref[0])) * _dot(ob_ref[...], wb_ref[...])
    y += jax.nn.sigmoid(_dot(h, gc_ref[0])) * _dot(oc_ref[...], wc_ref[...])
    y_ref[...] = y.astype(y_ref.dtype)


def _merge(h, o_a, o_b, o_c, w_in, layer, w_a, w_b, w_c, tm=1024, tn=PROJ_TN):
    m, k = h.shape
    d = w_a.shape[1]
    nb = d // tn
    row = lambda i, j: (i, 0)
    col = lambda i, j: (0, j)
    gate = lambda branch: pl.BlockSpec((1, k, tn), lambda i, j: (layer, 0, branch * nb + j))
    return pl.pallas_call(
        _merge_kernel,
        name="branch_merge",
        grid=(m // tm, nb),
        in_specs=[pl.BlockSpec((tm, k), row),
                  pl.BlockSpec((tm, o_a.shape[1]), row), pl.BlockSpec((tm, o_b.shape[1]), row),
                  pl.BlockSpec((tm, o_c.shape[1]), row),
                  gate(0), gate(1), gate(2),
                  pl.BlockSpec((w_a.shape[0], tn), col), pl.BlockSpec((w_b.shape[0], tn), col),
                  pl.BlockSpec((w_c.shape[0], tn), col)],
        out_specs=pl.BlockSpec((tm, tn), lambda i, j: (i, j)),
        out_shape=jax.ShapeDtypeStruct((m, d), MXU_DTYPE),
        compiler_params=_cparams(("parallel", "parallel")),
    )(h, o_a, o_b, o_c, w_in, w_in, w_in, w_a, w_b, w_c)


FFN_HALO = 16


def _ffn_kernel(h_ref, hp_ref, x_ref, wg_ref, wu_ref, wd_ref, cw_ref, cb_ref, g_ref, o_ref, hn_ref, hext_ref,
                *, tiles_per_seq):
    i = pl.program_id(0)
    f = pl.program_id(1)
    tm = h_ref.shape[0]

    @pl.when(f == 0)
    def _():
        first = (i % tiles_per_seq) == 0
        hext_ref[0:FFN_HALO, :] = jnp.where(first, jnp.zeros_like(hp_ref[...]), hp_ref[...])
        hext_ref[FFN_HALO:, :] = h_ref[...]
        o_ref[...] = x_ref[...]

    tf = wg_ref.shape[1]
    halves = [slice(0, tf // 2), slice(tf // 2, tf)]
    a = [_dot(hext_ref[...], wg_ref[:, s]) for s in halves]
    u = [_dot(h_ref[...], wu_ref[:, s]) for s in halves]
    cw = cw_ref[...]
    cb = cb_ref[...]
    act = []
    for a_j, u_j, s in zip(a, u, halves):
        c = (cw[0:1, s] * a_j[FFN_HALO - 2:FFN_HALO - 2 + tm] + cw[1:2, s] * a_j[FFN_HALO - 1:FFN_HALO - 1 + tm]
             + cw[2:3, s] * a_j[FFN_HALO:FFN_HALO + tm]) + cb[:, s]
        act.append((jax.nn.gelu(c) * u_j).astype(MXU_DTYPE))
    for act_j, s in zip(act, halves):
        o_ref[...] += _dot(act_j, wd_ref[s, :])

    @pl.when(f == pl.num_programs(1) - 1)
    def _():
        hn_ref[...] = _norm_rows(o_ref[...], g_ref[...]).astype(hn_ref.dtype)


def _conv_ffn(h, x, w_gate, w_up, w_down, conv_w, conv_b, g, seq, tm=512):
    m, d = h.shape
    ff = w_gate.shape[1]
    tf = FFN_TF
    assert seq % tm == 0
    hb = tm // FFN_HALO
    return pl.pallas_call(
        functools.partial(_ffn_kernel, tiles_per_seq=seq // tm),
        name="conv_ffn",
        grid=(m // tm, ff // tf),
        in_specs=[pl.BlockSpec((tm, d), lambda i, f: (i, 0)),
                  pl.BlockSpec((FFN_HALO, d), lambda i, f: (jnp.maximum(i * hb - 1, 0), 0)),
                  pl.BlockSpec((tm, d), lambda i, f: (i, 0)),
                  pl.BlockSpec((d, tf), lambda i, f: (0, f)),
                  pl.BlockSpec((d, tf), lambda i, f: (0, f)),
                  pl.BlockSpec((tf, d), lambda i, f: (f, 0)),
                  pl.BlockSpec((CONV_WIDTH, tf), lambda i, f: (0, f)),
                  pl.BlockSpec((1, tf), lambda i, f: (0, f)),
                  pl.BlockSpec((1, d), lambda i, f: (0, 0))],
        out_specs=[pl.BlockSpec((tm, d), lambda i, f: (i, 0)), pl.BlockSpec((tm, d), lambda i, f: (i, 0))],
        out_shape=[jax.ShapeDtypeStruct((m, d), F32), jax.ShapeDtypeStruct((m, d), MXU_DTYPE)],
        scratch_shapes=[pltpu.VMEM((tm + FFN_HALO, d), MXU_DTYPE)],
        compiler_params=_cparams(("parallel", "arbitrary")),
    )(h, h, x, w_gate, w_up, w_down, conv_w, conv_b.reshape(1, ff), g.reshape(1, d))


def _ple_kernel(h_ref, p_ref, x_ref, wg_ref, wp_ref, g_ref, o_ref, hn_ref):
    gate = jax.nn.sigmoid(_dot(h_ref[...], wg_ref[...]))
    x = x_ref[...] + gate * _dot(p_ref[...].astype(MXU_DTYPE), wp_ref[...])
    o_ref[...] = x
    hn_ref[...] = _norm_rows(x, g_ref[...]).astype(hn_ref.dtype)


def _ple(h, p, x, w_gate, w_proj, g, norm_dtype, tm=512):
    m, d = h.shape
    row = lambda i: (i, 0)
    fixed = lambda i: (0, 0)
    once = pl.Buffered(1)
    return pl.pallas_call(
        _ple_kernel,
        name="ple",
        grid=(m // tm,),
        in_specs=[pl.BlockSpec((tm, d), row),
                  pl.BlockSpec((tm, p.shape[1]), row),
                  pl.BlockSpec((tm, d), row),
                  pl.BlockSpec((d, d), fixed, pipeline_mode=once),
                  pl.BlockSpec((p.shape[1], d), fixed, pipeline_mode=once),
                  pl.BlockSpec((1, d), fixed)],
        out_specs=[pl.BlockSpec((tm, d), row), pl.BlockSpec((tm, d), row)],
        out_shape=[jax.ShapeDtypeStruct((m, d), F32), jax.ShapeDtypeStruct((m, d), norm_dtype)],
        compiler_params=_cparams(("parallel",)),
    )(h, p, x, w_gate, w_proj, g.reshape(1, d))


def _compress_kernel(x_ref, w1_ref, w2_ref, pe_ref, o_ref):
    x = x_ref[0, 0]
    pe = pe_ref[0]
    half = x.shape[1]
    lo = _dot((x + pe[0:1]).astype(MXU_DTYPE), w1_ref[0, :half, :])
    hi = _dot((x + pe[1:2]).astype(MXU_DTYPE), w1_ref[0, half:, :])
    n = x.shape[0]
    hid = lo + pltpu.roll(hi, n - 1, 0)
    o_ref[0, 0, 0] = _dot(jax.nn.gelu(hid).astype(MXU_DTYPE), w2_ref[0]).astype(o_ref.dtype)


def _compress(kv_hm, w1, w2, pe):
    b, _, s, d = kv_hm.shape
    nrow = s // CMP_STRIDE
    x = kv_hm.reshape(b, 2 * NSA_GROUPS, nrow, CMP_STRIDE * d)
    return pl.pallas_call(
        _compress_kernel,
        name="nsa_compress",
        grid=(b, 2, NSA_GROUPS),
        in_specs=[pl.BlockSpec((1, 1, nrow, CMP_STRIDE * d), lambda bi, kv, g: (bi, kv * NSA_GROUPS + g, 0, 0)),
                  pl.BlockSpec((1,) + w1.shape[1:], lambda bi, kv, g: (kv, 0, 0)),
                  pl.BlockSpec((1,) + w2.shape[1:], lambda bi, kv, g: (kv, 0, 0)),
                  pl.BlockSpec((1, 2, CMP_STRIDE * d), lambda bi, kv, g: (kv, 0, 0))],
        out_specs=pl.BlockSpec((1, 1, 1, nrow, d), lambda bi, kv, g: (bi, kv, g, 0, 0)),
        out_shape=jax.ShapeDtypeStruct((b, 2, NSA_GROUPS, nrow, d), MXU_DTYPE),
        compiler_params=_cparams(("parallel", "parallel", "parallel")),
    )(x, w1, w2, pe)


def _softmax_pv(lg_ref, mx_ref, acc_ref, v_ref, v_index, n_tiles):
    rows = lg_ref.shape[0]
    m = jnp.max(mx_ref[...], axis=1, keepdims=True)
    mx_ref[...] = jnp.zeros_like(mx_ref)
    acc_ref[...] = jnp.zeros_like(acc_ref)

    def chunk(k0, width):
        p = jnp.exp2(lg_ref[:, pl.ds(k0, width)] - m)
        part = p[:, 0:128]
        for c in range(1, width // 128):
            part = part + p[:, c * 128:(c + 1) * 128]
        mx_ref[...] += part
        acc_ref[...] += _dot(p.astype(MXU_DTYPE), v_ref[v_index + (pl.ds(k0, width), slice(None))])

    def body(j, carry):
        chunk(pl.multiple_of(j * (2 * KT), 2 * KT), 2 * KT)
        return carry

    lax.fori_loop(0, n_tiles // 2, body, 0)

    @pl.when(n_tiles % 2 == 1)
    def _():
        chunk(pl.multiple_of((n_tiles - 1) * KT, KT), KT)

    l = jnp.sum(mx_ref[...], axis=1, keepdims=True)
    return acc_ref[...] / l


def _lane_max(s):
    part = s[:, 0:128]
    for c in range(1, s.shape[1] // 128):
        part = jnp.maximum(part, s[:, c * 128:(c + 1) * 128])
    return part


def _nsa_kernel(q_ref, kc_ref, vc_ref, kslc_ref, vslc_ref, kwin_ref, vwin_ref, gate_ref,
                wslc_ref, wwin_ref, tcmp_ref, ovl_ref, o_ref,
                kslc_aug, kwin_aug, vwin_pad, qaug, lg_ref, mx_ref, acc_ref, score_ref):
    i = pl.program_id(2)
    seq = kslc_ref.shape[2]
    d = HEAD_DIM
    rows = NSA_REP * TQ
    scale = d ** -0.5 * LOG2E
    t0 = i * TQ

    @pl.when(i == 0)
    def _():
        kslc_aug[:, 0:d] = kslc_ref[0, 0]
        srow = lax.broadcasted_iota(jnp.int32, (seq, d), 0)
        lane = lax.broadcasted_iota(jnp.int32, (seq, d), 1)
        kslc_aug[:, d:2 * d] = jnp.where((srow >> 6) == lane, 1.0, 0.0).astype(kslc_aug.dtype)
        kwin_aug[0:NSA_WINDOW, 0:d] = jnp.zeros((NSA_WINDOW, d), kwin_aug.dtype)
        kwin_aug[NSA_WINDOW:, 0:d] = kwin_ref[0, 0]
        prow = lax.broadcasted_iota(jnp.int32, (seq + NSA_WINDOW, d), 0)
        plane = lax.broadcasted_iota(jnp.int32, (seq + NSA_WINDOW, d), 1)
        flag = jnp.where(prow < NSA_WINDOW, jnp.where(plane == MAX_SLC_BLOCKS, UNSELECTED, 0.0), 0.0)
        kwin_aug[:, d:2 * d] = flag.astype(kwin_aug.dtype)
        vwin_pad[0:NSA_WINDOW, :] = jnp.zeros((NSA_WINDOW, d), vwin_pad.dtype)
        vwin_pad[NSA_WINDOW:, :] = vwin_ref[0, 0]

    q4 = q_ref[0].reshape(rows, d)

    ncmp = kc_ref.shape[3]
    kc = kc_ref[0, 0, 0]
    vc = vc_ref[0, 0, 0]
    trow = t0 + lax.broadcasted_iota(jnp.int32, (TQ, ncmp), 0)
    cend = lax.broadcasted_iota(jnp.int32, (TQ, ncmp), 1) * CMP_STRIDE + (CMP_BLOCK - 1)
    valid_c = cend <= trow
    sc_all = _dot_nt(q4, kc) * scale
    c0 = i * (TQ // CMP_STRIDE)
    bias_start = pl.multiple_of(ncmp - 128 - 128 * (c0 // 128), 128)
    pb = []
    for r in range(NSA_REP):
        bias = tcmp_ref[0, r, :, pl.ds(bias_start, ncmp)]
        l = jnp.where(valid_c, sc_all[r * TQ:(r + 1) * TQ] + bias, MASKED)
        m = jnp.max(l, axis=1, keepdims=True)
        e = jnp.where(valid_c, jnp.exp2(l - m), 0.0)
        p = e / jnp.maximum(jnp.sum(e, axis=1, keepdims=True), 1e-30)
        pb.append(p.astype(MXU_DTYPE))
    o_cmp = [_dot(pb[r], vc) for r in range(NSA_REP)]
    imp_t = _dot_nt(ovl_ref[...], pb[0])
    for r in range(1, NSA_REP):
        imp_t = imp_t + _dot_nt(ovl_ref[...], pb[r])

    span = NSA_WINDOW + TQ
    w0 = pl.multiple_of(t0, TQ)
    one_lane = lax.broadcasted_iota(jnp.int32, (rows, d), 1) == MAX_SLC_BLOCKS
    q_win = jnp.concatenate([q4, jnp.where(one_lane, 1.0, 0.0).astype(q4.dtype)], axis=1)
    s = _dot_nt(q_win, kwin_aug[pl.ds(w0, span), :]) * scale + wwin_ref[...].reshape(rows, span)
    m = jnp.max(s, axis=1, keepdims=True)
    p = jnp.exp2(s - m)
    l = jnp.sum(p, axis=1, keepdims=True)
    o_win = _dot(p.astype(MXU_DTYPE), vwin_pad[pl.ds(w0, span), :]) / l

    nblk = MAX_SLC_BLOCKS
    jt = lax.broadcasted_iota(jnp.int32, (nblk, TQ), 0)
    tt = t0 + lax.broadcasted_iota(jnp.int32, (nblk, TQ), 1)
    cur = tt >> 6
    imp_t = imp_t[0:nblk]
    score = jnp.where(jt == 0, 1e9, jnp.where(jt == cur, 1e9, jnp.where(jt == cur - 1, 1e9, imp_t)))
    score = jnp.where(jt * SLC_BLOCK <= tt, score, -jnp.inf)
    score_ref[...] = score
    sub = lax.broadcasted_iota(jnp.int32, (8, TQ), 0)
    sv = [score[8 * v:8 * v + 8] for v in range(nblk // 8)]
    beaten = [jnp.zeros((8, TQ), F32) for _ in sv]
    for jp in range(nblk):
        other = score_ref[jp:jp + 1, :]
        for v in range(nblk // 8):
            if 8 * v > jp:
                hit = other >= sv[v]
            elif 8 * v + 7 <= jp:
                hit = other > sv[v]
            else:
                tie_loses = jnp.where(sub > jp - 8 * v, 1.0, 0.0)
                beaten[v] = beaten[v] + jnp.where(other == sv[v], tie_loses, 0.0)
                hit = other > sv[v]
            beaten[v] = beaten[v] + jnp.where(hit, 1.0, 0.0)
    aug_t = jnp.concatenate([jnp.where(b < SLC_TOPN, 0.0, UNSELECTED) for b in beaten], axis=0)
    row = lax.broadcasted_iota(jnp.int32, (d - nblk, TQ), 0)
    aug_t = jnp.concatenate([aug_t, jnp.where(row == 0, 1.0, 0.0)], axis=0)
    aug = aug_t.T
    qaug[:, 0:d] = q4
    for r in range(NSA_REP):
        qaug[r * TQ:(r + 1) * TQ, d:2 * d] = aug.astype(qaug.dtype)
    qa = qaug[...]

    n_tiles = (t0 + TQ + KT - 1) // KT
    off = t0 - (n_tiles - 1) * KT
    mx_ref[...] = jnp.full(mx_ref.shape, MASKED, F32)

    def slc_tile(kt, tmpl_start, width=KT):
        k0 = pl.multiple_of(kt * KT, KT)
        s = _dot_nt(qa, kslc_aug[pl.ds(k0, width), :]) * scale
        if tmpl_start is not None:
            ts = pl.multiple_of(tmpl_start, 128)
            s = s + wslc_ref[:, :, pl.ds(ts, width)].reshape(rows, width)
        lg_ref[:, pl.ds(k0, width)] = s
        mx_ref[...] = jnp.maximum(mx_ref[...], _lane_max(s))

    n_far = jnp.maximum(n_tiles - 2, 0)

    def far_body(j, carry):
        slc_tile(2 * j, None, 2 * KT)
        return carry

    lax.fori_loop(0, n_far // 2, far_body, 0)

    @pl.when(n_far % 2 == 1)
    def _():
        slc_tile(n_far - 1, None)

    @pl.when(n_tiles >= 2)
    def _():
        slc_tile(n_tiles - 2, TMPL_C0 - off - KT)

    slc_tile(n_tiles - 1, TMPL_C0 - off)
    o_slc = _softmax_pv(lg_ref, mx_ref, acc_ref, vslc_ref, (0, 0), n_tiles)

    g = jax.nn.sigmoid(gate_ref[...])
    for r in range(NSA_REP):
        rs = slice(r * TQ, (r + 1) * TQ)
        o = (g[:, 3 * r:3 * r + 1] * o_cmp[r] + g[:, 3 * r + 1:3 * r + 2] * o_slc[rs]
             + g[:, 3 * r + 2:3 * r + 3] * o_win[rs])
        o_ref[:, r * d:(r + 1) * d] = o.astype(o_ref.dtype)


def _nsa(hm, cmp_kv, misc, wslc, wwin, tcmp, ovl):
    b, _, seq, d = hm.shape
    nq = seq // TQ
    rows = NSA_REP * TQ
    ncmp = cmp_kv.shape[3]
    kv_spec = lambda head: pl.BlockSpec((1, 1, seq, d), lambda bi, g, i: (bi, head + g, 0, 0))
    once = pl.Buffered(1)
    return pl.pallas_call(
        _nsa_kernel,
        name="nsa",
        grid=(b, NSA_GROUPS, nq),
        in_specs=[pl.BlockSpec((1, NSA_REP, TQ, d), lambda bi, g, i: (bi, g, i, 0)),
                  pl.BlockSpec((1, 1, 1, ncmp, d), lambda bi, g, i: (bi, 0, g, 0, 0)),
                  pl.BlockSpec((1, 1, 1, ncmp, d), lambda bi, g, i: (bi, 1, g, 0, 0)),
                  kv_spec(HM_K_SLC), kv_spec(HM_V_SLC), kv_spec(HM_K_WIN), kv_spec(HM_V_WIN),
                  pl.BlockSpec((TQ, 128), lambda bi, g, i: (bi * nq + i, MISC_GATE + g)),
                  pl.BlockSpec((NSA_REP, TQ, TMPL_W), lambda bi, g, i: (g, 0, 0), pipeline_mode=once),
                  pl.BlockSpec((NSA_REP, TQ, NSA_WINDOW + TQ), lambda bi, g, i: (g, 0, 0), pipeline_mode=once),
                  pl.BlockSpec((1, NSA_REP) + tcmp.shape[2:], lambda bi, g, i: (i % tcmp.shape[0], g, 0, 0)),
                  pl.BlockSpec(ovl.shape, lambda bi, g, i: (0, 0), pipeline_mode=once)],
        out_specs=pl.BlockSpec((TQ, NSA_REP * d), lambda bi, g, i: (bi * nq + i, g)),
        out_shape=jax.ShapeDtypeStruct((b * seq, NSA_HEADS * d), MXU_DTYPE),
        scratch_shapes=[pltpu.VMEM((seq, 2 * d), MXU_DTYPE),
                        pltpu.VMEM((seq + NSA_WINDOW, 2 * d), MXU_DTYPE),
                        pltpu.VMEM((seq + NSA_WINDOW, d), MXU_DTYPE),
                        pltpu.VMEM((rows, 2 * d), MXU_DTYPE),
                        pltpu.VMEM((rows, seq), F32),
                        pltpu.VMEM((rows, 128), F32),
                        pltpu.VMEM((rows, d), F32),
                        pltpu.VMEM((MAX_SLC_BLOCKS, TQ), F32)],
        compiler_params=_cparams(("parallel", "parallel", "arbitrary")),
    )(hm, cmp_kv, cmp_kv, hm, hm, hm, hm, misc, wslc, wwin, tcmp, ovl)


def _sb_kernel(q_ref, k_ref, v_ref, upper_ref, o_ref, acc_ref, carry_ref):
    i = pl.program_id(1)
    t = SB_T
    d = HEAD_DIM
    scale = d ** -0.5 * LOG2E
    upper = upper_ref[...]
    acc_ref[...] = jnp.zeros_like(acc_ref)
    carry_ref[...] = jnp.zeros_like(carry_ref)

    heads = range(SB_HEADS)

    def tiles(kts, mask, heads=heads):
        k0 = [pl.multiple_of(kt * t, t) for kt in kts]
        chains = [(j, h) for j in range(len(kts)) for h in heads]
        z = {c: _dot_nt(q_ref[0, c[1]], k_ref[0, c[1], pl.ds(k0[c[0]], t), :]) * scale for c in chains}
        neg_abs = {c: lax.bitcast_convert_type(lax.bitcast_convert_type(z[c], jnp.int32) | INT_MIN, F32)
                   for c in chains}
        ls = {c: jnp.minimum(z[c], 0.0) - jnp.log2(1.0 + jnp.exp2(neg_abs[c])) for c in chains}
        lk = {c: ls[c] - z[c] for c in chains}
        if mask is not None:
            lk = {c: jnp.where(mask, lk[c], 0.0) for c in chains}
        carry = {}
        for h in heads:
            run = carry_ref[h]
            for j in range(len(kts)):
                carry[(j, h)] = run
                run = run + jnp.sum(lk[(j, h)], axis=1, keepdims=True)
            carry_ref[h] = run
        hi = {c: lk[c].astype(MXU_DTYPE) for c in chains}
        lo = {c: (lk[c] - hi[c].astype(F32)).astype(MXU_DTYPE) for c in chains}
        after = {c: carry[c] + (_dot(hi[c], upper) + _dot(lo[c], upper)) for c in chains}
        w = {c: jnp.exp2(ls[c] + after[c]) for c in chains}
        if mask is not None:
            w = {c: jnp.where(mask, w[c], 0.0) for c in chains}
        for j, h in chains:
            acc_ref[h] += _dot(w[(j, h)].astype(MXU_DTYPE), v_ref[0, h, pl.ds(k0[j], t), :])

    strict = lax.broadcasted_iota(jnp.int32, (t, t), 1) < lax.broadcasted_iota(jnp.int32, (t, t), 0)
    tiles([i], strict)

    def body(n, carry):
        for hs in SB_HEAD_GROUPS:
            tiles([i - 1 - 2 * n, i - 2 - 2 * n], None, hs)
        return carry

    lax.fori_loop(0, i // 2, body, 0)

    @pl.when(i % 2 == 1)
    def _():
        tiles([0], None)

    for h in range(SB_HEADS):
        o_ref[:, h * d:(h + 1) * d] = acc_ref[h].astype(o_ref.dtype)


def _stick_breaking(hm, upper):
    b, _, seq, d = hm.shape
    nq = seq // SB_T
    return pl.pallas_call(
        _sb_kernel,
        grid=(b, nq),
        in_specs=[pl.BlockSpec((1, SB_HEADS, SB_T, d), lambda bi, i: (bi, HM_SB_Q // SB_HEADS, i, 0)),
                  pl.BlockSpec((1, SB_HEADS, seq, d), lambda bi, i: (bi, HM_SB_K // SB_HEADS, 0, 0)),
                  pl.BlockSpec((1, SB_HEADS, seq, d), lambda bi, i: (bi, HM_SB_V // SB_HEADS, 0, 0)),
                  pl.BlockSpec(upper.shape, lambda bi, i: (0, 0))],
        out_specs=pl.BlockSpec((SB_T, SB_HEADS * d), lambda bi, i: (bi * nq + i, 0)),
        out_shape=jax.ShapeDtypeStruct((b * seq, SB_HEADS * d), MXU_DTYPE),
        scratch_shapes=[pltpu.VMEM((SB_HEADS, SB_T, d), F32), pltpu.VMEM((SB_HEADS, SB_T, 1), F32)],
        name="stick_breaking",
        compiler_params=_cparams(("parallel", "parallel")),
    )(hm, hm, hm, upper)


def _dsa_kv_kernel(c_ref, g_ref, wk_ref, wv_ref, k_ref, v_ref):
    c = c_ref[...]
    y = c * lax.rsqrt(jnp.mean(c * c, axis=-1, keepdims=True) + EPS)
    y = (y * g_ref[...]).astype(MXU_DTYPE)
    k_ref[...] = _dot(y, wk_ref[...]).astype(k_ref.dtype)
    v_ref[...] = _dot(y, wv_ref[...]).astype(v_ref.dtype)


def _dsa_kv(misc, kv_norm, w_uk, w_uv, tm=512):
    m = misc.shape[0]
    r = DSA_KV_RANK
    out = jax.ShapeDtypeStruct((m, HEAD_DIM), MXU_DTYPE)
    return pl.pallas_call(
        _dsa_kv_kernel,
        name="dsa_kv",
        grid=(m // tm,),
        in_specs=[pl.BlockSpec((tm, r), lambda i: (i, MISC_CKV * 128 // r)),
                  pl.BlockSpec((1, r), lambda i: (0, 0)),
                  pl.BlockSpec((r, HEAD_DIM), lambda i: (0, 0)),
                  pl.BlockSpec((r, HEAD_DIM), lambda i: (0, 0))],
        out_specs=[pl.BlockSpec((tm, HEAD_DIM), lambda i: (i, 0)), pl.BlockSpec((tm, HEAD_DIM), lambda i: (i, 0))],
        out_shape=[out, out],
        compiler_params=_cparams(("parallel",)),
    )(misc, kv_norm.reshape(1, r), w_uk, w_uv)


PLANE_GROUPS_PER_TILE = KT // 256


def _bit_planes(words):
    a = list(words)
    j, m = 16, 0x0000FFFF
    while j:
        sh = jnp.full(a[0].shape, j, jnp.int32)
        for k in range(32):
            if not k & j:
                t = (a[k] ^ lax.shift_right_logical(a[k + j], sh)) & m
                a[k] = a[k] ^ t
                a[k + j] = a[k + j] ^ (t << j)
        j >>= 1
        m = (m ^ (m << j)) & 0xFFFFFFFF
    return a


def _dsa_kernel(q_ref, k_ref, v_ref, iq_ref, ik_ref, iw_ref, wd_ref, low_ref, o_ref,
                key_ref, plane_ref, alive_ref, add_ref, lg_ref, mx_ref, acc_ref, *, n_keep):
    i = pl.program_id(1)
    d = HEAD_DIM
    tq = DSA_TQ
    rows = DSA_HEADS * tq
    scale = d ** -0.5 * LOG2E
    t0 = i * tq
    n_tiles = (t0 + tq + KT - 1) // KT
    off = t0 - (n_tiles - 1) * KT

    iq = iq_ref[...].astype(MXU_DTYPE)
    qh = [iq[:, h * IDX_DIM:(h + 1) * IDX_DIM] for h in range(IDX_HEADS)]
    wi_t = iw_ref[...].T * (IDX_HEADS ** -0.5) * (IDX_DIM ** -0.5)
    wh = [wi_t[IDX_DIM + h:IDX_DIM + h + 1, :] for h in range(IDX_HEADS)]
    tq_row = t0 + lax.broadcasted_iota(jnp.int32, (KT, tq), 1)
    krow = lax.broadcasted_iota(jnp.int32, (KT, tq), 0)

    def column_sum(a):
        return jnp.sum(a.reshape(KT // 8, 8, tq), axis=0)

    def score_tile(kt, causal):
        k0 = pl.multiple_of(kt * KT, KT)
        ki = ik_ref[pl.ds(k0, KT), 0:IDX_DIM].astype(MXU_DTYPE)
        dots = [_dot_nt(ki, qh[h]) for h in range(IDX_HEADS)]
        sc = wh[0] * jnp.maximum(dots[0], 0.0)
        for h in range(1, IDX_HEADS):
            sc = sc + wh[h] * jnp.maximum(dots[h], 0.0)
        sc = sc + 0.0
        if causal:
            sc = jnp.where(k0 + krow <= tq_row, sc, -jnp.inf)
        bits = lax.bitcast_convert_type(sc, jnp.int32)
        key = bits ^ ((bits >> 31) & 0x7FFFFFFF)
        key_ref[pl.ds(k0, KT), :] = key
        ukey = key ^ INT_MIN
        for g in range(PLANE_GROUPS_PER_TILE):
            words = [ukey[(32 * g + w) * 8:(32 * g + w + 1) * 8] for w in range(32)]
            for x, plane in enumerate(_bit_planes(words)):
                plane_ref[x, PLANE_GROUPS_PER_TILE * kt + g] = plane

    @pl.when((pl.program_id(0) == 0) & (i == 0))
    def _():
        plane_ref[...] = jnp.zeros_like(plane_ref)

    def score_body(kt, carry):
        score_tile(kt, False)
        return carry

    lax.fori_loop(0, n_tiles - 1, score_body, 0)
    score_tile(n_tiles - 1, True)

    ngrp = alive_ref.shape[0]
    for g in range(ngrp):
        alive_ref[g] = jnp.where(g < PLANE_GROUPS_PER_TILE * n_tiles, -1, 0) + jnp.zeros((8, tq), jnp.int32)

    def bit_body(x, carry):
        thr_u, remaining = carry
        ones = [alive_ref[g] & plane_ref[x, g] for g in range(ngrp)]
        c = lax.population_count(ones[0])
        for g in range(1, ngrp):
            c = c + lax.population_count(ones[g])
        c = jnp.sum(c, axis=0, keepdims=True)
        take = c >= remaining
        for g in range(ngrp):
            alive_ref[g] = jnp.where(take, ones[g], alive_ref[g] ^ ones[g])
        bit = jnp.int32(1) << (31 - x)
        return jnp.where(take, thr_u | bit, thr_u), jnp.where(take, remaining, remaining - c)

    thr_u, need = lax.fori_loop(0, 32, bit_body,
                                (jnp.zeros((1, tq), jnp.int32), jnp.full((1, tq), n_keep, jnp.int32)))
    thr = thr_u ^ INT_MIN
    n_equal = lax.population_count(alive_ref[0])
    for g in range(1, ngrp):
        n_equal = n_equal + lax.population_count(alive_ref[g])
    n_equal = jnp.sum(n_equal, axis=0, keepdims=True)
    need = need.astype(F32)
    surplus = jnp.max(n_equal.astype(F32) - need)

    @pl.when(surplus <= 0)
    def _():
        def mask_body(kt, carry):
            k0 = pl.multiple_of(kt * KT, KT)
            add_ref[:, pl.ds(k0, KT)] = jnp.where(key_ref[pl.ds(k0, KT), :] >= thr, 0.0, MASKED).T
            return carry

        lax.fori_loop(0, n_tiles, mask_body, 0)

    @pl.when(surplus > 0)
    def _():
        def mask_body(kt, seen):
            k0 = pl.multiple_of(kt * KT, KT)
            kk = key_ref[pl.ds(k0, KT), :]
            eq = jnp.where(kk == thr, 1.0, 0.0)
            before = seen + _dot(low_ref[...], eq.astype(MXU_DTYPE))
            tie = jnp.where(before < need, 0.0, MASKED)
            add_t = jnp.where(kk > thr, 0.0, jnp.where(kk == thr, tie, MASKED))
            add_ref[:, pl.ds(k0, KT)] = add_t.T
            return seen + jnp.sum(column_sum(eq), axis=0, keepdims=True)

        lax.fori_loop(0, n_tiles, mask_body, jnp.zeros((1, tq), F32))

    q4 = q_ref[0].reshape(rows, d)
    mx_ref[...] = jnp.full(mx_ref.shape, MASKED, F32)

    def att_tile(kt, tmpl_start, width=KT):
        k0 = pl.multiple_of(kt * KT, KT)
        s = _dot_nt(q4, k_ref[0, pl.ds(k0, width), :]) * scale
        addm = add_ref[:, pl.ds(k0, width)]
        s = s + jnp.concatenate([addm] * DSA_HEADS, axis=0)
        if tmpl_start is not None:
            ts = pl.multiple_of(tmpl_start, 128)
            s = s + wd_ref[:, :, pl.ds(ts, width)].reshape(rows, width)
        lg_ref[:, pl.ds(k0, width)] = s
        mx_ref[...] = jnp.maximum(mx_ref[...], _lane_max(s))

    n_far = jnp.maximum(n_tiles - 2, 0)

    def far_body(j, carry):
        att_tile(2 * j, None, 2 * KT)
        return carry

    lax.fori_loop(0, n_far // 2, far_body, 0)

    @pl.when(n_far % 2 == 1)
    def _():
        att_tile(n_far - 1, None)

    @pl.when(n_tiles >= 2)
    def _():
        att_tile(n_tiles - 2, TMPL_C0 - off - KT)

    att_tile(n_tiles - 1, TMPL_C0 - off)
    o = _softmax_pv(lg_ref, mx_ref, acc_ref, v_ref, (0,), n_tiles)
    for r in range(DSA_HEADS):
        o_ref[:, r * d:(r + 1) * d] = o[r * tq:(r + 1) * tq].astype(o_ref.dtype)


def _dsa(hm, k, v, misc, wdsa, low):
    b, _, seq, d = hm.shape
    tq = DSA_TQ
    nq = seq // tq
    rows = DSA_HEADS * tq
    k = k.reshape(b, seq, d)
    v = v.reshape(b, seq, d)
    n_keep = min(DSA_TOPK, seq // 4)
    ngrp = PLANE_GROUPS_PER_TILE * (seq // KT)
    return pl.pallas_call(
        functools.partial(_dsa_kernel, n_keep=n_keep),
        name="dsa",
        grid=(b, nq),
        in_specs=[pl.BlockSpec((1, DSA_HEADS, tq, d), lambda bi, i: (bi, HM_DSA_Q // DSA_HEADS, i, 0)),
                  pl.BlockSpec((1, seq, d), lambda bi, i: (bi, 0, 0)),
                  pl.BlockSpec((1, seq, d), lambda bi, i: (bi, 0, 0)),
                  pl.BlockSpec((tq, IDX_HEADS * IDX_DIM), lambda bi, i: (bi * nq + i, MISC_IDXQ)),
                  pl.BlockSpec((seq, 128), lambda bi, i: (bi, MISC_IDXK)),
                  pl.BlockSpec((tq, 128), lambda bi, i: (bi * nq + i, MISC_IDXK)),
                  pl.BlockSpec((DSA_HEADS, tq, TMPL_W), lambda bi, i: (0, 0, 0), pipeline_mode=pl.Buffered(1)),
                  pl.BlockSpec(low.shape, lambda bi, i: (0, 0), pipeline_mode=pl.Buffered(1))],
        out_specs=pl.BlockSpec((tq, DSA_HEADS * d), lambda bi, i: (bi * nq + i, 0)),
        out_shape=jax.ShapeDtypeStruct((b * seq, DSA_HEADS * d), MXU_DTYPE),
        scratch_shapes=[pltpu.VMEM((seq, tq), jnp.int32),
                        pltpu.VMEM((32, ngrp, 8, tq), jnp.int32),
                        pltpu.VMEM((ngrp, 8, tq), jnp.int32),
                        pltpu.VMEM((tq, seq), F32),
                        pltpu.VMEM((rows, seq), F32),
                        pltpu.VMEM((rows, 128), F32),
                        pltpu.VMEM((rows, d), F32)],
        compiler_params=_cparams(("arbitrary", "arbitrary")),
    )(hm, k, v, misc, misc, misc, wdsa, low)


def _t5_bucket(dist):
    n = jnp.maximum(dist, 0)
    max_exact = REL_BUCKETS // 2
    nf = jnp.maximum(n, 1).astype(F32)
    large = max_exact + (jnp.log(nf / max_exact) / math.log(REL_MAX_DIST / max_exact)
                         * (REL_BUCKETS - max_exact)).astype(jnp.int32)
    large = jnp.minimum(large, REL_BUCKETS - 1)
    return jnp.where(n < max_exact, n, large)


def _bias_templates(rel_tab, ncmp):
    far = REL_MAX_DIST
    by_dist = rel_tab[_t5_bucket(jnp.arange(far + 1))] - rel_tab[REL_BUCKETS - 1][None, :]
    by_dist = by_dist.T * LOG2E

    def build(dist, valid, fill=MASKED):
        t = by_dist[:, np.clip(dist, 0, far)]
        return jnp.where(valid[None], t, fill).astype(F32)

    def toeplitz(u, nrows, width):
        nh, l = u.shape
        return jnp.tile(u, (1, nrows))[:, :nrows * (l - 1)].reshape(nh, nrows, l - 1)[:, :, :width]

    def diagonals(nrows, width):
        l = width + nrows
        k = np.arange(l)
        return np.where(k < width, k, k - l)

    def causal(heads, nrows):
        dist = TMPL_C0 - diagonals(nrows, TMPL_W)
        return toeplitz(build(dist, dist >= 0)[heads], nrows, TMPL_W)

    nsa = slice(0, NSA_HEADS)
    span = NSA_WINDOW + TQ
    dist = NSA_WINDOW - diagonals(TQ, span)
    window = toeplitz(build(dist, (dist >= 0) & (dist < NSA_WINDOW))[nsa], TQ, span)
    cc = np.arange(CMP_BAND)[None, :] - CMP_BAND // 2
    dist = np.arange(TQ)[:, None] - CMP_STRIDE * cc - (CMP_BLOCK - 1)
    band = build(dist, dist >= 0, 0.0)[nsa]
    width = 2 * ncmp - 128
    step = TQ // CMP_STRIDE
    cmp = []
    for v in range(128 // step):
        left = ncmp - 128 + step * v
        canvas = jnp.pad(band, ((0, 0), (0, 0), (left, width - left)))
        cmp.append(canvas[:, :, CMP_BAND // 2:CMP_BAND // 2 + width])
    return causal(nsa, TQ), causal(slice(NSA_HEADS, None), DSA_TQ), window, jnp.stack(cmp)


def _pack_w_in(w_in):
    d3 = 3 * w_in.shape[1]
    kv = NSA_GROUPS * HEAD_DIM
    o_q = d3
    o_kc = o_q + NSA_HEADS * HEAD_DIM
    o_vc, o_ks, o_vs, o_kw, o_vw = (o_kc + j * kv for j in range(1, 6))
    o_g = o_vw + kv
    o_sbq = o_g + 3 * NSA_HEADS
    o_sbk = o_sbq + SB_HEADS * HEAD_DIM
    o_sbv = o_sbk + SB_HEADS * HEAD_DIM
    o_dq = o_sbv + SB_HEADS * HEAD_DIM
    o_ckv = o_dq + DSA_HEADS * HEAD_DIM
    o_iq = o_ckv + DSA_KV_RANK
    o_ik = o_iq + IDX_HEADS * IDX_DIM
    o_iw = o_ik + IDX_DIM
    w = w_in.astype(MXU_DTYPE)
    c = lambda a, n: w[:, :, a:a + n]
    zeros = lambda n: jnp.zeros(w.shape[:2] + (n,), w.dtype)
    tn = PROJ_TN
    assert all(o % tn == 0 for o in (d3, o_q, o_kc, o_ks, o_kw)) and 2 * kv == tn and NSA_HEADS * HEAD_DIM == 2 * tn
    blocks_a = (o_q // tn, o_q // tn + 1, o_ks // tn, o_kw // tn)
    w_b = c(o_sbq, (3 * SB_HEADS + DSA_HEADS) * HEAD_DIM)
    gw = 3 * NSA_REP
    w_misc = jnp.concatenate([c(o_iq, IDX_HEADS * IDX_DIM), c(o_ckv, DSA_KV_RANK),
                              c(o_ik, IDX_DIM), c(o_iw, IDX_HEADS), zeros(128 - IDX_DIM - IDX_HEADS),
                              c(o_g, gw), zeros(128 - gw), c(o_g + gw, gw), zeros(128 - gw)], axis=2)
    return w, blocks_a, o_kc // tn, w_b, w_misc


def kernel(x, p, w_in, norm_mix, norm_ffn, norm_ple, norm_final, w_proj_a, w_proj_b, w_proj_c, w_out,
           cmp_k_w1, cmp_k_w2, cmp_k_pe, cmp_v_w1, cmp_v_w2, cmp_v_pe, dsa_kv_norm, dsa_w_uk, dsa_w_uv,
           rel_bias_table, ffn_w_gate, ffn_w_up, ffn_w_down, ffn_conv_w, ffn_conv_b, ple_w_gate, ple_w_proj):
    batch, seq, d_model = x.shape
    depth = w_in.shape[0]
    m = batch * seq
    assert seq % KT == 0 and seq % SB_T == 0 and seq // SLC_BLOCK <= MAX_SLC_BLOCKS
    ncmp = seq // CMP_STRIDE
    bf = lambda w: w.astype(MXU_DTYPE)

    w_all, blocks_a, block_cmp, w_hm_b, w_misc = _pack_w_in(w_in)
    a0, a1, a2, a3 = blocks_a
    nsa_block = lambda j: jnp.where(j == 0, a0, jnp.where(j == 1, a1, jnp.where(j == 2, a2, a3)))
    w_a, w_b, w_c, w_o = bf(w_proj_a), bf(w_proj_b), bf(w_proj_c), bf(w_out)
    cmp_w1 = bf(jnp.stack([cmp_k_w1, cmp_v_w1], axis=1))
    cmp_w2 = bf(jnp.stack([cmp_k_w2, cmp_v_w2], axis=1))
    cmp_pe = jnp.stack([cmp_k_pe, cmp_v_pe], axis=1).reshape(depth, 2, 2, CMP_STRIDE * HEAD_DIM)
    w_uk, w_uv = bf(dsa_w_uk), bf(dsa_w_uv)
    f_gate, f_up, f_down = bf(ffn_w_gate), bf(ffn_w_up), bf(ffn_w_down)
    pl_gate, pl_proj = bf(ple_w_gate), bf(ple_w_proj)

    wslc, wdsa, wwin, tcmp = _bias_templates(rel_bias_table, ncmp)
    cc = np.arange(ncmp)[None, :]
    jj = np.arange(128)[:, None]
    per = SLC_BLOCK // CMP_STRIDE
    ovl = ((cc >= per * jj - (CMP_BLOCK // CMP_STRIDE - 1)) & (cc <= per * jj + per - 1)
           & (cc < ncmp - 1) & (jj < seq // SLC_BLOCK))
    ovl = jnp.asarray(ovl, MXU_DTYPE)
    low = jnp.asarray(np.arange(KT)[:, None] > np.arange(KT)[None, :], MXU_DTYPE)
    upper = jnp.asarray(np.arange(SB_T)[:, None] > np.arange(SB_T)[None, :], MXU_DTYPE)

    x = x.reshape(m, d_model)
    p = p.reshape(depth, m, p.shape[-1])
    h = _rmsnorm(x, norm_mix[0], MXU_DTYPE)
    for i in range(depth):
        hm_a = _matmul_heads(h, w_all, i, nsa_block, len(blocks_a), batch, MXU_DTYPE, "in_proj_nsa")
        hm_b = _matmul_heads(h, w_hm_b, i, lambda j: j, w_hm_b.shape[2] // PROJ_TN, batch, MXU_DTYPE, "in_proj_sb_dsa")
        cmp_in = _matmul_heads(h, w_all, i, lambda j: block_cmp, 1, batch, F32, "in_proj_cmp")
        misc = _matmul(h, w_misc[i], F32, 1024, MISC_COLS, "in_proj_misc")
        cmp_kv = _compress(cmp_in, cmp_w1[i], cmp_w2[i], cmp_pe[i])
        o_a = _nsa(hm_a, cmp_kv, misc, wslc, wwin, tcmp, ovl)
        o_b = _stick_breaking(hm_b, upper)
        dk, dv = _dsa_kv(misc, dsa_kv_norm[i], w_uk[i], w_uv[i])
        o_c = _dsa(hm_b, dk, dv, misc, wdsa, low)
        y = _merge(h, o_a, o_b, o_c, w_all, i, w_a[i], w_b[i], w_c[i])
        x, h = _matmul_residual(y, w_o[i], x, norm_ffn[i])
        x, h = _conv_ffn(h, x, f_gate[i], f_up[i], f_down[i], ffn_conv_w[i], ffn_conv_b[i], norm_ple[i], seq)
        last = i == depth - 1
        x, h = _ple(h, p[i], x, pl_gate[i], pl_proj[i], norm_final if last else norm_mix[i + 1],
                    F32 if last else MXU_DTYPE)
    return h.reshape(batch, seq, d_model)
```

```python
import functools
import math

import numpy as np
import jax
import jax.numpy as jnp
from jax import lax
from jax.experimental import pallas as pl
from jax.experimental.pallas import tpu as pltpu

F32 = jnp.float32
MXU_DTYPE = jnp.bfloat16

HEAD_DIM = 128
NSA_HEADS = 8
NSA_GROUPS = 2
NSA_REP = NSA_HEADS // NSA_GROUPS
CMP_BLOCK = 32
CMP_STRIDE = 16
SLC_BLOCK = 64
MAX_SLC_BLOCKS = 64
SLC_TOPN = 16
NSA_WINDOW = 512
SB_HEADS = 4
DSA_HEADS = 4
DSA_KV_RANK = 256
IDX_HEADS = 8
IDX_DIM = 64
DSA_TOPK = 256
REL_BUCKETS = 32
REL_MAX_DIST = 128
CONV_WIDTH = 3
EPS = 1e-6

MASKED = -1e30
UNSELECTED = -1e30
INT_MIN = -2 ** 31
LOG2E = math.log2(math.e)
VMEM_LIMIT = 56 * 1024 * 1024

TQ = 256
DSA_TQ = 256
KT = 512
SB_T = 256
SB_HEAD_GROUPS = ((0, 1), (2, 3))
TMPL_C0 = 896
TMPL_W = TMPL_C0 + KT
PROJ_TN = 512
FFN_TF = 512
CMP_BAND = 32

HM_NSA_Q, HM_K_SLC, HM_V_SLC, HM_K_WIN, HM_V_WIN = 0, 8, 10, 12, 14
HM_SB_Q, HM_SB_K, HM_SB_V, HM_DSA_Q = 0, 4, 8, 12
MISC_IDXQ, MISC_CKV, MISC_IDXK, MISC_GATE, MISC_COLS = 0, 4, 6, 7, 9 * 128


def _cparams(sem):
    return pltpu.CompilerParams(dimension_semantics=sem, vmem_limit_bytes=VMEM_LIMIT)


def _dot(a, b):
    return jnp.dot(a, b, preferred_element_type=F32)


def _dot_nt(a, b):
    return lax.dot_general(a, b, (((1,), (1,)), ((), ())), preferred_element_type=F32)


def _rmsnorm_kernel(x_ref, g_ref, o_ref):
    x = x_ref[...]
    y = x * lax.rsqrt(jnp.mean(x * x, axis=-1, keepdims=True) + EPS)
    o_ref[...] = (y * g_ref[...]).astype(o_ref.dtype)


def _rmsnorm(x, g, out_dtype, tm=512):
    m, d = x.shape
    return pl.pallas_call(
        _rmsnorm_kernel,
        grid=(m // tm,),
        in_specs=[pl.BlockSpec((tm, d), lambda i: (i, 0)), pl.BlockSpec((1, d), lambda i: (0, 0))],
        out_specs=pl.BlockSpec((tm, d), lambda i: (i, 0)),
        out_shape=jax.ShapeDtypeStruct((m, d), out_dtype),
        name="rmsnorm",
        compiler_params=_cparams(("parallel",)),
    )(x, g.reshape(1, d))


def _mm_kernel(a_ref, w_ref, o_ref):
    o_ref[...] = _dot(a_ref[...], w_ref[...]).astype(o_ref.dtype)


def _matmul(a, w, out_dtype, tm, tn, name):
    m, k = a.shape
    n = w.shape[1]
    return pl.pallas_call(
        _mm_kernel,
        name=name,
        grid=(m // tm, n // tn),
        in_specs=[pl.BlockSpec((tm, k), lambda i, j: (i, 0)), pl.BlockSpec((k, tn), lambda i, j: (0, j))],
        out_specs=pl.BlockSpec((tm, tn), lambda i, j: (i, j)),
        out_shape=jax.ShapeDtypeStruct((m, n), out_dtype),
        compiler_params=_cparams(("parallel", "parallel")),
    )(a, w)


def _mm_heads_kernel(a_ref, w_ref, o_ref, *, hb):
    r = _dot(a_ref[...], w_ref[0])
    for j in range(hb):
        o_ref[0, j] = r[:, j * HEAD_DIM:(j + 1) * HEAD_DIM].astype(o_ref.dtype)


def _matmul_heads(a, w, layer, col_block, nb, batch, out_dtype, name, tm=1024):
    m, k = a.shape
    s = m // batch
    tn = PROJ_TN
    hb = tn // HEAD_DIM
    spb = s // tm
    return pl.pallas_call(
        functools.partial(_mm_heads_kernel, hb=hb),
        name=name,
        grid=(batch, spb, nb),
        in_specs=[pl.BlockSpec((tm, k), lambda b, i, j: (b * spb + i, 0)),
                  pl.BlockSpec((1, k, tn), lambda b, i, j: (layer, 0, col_block(j)))],
        out_specs=pl.BlockSpec((1, hb, tm, HEAD_DIM), lambda b, i, j: (b, j, i, 0)),
        out_shape=jax.ShapeDtypeStruct((batch, nb * hb, s, HEAD_DIM), out_dtype),
        compiler_params=_cparams(("parallel", "parallel", "parallel")),
    )(a, w)


def _norm_rows(x, g):
    return x * lax.rsqrt(jnp.mean(x * x, axis=-1, keepdims=True) + EPS) * g


def _mm_res_kernel(a_ref, w_ref, x_ref, g_ref, o_ref, h_ref):
    x = x_ref[...] + _dot(a_ref[...], w_ref[...])
    o_ref[...] = x
    h_ref[...] = _norm_rows(x, g_ref[...]).astype(h_ref.dtype)


def _matmul_residual(a, w, x, g, tm=512):
    m, k = a.shape
    n = w.shape[1]
    row = lambda i: (i, 0)
    fixed = lambda i: (0, 0)
    return pl.pallas_call(
        _mm_res_kernel,
        name="out_proj_residual",
        grid=(m // tm,),
        in_specs=[pl.BlockSpec((tm, k), row),
                  pl.BlockSpec((k, n), fixed, pipeline_mode=pl.Buffered(1)),
                  pl.BlockSpec((tm, n), row),
                  pl.BlockSpec((1, n), fixed)],
        out_specs=[pl.BlockSpec((tm, n), row), pl.BlockSpec((tm, n), row)],
        out_shape=[jax.ShapeDtypeStruct((m, n), F32), jax.ShapeDtypeStruct((m, n), MXU_DTYPE)],
        compiler_params=_cparams(("parallel",)),
    )(a, w, x, g.reshape(1, n))


def _merge_kernel(h_ref, oa_ref, ob_ref, oc_ref, ga_ref, gb_ref, gc_ref, wa_ref, wb_ref, wc_ref, y_ref):
    h = h_ref[...]
    y = jax.nn.sigmoid(_dot(h, ga_ref[0])) * _dot(oa_ref[...], wa_ref[...])
    y += jax.nn.sigmoid(_dot(h, gb_ref[0])) * _dot(ob_ref[...], wb_ref[...])
    y += jax.nn.sigmoid(_dot(h, gc_ref[0])) * _dot(oc_ref[...], wc_ref[...])
    y_ref[...] = y.astype(y_ref.dtype)


def _merge(h, o_a, o_b, o_c, w_in, layer, w_a, w_b, w_c, tm=1024, tn=PROJ_TN):
    m, k = h.shape
    d = w_a.shape[1]
    nb = d // tn
    row = lambda i, j: (i, 0)
    col = lambda i, j: (0, j)
    gate = lambda branch: pl.BlockSpec((1, k, tn), lambda i, j: (layer, 0, branch * nb + j))
    return pl.pallas_call(
        _merge_kernel,
        name="branch_merge",
        grid=(m // tm, nb),
        in_specs=[pl.BlockSpec((tm, k), row),
                  pl.BlockSpec((tm, o_a.shape[1]), row), pl.BlockSpec((tm, o_b.shape[1]), row),
                  pl.BlockSpec((tm, o_c.shape[1]), row),
                  gate(0), gate(1), gate(2),
                  pl.BlockSpec((w_a.shape[0], tn), col), pl.BlockSpec((w_b.shape[0], tn), col),
                  pl.BlockSpec((w_c.shape[0], tn), col)],
        out_specs=pl.BlockSpec((tm, tn), lambda i, j: (i, j)),
        out_shape=jax.ShapeDtypeStruct((m, d), MXU_DTYPE),
        compiler_params=_cparams(("parallel", "parallel")),
    )(h, o_a, o_b, o_c, w_in, w_in, w_in, w_a, w_b, w_c)


FFN_HALO = 16


def _ffn_kernel(h_ref, hp_ref, x_ref, wg_ref, wu_ref, wd_ref, cw_ref, cb_ref, g_ref, o_ref, hn_ref, hext_ref,
                *, tiles_per_seq):
    i = pl.program_id(0)
    f = pl.program_id(1)
    tm = h_ref.shape[0]

    @pl.when(f == 0)
    def _():
        first = (i % tiles_per_seq) == 0
        hext_ref[0:FFN_HALO, :] = jnp.where(first, jnp.zeros_like(hp_ref[...]), hp_ref[...])
        hext_ref[FFN_HALO:, :] = h_ref[...]
        o_ref[...] = x_ref[...]

    a = _dot(hext_ref[...], wg_ref[...])
    cw = cw_ref[...]
    c = (cw[0:1] * a[FFN_HALO - 2:FFN_HALO - 2 + tm] + cw[1:2] * a[FFN_HALO - 1:FFN_HALO - 1 + tm]
         + cw[2:3] * a[FFN_HALO:FFN_HALO + tm]) + cb_ref[...]
    u = _dot(h_ref[...], wu_ref[...])
    act = (jax.nn.gelu(c) * u).astype(MXU_DTYPE)
    o_ref[...] += _dot(act, wd_ref[...])

    @pl.when(f == pl.num_programs(1) - 1)
    def _():
        hn_ref[...] = _norm_rows(o_ref[...], g_ref[...]).astype(hn_ref.dtype)


def _conv_ffn(h, x, w_gate, w_up, w_down, conv_w, conv_b, g, seq, tm=512):
    m, d = h.shape
    ff = w_gate.shape[1]
    tf = FFN_TF
    assert seq % tm == 0
    hb = tm // FFN_HALO
    return pl.pallas_call(
        functools.partial(_ffn_kernel, tiles_per_seq=seq // tm),
        name="conv_ffn",
        grid=(m // tm, ff // tf),
        in_specs=[pl.BlockSpec((tm, d), lambda i, f: (i, 0)),
                  pl.BlockSpec((FFN_HALO, d), lambda i, f: (jnp.maximum(i * hb - 1, 0), 0)),
                  pl.BlockSpec((tm, d), lambda i, f: (i, 0)),
                  pl.BlockSpec((d, tf), lambda i, f: (0, f)),
                  pl.BlockSpec((d, tf), lambda i, f: (0, f)),
                  pl.BlockSpec((tf, d), lambda i, f: (f, 0)),
                  pl.BlockSpec((CONV_WIDTH, tf), lambda i, f: (0, f)),
                  pl.BlockSpec((1, tf), lambda i, f: (0, f)),
                  pl.BlockSpec((1, d), lambda i, f: (0, 0))],
        out_specs=[pl.BlockSpec((tm, d), lambda i, f: (i, 0)), pl.BlockSpec((tm, d), lambda i, f: (i, 0))],
        out_shape=[jax.ShapeDtypeStruct((m, d), F32), jax.ShapeDtypeStruct((m, d), MXU_DTYPE)],
        scratch_shapes=[pltpu.VMEM((tm + FFN_HALO, d), MXU_DTYPE)],
        compiler_params=_cparams(("parallel", "arbitrary")),
    )(h, h, x, w_gate, w_up, w_down, conv_w, conv_b.reshape(1, ff), g.reshape(1, d))


def _ple_kernel(h_ref, p_ref, x_ref, wg_ref, wp_ref, g_ref, o_ref, hn_ref):
    gate = jax.nn.sigmoid(_dot(h_ref[...], wg_ref[...]))
    x = x_ref[...] + gate * _dot(p_ref[...].astype(MXU_DTYPE), wp_ref[...])
    o_ref[...] = x
    hn_ref[...] = _norm_rows(x, g_ref[...]).astype(hn_ref.dtype)


def _ple(h, p, x, w_gate, w_proj, g, norm_dtype, tm=512):
    m, d = h.shape
    row = lambda i: (i, 0)
    fixed = lambda i: (0, 0)
    once = pl.Buffered(1)
    return pl.pallas_call(
        _ple_kernel,
        name="ple",
        grid=(m // tm,),
        in_specs=[pl.BlockSpec((tm, d), row),
                  pl.BlockSpec((tm, p.shape[1]), row),
                  pl.BlockSpec((tm, d), row),
                  pl.BlockSpec((d, d), fixed, pipeline_mode=once),
                  pl.BlockSpec((p.shape[1], d), fixed, pipeline_mode=once),
                  pl.BlockSpec((1, d), fixed)],
        out_specs=[pl.BlockSpec((tm, d), row), pl.BlockSpec((tm, d), row)],
        out_shape=[jax.ShapeDtypeStruct((m, d), F32), jax.ShapeDtypeStruct((m, d), norm_dtype)],
        compiler_params=_cparams(("parallel",)),
    )(h, p, x, w_gate, w_proj, g.reshape(1, d))


def _compress_kernel(x_ref, w1_ref, w2_ref, pe_ref, o_ref):
    x = x_ref[0, 0]
    pe = pe_ref[0]
    half = x.shape[1]
    lo = _dot((x + pe[0:1]).astype(MXU_DTYPE), w1_ref[0, :half, :])
    hi = _dot((x + pe[1:2]).astype(MXU_DTYPE), w1_ref[0, half:, :])
    n = x.shape[0]
    hid = lo + pltpu.roll(hi, n - 1, 0)
    o_ref[0, 0, 0] = _dot(jax.nn.gelu(hid).astype(MXU_DTYPE), w2_ref[0]).astype(o_ref.dtype)


def _compress(kv_hm, w1, w2, pe):
    b, _, s, d = kv_hm.shape
    nrow = s // CMP_STRIDE
    x = kv_hm.reshape(b, 2 * NSA_GROUPS, nrow, CMP_STRIDE * d)
    return pl.pallas_call(
        _compress_kernel,
        name="nsa_compress",
        grid=(b, 2, NSA_GROUPS),
        in_specs=[pl.BlockSpec((1, 1, nrow, CMP_STRIDE * d), lambda bi, kv, g: (bi, kv * NSA_GROUPS + g, 0, 0)),
                  pl.BlockSpec((1,) + w1.shape[1:], lambda bi, kv, g: (kv, 0, 0)),
                  pl.BlockSpec((1,) + w2.shape[1:], lambda bi, kv, g: (kv, 0, 0)),
                  pl.BlockSpec((1, 2, CMP_STRIDE * d), lambda bi, kv, g: (kv, 0, 0))],
        out_specs=pl.BlockSpec((1, 1, 1, nrow, d), lambda bi, kv, g: (bi, kv, g, 0, 0)),
        out_shape=jax.ShapeDtypeStruct((b, 2, NSA_GROUPS, nrow, d), MXU_DTYPE),
        compiler_params=_cparams(("parallel", "parallel", "parallel")),
    )(x, w1, w2, pe)


def _softmax_pv(lg_ref, mx_ref, acc_ref, v_ref, v_index, n_tiles):
    rows = lg_ref.shape[0]
    m = jnp.max(mx_ref[...], axis=1, keepdims=True)
    mx_ref[...] = jnp.zeros_like(mx_ref)
    acc_ref[...] = jnp.zeros_like(acc_ref)

    def chunk(k0, width):
        p = jnp.exp2(lg_ref[:, pl.ds(k0, width)] - m)
        part = p[:, 0:128]
        for c in range(1, width // 128):
            part = part + p[:, c * 128:(c + 1) * 128]
        mx_ref[...] += part
        acc_ref[...] += _dot(p.astype(MXU_DTYPE), v_ref[v_index + (pl.ds(k0, width), slice(None))])

    def body(j, carry):
        chunk(pl.multiple_of(j * (2 * KT), 2 * KT), 2 * KT)
        return carry

    lax.fori_loop(0, n_tiles // 2, body, 0)

    @pl.when(n_tiles % 2 == 1)
    def _():
        chunk(pl.multiple_of((n_tiles - 1) * KT, KT), KT)

    l = jnp.sum(mx_ref[...], axis=1, keepdims=True)
    return acc_ref[...] / l


def _lane_max(s):
    part = s[:, 0:128]
    for c in range(1, s.shape[1] // 128):
        part = jnp.maximum(part, s[:, c * 128:(c + 1) * 128])
    return part


def _nsa_kernel(q_ref, kc_ref, vc_ref, kslc_ref, vslc_ref, kwin_ref, vwin_ref, gate_ref,
                wslc_ref, wwin_ref, tcmp_ref, ovl_ref, o_ref,
                kslc_aug, kwin_aug, vwin_pad, qaug, lg_ref, mx_ref, acc_ref, score_ref):
    i = pl.program_id(2)
    seq = kslc_ref.shape[2]
    d = HEAD_DIM
    rows = NSA_REP * TQ
    scale = d ** -0.5 * LOG2E
    t0 = i * TQ

    @pl.when(i == 0)
    def _():
        kslc_aug[:, 0:d] = kslc_ref[0, 0]
        srow = lax.broadcasted_iota(jnp.int32, (seq, d), 0)
        lane = lax.broadcasted_iota(jnp.int32, (seq, d), 1)
        kslc_aug[:, d:2 * d] = jnp.where((srow >> 6) == lane, 1.0, 0.0).astype(kslc_aug.dtype)
        kwin_aug[0:NSA_WINDOW, 0:d] = jnp.zeros((NSA_WINDOW, d), kwin_aug.dtype)
        kwin_aug[NSA_WINDOW:, 0:d] = kwin_ref[0, 0]
        prow = lax.broadcasted_iota(jnp.int32, (seq + NSA_WINDOW, d), 0)
        plane = lax.broadcasted_iota(jnp.int32, (seq + NSA_WINDOW, d), 1)
        flag = jnp.where(prow < NSA_WINDOW, jnp.where(plane == MAX_SLC_BLOCKS, UNSELECTED, 0.0), 0.0)
        kwin_aug[:, d:2 * d] = flag.astype(kwin_aug.dtype)
        vwin_pad[0:NSA_WINDOW, :] = jnp.zeros((NSA_WINDOW, d), vwin_pad.dtype)
        vwin_pad[NSA_WINDOW:, :] = vwin_ref[0, 0]

    q4 = q_ref[0].reshape(rows, d)

    ncmp = kc_ref.shape[3]
    kc = kc_ref[0, 0, 0]
    vc = vc_ref[0, 0, 0]
    trow = t0 + lax.broadcasted_iota(jnp.int32, (TQ, ncmp), 0)
    cend = lax.broadcasted_iota(jnp.int32, (TQ, ncmp), 1) * CMP_STRIDE + (CMP_BLOCK - 1)
    valid_c = cend <= trow
    sc_all = _dot_nt(q4, kc) * scale
    c0 = i * (TQ // CMP_STRIDE)
    bias_start = pl.multiple_of(ncmp - 128 - 128 * (c0 // 128), 128)
    pb = []
    for r in range(NSA_REP):
        bias = tcmp_ref[0, r, :, pl.ds(bias_start, ncmp)]
        l = jnp.where(valid_c, sc_all[r * TQ:(r + 1) * TQ] + bias, MASKED)
        m = jnp.max(l, axis=1, keepdims=True)
        e = jnp.where(valid_c, jnp.exp2(l - m), 0.0)
        p = e / jnp.maximum(jnp.sum(e, axis=1, keepdims=True), 1e-30)
        pb.append(p.astype(MXU_DTYPE))
    o_cmp = [_dot(pb[r], vc) for r in range(NSA_REP)]
    imp_t = _dot_nt(ovl_ref[...], pb[0])
    for r in range(1, NSA_REP):
        imp_t = imp_t + _dot_nt(ovl_ref[...], pb[r])

    span = NSA_WINDOW + TQ
    w0 = pl.multiple_of(t0, TQ)
    one_lane = lax.broadcasted_iota(jnp.int32, (rows, d), 1) == MAX_SLC_BLOCKS
    q_win = jnp.concatenate([q4, jnp.where(one_lane, 1.0, 0.0).astype(q4.dtype)], axis=1)
    s = _dot_nt(q_win, kwin_aug[pl.ds(w0, span), :]) * scale + wwin_ref[...].reshape(rows, span)
    m = jnp.max(s, axis=1, keepdims=True)
    p = jnp.exp2(s - m)
    l = jnp.sum(p, axis=1, keepdims=True)
    o_win = _dot(p.astype(MXU_DTYPE), vwin_pad[pl.ds(w0, span), :]) / l

    nblk = MAX_SLC_BLOCKS
    jt = lax.broadcasted_iota(jnp.int32, (nblk, TQ), 0)
    tt = t0 + lax.broadcasted_iota(jnp.int32, (nblk, TQ), 1)
    cur = tt >> 6
    imp_t = imp_t[0:nblk]
    score = jnp.where(jt == 0, 1e9, jnp.where(jt == cur, 1e9, jnp.where(jt == cur - 1, 1e9, imp_t)))
    score = jnp.where(jt * SLC_BLOCK <= tt, score, -jnp.inf)
    score_ref[...] = score
    sub = lax.broadcasted_iota(jnp.int32, (8, TQ), 0)
    sv = [score[8 * v:8 * v + 8] for v in range(nblk // 8)]
    beaten = [jnp.zeros((8, TQ), F32) for _ in sv]
    for jp in range(nblk):
        other = score_ref[jp:jp + 1, :]
        for v in range(nblk // 8):
            if 8 * v > jp:
                hit = other >= sv[v]
            elif 8 * v + 7 <= jp:
                hit = other > sv[v]
            else:
                tie_loses = jnp.where(sub > jp - 8 * v, 1.0, 0.0)
                beaten[v] = beaten[v] + jnp.where(other == sv[v], tie_loses, 0.0)
                hit = other > sv[v]
            beaten[v] = beaten[v] + jnp.where(hit, 1.0, 0.0)
    aug_t = jnp.concatenate([jnp.where(b < SLC_TOPN, 0.0, UNSELECTED) for b in beaten], axis=0)
    row = lax.broadcasted_iota(jnp.int32, (d - nblk, TQ), 0)
    aug_t = jnp.concatenate([aug_t, jnp.where(row == 0, 1.0, 0.0)], axis=0)
    aug = aug_t.T
    qaug[:, 0:d] = q4
    for r in range(NSA_REP):
        qaug[r * TQ:(r + 1) * TQ, d:2 * d] = aug.astype(qaug.dtype)
    qa = qaug[...]

    n_tiles = (t0 + TQ + KT - 1) // KT
    off = t0 - (n_tiles - 1) * KT
    mx_ref[...] = jnp.full(mx_ref.shape, MASKED, F32)

    def slc_tile(kt, tmpl_start, width=KT):
        k0 = pl.multiple_of(kt * KT, KT)
        s = _dot_nt(qa, kslc_aug[pl.ds(k0, width), :]) * scale
        if tmpl_start is not None:
            ts = pl.multiple_of(tmpl_start, 128)
            s = s + wslc_ref[:, :, pl.ds(ts, width)].reshape(rows, width)
        lg_ref[:, pl.ds(k0, width)] = s
        mx_ref[...] = jnp.maximum(mx_ref[...], _lane_max(s))

    n_far = jnp.maximum(n_tiles - 2, 0)

    def far_body(j, carry):
        slc_tile(2 * j, None, 2 * KT)
        return carry

    lax.fori_loop(0, n_far // 2, far_body, 0)

    @pl.when(n_far % 2 == 1)
    def _():
        slc_tile(n_far - 1, None)

    @pl.when(n_tiles >= 2)
    def _():
        slc_tile(n_tiles - 2, TMPL_C0 - off - KT)

    slc_tile(n_tiles - 1, TMPL_C0 - off)
    o_slc = _softmax_pv(lg_ref, mx_ref, acc_ref, vslc_ref, (0, 0), n_tiles)

    g = jax.nn.sigmoid(gate_ref[...])
    for r in range(NSA_REP):
        rs = slice(r * TQ, (r + 1) * TQ)
        o = (g[:, 3 * r:3 * r + 1] * o_cmp[r] + g[:, 3 * r + 1:3 * r + 2] * o_slc[rs]
             + g[:, 3 * r + 2:3 * r + 3] * o_win[rs])
        o_ref[:, r * d:(r + 1) * d] = o.astype(o_ref.dtype)


def _nsa(hm, cmp_kv, misc, wslc, wwin, tcmp, ovl):
    b, _, seq, d = hm.shape
    nq = seq // TQ
    rows = NSA_REP * TQ
    ncmp = cmp_kv.shape[3]
    kv_spec = lambda head: pl.BlockSpec((1, 1, seq, d), lambda bi, g, i: (bi, head + g, 0, 0))
    once = pl.Buffered(1)
    return pl.pallas_call(
        _nsa_kernel,
        name="nsa",
        grid=(b, NSA_GROUPS, nq),
        in_specs=[pl.BlockSpec((1, NSA_REP, TQ, d), lambda bi, g, i: (bi, g, i, 0)),
                  pl.BlockSpec((1, 1, 1, ncmp, d), lambda bi, g, i: (bi, 0, g, 0, 0)),
                  pl.BlockSpec((1, 1, 1, ncmp, d), lambda bi, g, i: (bi, 1, g, 0, 0)),
                  kv_spec(HM_K_SLC), kv_spec(HM_V_SLC), kv_spec(HM_K_WIN), kv_spec(HM_V_WIN),
                  pl.BlockSpec((TQ, 128), lambda bi, g, i: (bi * nq + i, MISC_GATE + g)),
                  pl.BlockSpec((NSA_REP, TQ, TMPL_W), lambda bi, g, i: (g, 0, 0), pipeline_mode=once),
                  pl.BlockSpec((NSA_REP, TQ, NSA_WINDOW + TQ), lambda bi, g, i: (g, 0, 0), pipeline_mode=once),
                  pl.BlockSpec((1, NSA_REP) + tcmp.shape[2:], lambda bi, g, i: (i % tcmp.shape[0], g, 0, 0)),
                  pl.BlockSpec(ovl.shape, lambda bi, g, i: (0, 0), pipeline_mode=once)],
        out_specs=pl.BlockSpec((TQ, NSA_REP * d), lambda bi, g, i: (bi * nq + i, g)),
        out_shape=jax.ShapeDtypeStruct((b * seq, NSA_HEADS * d), MXU_DTYPE),
        scratch_shapes=[pltpu.VMEM((seq, 2 * d), MXU_DTYPE),
                        pltpu.VMEM((seq + NSA_WINDOW, 2 * d), MXU_DTYPE),
                        pltpu.VMEM((seq + NSA_WINDOW, d), MXU_DTYPE),
                        pltpu.VMEM((rows, 2 * d), MXU_DTYPE),
                        pltpu.VMEM((rows, seq), F32),
                        pltpu.VMEM((rows, 128), F32),
                        pltpu.VMEM((rows, d), F32),
                        pltpu.VMEM((MAX_SLC_BLOCKS, TQ), F32)],
        compiler_params=_cparams(("parallel", "parallel", "arbitrary")),
    )(hm, cmp_kv, cmp_kv, hm, hm, hm, hm, misc, wslc, wwin, tcmp, ovl)


def _sb_kernel(q_ref, k_ref, v_ref, upper_ref, o_ref, acc_ref, carry_ref):
    i = pl.program_id(1)
    t = SB_T
    d = HEAD_DIM
    scale = d ** -0.5 * LOG2E
    upper = upper_ref[...]
    acc_ref[...] = jnp.zeros_like(acc_ref)
    carry_ref[...] = jnp.zeros_like(carry_ref)

    heads = range(SB_HEADS)

    def tiles(kts, mask, heads=heads):
        k0 = [pl.multiple_of(kt * t, t) for kt in kts]
        chains = [(j, h) for j in range(len(kts)) for h in heads]
        z = {c: _dot_nt(q_ref[0, c[1]], k_ref[0, c[1], pl.ds(k0[c[0]], t), :]) * scale for c in chains}
        neg_abs = {c: lax.bitcast_convert_type(lax.bitcast_convert_type(z[c], jnp.int32) | INT_MIN, F32)
                   for c in chains}
        ls = {c: jnp.minimum(z[c], 0.0) - jnp.log2(1.0 + jnp.exp2(neg_abs[c])) for c in chains}
        lk = {c: ls[c] - z[c] for c in chains}
        if mask is not None:
            lk = {c: jnp.where(mask, lk[c], 0.0) for c in chains}
        carry = {}
        for h in heads:
            run = carry_ref[h]
            for j in range(len(kts)):
                carry[(j, h)] = run
                run = run + jnp.sum(lk[(j, h)], axis=1, keepdims=True)
            carry_ref[h] = run
        hi = {c: lk[c].astype(MXU_DTYPE) for c in chains}
        lo = {c: (lk[c] - hi[c].astype(F32)).astype(MXU_DTYPE) for c in chains}
        after = {c: carry[c] + (_dot(hi[c], upper) + _dot(lo[c], upper)) for c in chains}
        w = {c: jnp.exp2(ls[c] + after[c]) for c in chains}
        if mask is not None:
            w = {c: jnp.where(mask, w[c], 0.0) for c in chains}
        for j, h in chains:
            acc_ref[h] += _dot(w[(j, h)].astype(MXU_DTYPE), v_ref[0, h, pl.ds(k0[j], t), :])

    strict = lax.broadcasted_iota(jnp.int32, (t, t), 1) < lax.broadcasted_iota(jnp.int32, (t, t), 0)
    tiles([i], strict)

    def body(n, carry):
        for hs in SB_HEAD_GROUPS:
            tiles([i - 1 - 2 * n, i - 2 - 2 * n], None, hs)
        return carry

    lax.fori_loop(0, i // 2, body, 0)

    @pl.when(i % 2 == 1)
    def _():
        tiles([0], None)

    for h in range(SB_HEADS):
        o_ref[:, h * d:(h + 1) * d] = acc_ref[h].astype(o_ref.dtype)


def _stick_breaking(hm, upper):
    b, _, seq, d = hm.shape
    nq = seq // SB_T
    return pl.pallas_call(
        _sb_kernel,
        grid=(b, nq),
        in_specs=[pl.BlockSpec((1, SB_HEADS, SB_T, d), lambda bi, i: (bi, HM_SB_Q // SB_HEADS, i, 0)),
                  pl.BlockSpec((1, SB_HEADS, seq, d), lambda bi, i: (bi, HM_SB_K // SB_HEADS, 0, 0)),
                  pl.BlockSpec((1, SB_HEADS, seq, d), lambda bi, i: (bi, HM_SB_V // SB_HEADS, 0, 0)),
                  pl.BlockSpec(upper.shape, lambda bi, i: (0, 0))],
        out_specs=pl.BlockSpec((SB_T, SB_HEADS * d), lambda bi, i: (bi * nq + i, 0)),
        out_shape=jax.ShapeDtypeStruct((b * seq, SB_HEADS * d), MXU_DTYPE),
        scratch_shapes=[pltpu.VMEM((SB_HEADS, SB_T, d), F32), pltpu.VMEM((SB_HEADS, SB_T, 1), F32)],
        name="stick_breaking",
        compiler_params=_cparams(("parallel", "parallel")),
    )(hm, hm, hm, upper)


def _dsa_kv_kernel(c_ref, g_ref, wk_ref, wv_ref, k_ref, v_ref):
    c = c_ref[...]
    y = c * lax.rsqrt(jnp.mean(c * c, axis=-1, keepdims=True) + EPS)
    y = (y * g_ref[...]).astype(MXU_DTYPE)
    k_ref[...] = _dot(y, wk_ref[...]).astype(k_ref.dtype)
    v_ref[...] = _dot(y, wv_ref[...]).astype(v_ref.dtype)


def _dsa_kv(misc, kv_norm, w_uk, w_uv, tm=512):
    m = misc.shape[0]
    r = DSA_KV_RANK
    out = jax.ShapeDtypeStruct((m, HEAD_DIM), MXU_DTYPE)
    return pl.pallas_call(
        _dsa_kv_kernel,
        name="dsa_kv",
        grid=(m // tm,),
        in_specs=[pl.BlockSpec((tm, r), lambda i: (i, MISC_CKV * 128 // r)),
                  pl.BlockSpec((1, r), lambda i: (0, 0)),
                  pl.BlockSpec((r, HEAD_DIM), lambda i: (0, 0)),
                  pl.BlockSpec((r, HEAD_DIM), lambda i: (0, 0))],
        out_specs=[pl.BlockSpec((tm, HEAD_DIM), lambda i: (i, 0)), pl.BlockSpec((tm, HEAD_DIM), lambda i: (i, 0))],
        out_shape=[out, out],
        compiler_params=_cparams(("parallel",)),
    )(misc, kv_norm.reshape(1, r), w_uk, w_uv)


PLANE_GROUPS_PER_TILE = KT // 256


def _bit_planes(words):
    a = list(words)
    j, m = 16, 0x0000FFFF
    while j:
        sh = jnp.full(a[0].shape, j, jnp.int32)
        for k in range(32):
            if not k & j:
                t = (a[k] ^ lax.shift_right_logical(a[k + j], sh)) & m
                a[k] = a[k] ^ t
                a[k + j] = a[k + j] ^ (t << j)
        j >>= 1
        m = (m ^ (m << j)) & 0xFFFFFFFF
    return a


def _dsa_kernel(q_ref, k_ref, v_ref, iq_ref, ik_ref, iw_ref, wd_ref, low_ref, o_ref,
                key_ref, plane_ref, alive_ref, add_ref, lg_ref, mx_ref, acc_ref, *, n_keep):
    i = pl.program_id(1)
    d = HEAD_DIM
    tq = DSA_TQ
    rows = DSA_HEADS * tq
    scale = d ** -0.5 * LOG2E
    t0 = i * tq
    n_tiles = (t0 + tq + KT - 1) // KT
    off = t0 - (n_tiles - 1) * KT

    iq = iq_ref[...].astype(MXU_DTYPE)
    qh = [iq[:, h * IDX_DIM:(h + 1) * IDX_DIM] for h in range(IDX_HEADS)]
    wi_t = iw_ref[...].T * (IDX_HEADS ** -0.5) * (IDX_DIM ** -0.5)
    wh = [wi_t[IDX_DIM + h:IDX_DIM + h + 1, :] for h in range(IDX_HEADS)]
    tq_row = t0 + lax.broadcasted_iota(jnp.int32, (KT, tq), 1)
    krow = lax.broadcasted_iota(jnp.int32, (KT, tq), 0)

    def column_sum(a):
        return jnp.sum(a.reshape(KT // 8, 8, tq), axis=0)

    def score_tile(kt, causal):
        k0 = pl.multiple_of(kt * KT, KT)
        ki = ik_ref[pl.ds(k0, KT), 0:IDX_DIM].astype(MXU_DTYPE)
        dots = [_dot_nt(ki, qh[h]) for h in range(IDX_HEADS)]
        sc = wh[0] * jnp.maximum(dots[0], 0.0)
        for h in range(1, IDX_HEADS):
            sc = sc + wh[h] * jnp.maximum(dots[h], 0.0)
        sc = jnp.where(sc == 0.0, 0.0, sc)
        if causal:
            sc = jnp.where(k0 + krow <= tq_row, sc, -jnp.inf)
        bits = lax.bitcast_convert_type(sc, jnp.int32)
        key = bits ^ ((bits >> 31) & 0x7FFFFFFF)
        key_ref[pl.ds(k0, KT), :] = key
        ukey = key ^ INT_MIN
        for g in range(PLANE_GROUPS_PER_TILE):
            words = [ukey[(32 * g + w) * 8:(32 * g + w + 1) * 8] for w in range(32)]
            for x, plane in enumerate(_bit_planes(words)):
                plane_ref[x, PLANE_GROUPS_PER_TILE * kt + g] = plane

    @pl.when((pl.program_id(0) == 0) & (i == 0))
    def _():
        plane_ref[...] = jnp.zeros_like(plane_ref)

    def score_body(kt, carry):
        score_tile(kt, False)
        return carry

    lax.fori_loop(0, n_tiles - 1, score_body, 0)
    score_tile(n_tiles - 1, True)

    ngrp = alive_ref.shape[0]
    for g in range(ngrp):
        alive_ref[g] = jnp.where(g < PLANE_GROUPS_PER_TILE * n_tiles, -1, 0) + jnp.zeros((8, tq), jnp.int32)

    def bit_body(x, carry):
        thr_u, remaining = carry
        ones = [alive_ref[g] & plane_ref[x, g] for g in range(ngrp)]
        c = lax.population_count(ones[0])
        for g in range(1, ngrp):
            c = c + lax.population_count(ones[g])
        c = jnp.sum(c, axis=0, keepdims=True)
        take = c >= remaining
        for g in range(ngrp):
            alive_ref[g] = jnp.where(take, ones[g], alive_ref[g] ^ ones[g])
        bit = jnp.int32(1) << (31 - x)
        return jnp.where(take, thr_u | bit, thr_u), jnp.where(take, remaining, remaining - c)

    thr_u, need = lax.fori_loop(0, 32, bit_body,
                                (jnp.zeros((1, tq), jnp.int32), jnp.full((1, tq), n_keep, jnp.int32)))
    thr = thr_u ^ INT_MIN
    n_equal = lax.population_count(alive_ref[0])
    for g in range(1, ngrp):
        n_equal = n_equal + lax.population_count(alive_ref[g])
    n_equal = jnp.sum(n_equal, axis=0, keepdims=True)
    need = need.astype(F32)
    surplus = jnp.max(n_equal.astype(F32) - need)

    @pl.when(surplus <= 0)
    def _():
        def mask_body(kt, carry):
            k0 = pl.multiple_of(kt * KT, KT)
            add_ref[:, pl.ds(k0, KT)] = jnp.where(key_ref[pl.ds(k0, KT), :] >= thr, 0.0, MASKED).T
            return carry

        lax.fori_loop(0, n_tiles, mask_body, 0)

    @pl.when(surplus > 0)
    def _():
        def mask_body(kt, seen):
            k0 = pl.multiple_of(kt * KT, KT)
            kk = key_ref[pl.ds(k0, KT), :]
            eq = jnp.where(kk == thr, 1.0, 0.0)
            before = seen + _dot(low_ref[...], eq.astype(MXU_DTYPE))
            tie = jnp.where(before < need, 0.0, MASKED)
            add_t = jnp.where(kk > thr, 0.0, jnp.where(kk == thr, tie, MASKED))
            add_ref[:, pl.ds(k0, KT)] = add_t.T
            return seen + jnp.sum(column_sum(eq), axis=0, keepdims=True)

        lax.fori_loop(0, n_tiles, mask_body, jnp.zeros((1, tq), F32))

    q4 = q_ref[0].reshape(rows, d)
    mx_ref[...] = jnp.full(mx_ref.shape, MASKED, F32)

    def att_tile(kt, tmpl_start, width=KT):
        k0 = pl.multiple_of(kt * KT, KT)
        s = _dot_nt(q4, k_ref[0, pl.ds(k0, width), :]) * scale
        addm = add_ref[:, pl.ds(k0, width)]
        s = s + jnp.concatenate([addm] * DSA_HEADS, axis=0)
        if tmpl_start is not None:
            ts = pl.multiple_of(tmpl_start, 128)
            s = s + wd_ref[:, :, pl.ds(ts, width)].reshape(rows, width)
        lg_ref[:, pl.ds(k0, width)] = s
        mx_ref[...] = jnp.maximum(mx_ref[...], _lane_max(s))

    n_far = jnp.maximum(n_tiles - 2, 0)

    def far_body(j, carry):
        att_tile(2 * j, None, 2 * KT)
        return carry

    lax.fori_loop(0, n_far // 2, far_body, 0)

    @pl.when(n_far % 2 == 1)
    def _():
        att_tile(n_far - 1, None)

    @pl.when(n_tiles >= 2)
    def _():
        att_tile(n_tiles - 2, TMPL_C0 - off - KT)

    att_tile(n_tiles - 1, TMPL_C0 - off)
    o = _softmax_pv(lg_ref, mx_ref, acc_ref, v_ref, (0,), n_tiles)
    for r in range(DSA_HEADS):
        o_ref[:, r * d:(r + 1) * d] = o[r * tq:(r + 1) * tq].astype(o_ref.dtype)


def _dsa(hm, k, v, misc, wdsa, low):
    b, _, seq, d = hm.shape
    tq = DSA_TQ
    nq = seq // tq
    rows = DSA_HEADS * tq
    k = k.reshape(b, seq, d)
    v = v.reshape(b, seq, d)
    n_keep = min(DSA_TOPK, seq // 4)
    ngrp = PLANE_GROUPS_PER_TILE * (seq // KT)
    return pl.pallas_call(
        functools.partial(_dsa_kernel, n_keep=n_keep),
        name="dsa",
        grid=(b, nq),
        in_specs=[pl.BlockSpec((1, DSA_HEADS, tq, d), lambda bi, i: (bi, HM_DSA_Q // DSA_HEADS, i, 0)),
                  pl.BlockSpec((1, seq, d), lambda bi, i: (bi, 0, 0)),
                  pl.BlockSpec((1, seq, d), lambda bi, i: (bi, 0, 0)),
                  pl.BlockSpec((tq, IDX_HEADS * IDX_DIM), lambda bi, i: (bi * nq + i, MISC_IDXQ)),
                  pl.BlockSpec((seq, 128), lambda bi, i: (bi, MISC_IDXK)),
                  pl.BlockSpec((tq, 128), lambda bi, i: (bi * nq + i, MISC_IDXK)),
                  pl.BlockSpec((DSA_HEADS, tq, TMPL_W), lambda bi, i: (0, 0, 0), pipeline_mode=pl.Buffered(1)),
                  pl.BlockSpec(low.shape, lambda bi, i: (0, 0), pipeline_mode=pl.Buffered(1))],
        out_specs=pl.BlockSpec((tq, DSA_HEADS * d), lambda bi, i: (bi * nq + i, 0)),
        out_shape=jax.ShapeDtypeStruct((b * seq, DSA_HEADS * d), MXU_DTYPE),
        scratch_shapes=[pltpu.VMEM((seq, tq), jnp.int32),
                        pltpu.VMEM((32, ngrp, 8, tq), jnp.int32),
                        pltpu.VMEM((ngrp, 8, tq), jnp.int32),
                        pltpu.VMEM((tq, seq), F32),
                        pltpu.VMEM((rows, seq), F32),
                        pltpu.VMEM((rows, 128), F32),
                        pltpu.VMEM((rows, d), F32)],
        compiler_params=_cparams(("arbitrary", "arbitrary")),
    )(hm, k, v, misc, misc, misc, wdsa, low)


def _t5_bucket(dist):
    n = jnp.maximum(dist, 0)
    max_exact = REL_BUCKETS // 2
    nf = jnp.maximum(n, 1).astype(F32)
    large = max_exact + (jnp.log(nf / max_exact) / math.log(REL_MAX_DIST / max_exact)
                         * (REL_BUCKETS - max_exact)).astype(jnp.int32)
    large = jnp.minimum(large, REL_BUCKETS - 1)
    return jnp.where(n < max_exact, n, large)


def _bias_templates(rel_tab, ncmp):
    far = REL_MAX_DIST
    by_dist = rel_tab[_t5_bucket(jnp.arange(far + 1))] - rel_tab[REL_BUCKETS - 1][None, :]
    by_dist = by_dist.T * LOG2E

    def build(dist, valid, fill=MASKED):
        t = by_dist[:, np.clip(dist, 0, far)]
        return jnp.where(valid[None], t, fill).astype(F32)

    def toeplitz(u, nrows, width):
        nh, l = u.shape
        return jnp.tile(u, (1, nrows))[:, :nrows * (l - 1)].reshape(nh, nrows, l - 1)[:, :, :width]

    def diagonals(nrows, width):
        l = width + nrows
        k = np.arange(l)
        return np.where(k < width, k, k - l)

    def causal(heads, nrows):
        dist = TMPL_C0 - diagonals(nrows, TMPL_W)
        return toeplitz(build(dist, dist >= 0)[heads], nrows, TMPL_W)

    nsa = slice(0, NSA_HEADS)
    span = NSA_WINDOW + TQ
    dist = NSA_WINDOW - diagonals(TQ, span)
    window = toeplitz(build(dist, (dist >= 0) & (dist < NSA_WINDOW))[nsa], TQ, span)
    cc = np.arange(CMP_BAND)[None, :] - CMP_BAND // 2
    dist = np.arange(TQ)[:, None] - CMP_STRIDE * cc - (CMP_BLOCK - 1)
    band = build(dist, dist >= 0, 0.0)[nsa]
    width = 2 * ncmp - 128
    step = TQ // CMP_STRIDE
    cmp = []
    for v in range(128 // step):
        left = ncmp - 128 + step * v
        canvas = jnp.pad(band, ((0, 0), (0, 0), (left, width - left)))
        cmp.append(canvas[:, :, CMP_BAND // 2:CMP_BAND // 2 + width])
    return causal(nsa, TQ), causal(slice(NSA_HEADS, None), DSA_TQ), window, jnp.stack(cmp)


def _pack_w_in(w_in):
    d3 = 3 * w_in.shape[1]
    kv = NSA_GROUPS * HEAD_DIM
    o_q = d3
    o_kc = o_q + NSA_HEADS * HEAD_DIM
    o_vc, o_ks, o_vs, o_kw, o_vw = (o_kc + j * kv for j in range(1, 6))
    o_g = o_vw + kv
    o_sbq = o_g + 3 * NSA_HEADS
    o_sbk = o_sbq + SB_HEADS * HEAD_DIM
    o_sbv = o_sbk + SB_HEADS * HEAD_DIM
    o_dq = o_sbv + SB_HEADS * HEAD_DIM
    o_ckv = o_dq + DSA_HEADS * HEAD_DIM
    o_iq = o_ckv + DSA_KV_RANK
    o_ik = o_iq + IDX_HEADS * IDX_DIM
    o_iw = o_ik + IDX_DIM
    w = w_in.astype(MXU_DTYPE)
    c = lambda a, n: w[:, :, a:a + n]
    zeros = lambda n: jnp.zeros(w.shape[:2] + (n,), w.dtype)
    tn = PROJ_TN
    assert all(o % tn == 0 for o in (d3, o_q, o_kc, o_ks, o_kw)) and 2 * kv == tn and NSA_HEADS * HEAD_DIM == 2 * tn
    blocks_a = (o_q // tn, o_q // tn + 1, o_ks // tn, o_kw // tn)
    w_b = c(o_sbq, (3 * SB_HEADS + DSA_HEADS) * HEAD_DIM)
    gw = 3 * NSA_REP
    w_misc = jnp.concatenate([c(o_iq, IDX_HEADS * IDX_DIM), c(o_ckv, DSA_KV_RANK),
                              c(o_ik, IDX_DIM), c(o_iw, IDX_HEADS), zeros(128 - IDX_DIM - IDX_HEADS),
                              c(o_g, gw), zeros(128 - gw), c(o_g + gw, gw), zeros(128 - gw)], axis=2)
    return w, blocks_a, o_kc // tn, w_b, w_misc


def kernel(x, p, w_in, norm_mix, norm_ffn, norm_ple, norm_final, w_proj_a, w_proj_b, w_proj_c, w_out,
           cmp_k_w1, cmp_k_w2, cmp_k_pe, cmp_v_w1, cmp_v_w2, cmp_v_pe, dsa_kv_norm, dsa_w_uk, dsa_w_uv,
           rel_bias_table, ffn_w_gate, ffn_w_up, ffn_w_down, ffn_conv_w, ffn_conv_b, ple_w_gate, ple_w_proj):
    batch, seq, d_model = x.shape
    depth = w_in.shape[0]
    m = batch * seq
    assert seq % KT == 0 and seq % SB_T == 0 and seq // SLC_BLOCK <= MAX_SLC_BLOCKS
    ncmp = seq // CMP_STRIDE
    bf = lambda w: w.astype(MXU_DTYPE)

    w_all, blocks_a, block_cmp, w_hm_b, w_misc = _pack_w_in(w_in)
    a0, a1, a2, a3 = blocks_a
    nsa_block = lambda j: jnp.where(j == 0, a0, jnp.where(j == 1, a1, jnp.where(j == 2, a2, a3)))
    w_a, w_b, w_c, w_o = bf(w_proj_a), bf(w_proj_b), bf(w_proj_c), bf(w_out)
    cmp_w1 = bf(jnp.stack([cmp_k_w1, cmp_v_w1], axis=1))
    cmp_w2 = bf(jnp.stack([cmp_k_w2, cmp_v_w2], axis=1))
    cmp_pe = jnp.stack([cmp_k_pe, cmp_v_pe], axis=1).reshape(depth, 2, 2, CMP_STRIDE * HEAD_DIM)
    w_uk, w_uv = bf(dsa_w_uk), bf(dsa_w_uv)
    f_gate, f_up, f_down = bf(ffn_w_gate), bf(ffn_w_up), bf(ffn_w_down)
    pl_gate, pl_proj = bf(ple_w_gate), bf(ple_w_proj)

    wslc, wdsa, wwin, tcmp = _bias_templates(rel_bias_table, ncmp)
    cc = np.arange(ncmp)[None, :]
    jj = np.arange(128)[:, None]
    per = SLC_BLOCK // CMP_STRIDE
    ovl = ((cc >= per * jj - (CMP_BLOCK // CMP_STRIDE - 1)) & (cc <= per * jj + per - 1)
           & (cc < ncmp - 1) & (jj < seq // SLC_BLOCK))
    ovl = jnp.asarray(ovl, MXU_DTYPE)
    low = jnp.asarray(np.arange(KT)[:, None] > np.arange(KT)[None, :], MXU_DTYPE)
    upper = jnp.asarray(np.arange(SB_T)[:, None] > np.arange(SB_T)[None, :], MXU_DTYPE)

    x = x.reshape(m, d_model)
    p = p.reshape(depth, m, p.shape[-1])
    h = _rmsnorm(x, norm_mix[0], MXU_DTYPE)
    for i in range(depth):
        hm_a = _matmul_heads(h, w_all, i, nsa_block, len(blocks_a), batch, MXU_DTYPE, "in_proj_nsa")
        hm_b = _matmul_heads(h, w_hm_b, i, lambda j: j, w_hm_b.shape[2] // PROJ_TN, batch, MXU_DTYPE, "in_proj_sb_dsa")
        cmp_in = _matmul_heads(h, w_all, i, lambda j: block_cmp, 1, batch, F32, "in_proj_cmp")
        misc = _matmul(h, w_misc[i], F32, 1024, MISC_COLS, "in_proj_misc")
        cmp_kv = _compress(cmp_in, cmp_w1[i], cmp_w2[i], cmp_pe[i])
        o_a = _nsa(hm_a, cmp_kv, misc, wslc, wwin, tcmp, ovl)
        o_b = _stick_breaking(hm_b, upper)
        dk, dv = _dsa_kv(misc, dsa_kv_norm[i], w_uk[i], w_uv[i])
        o_c = _dsa(hm_b, dk, dv, misc, wdsa, low)
        y = _merge(h, o_a, o_b, o_c, w_all, i, w_a[i], w_b[i], w_c[i])
        x, h = _matmul_residual(y, w_o[i], x, norm_ffn[i])
        x, h = _conv_ffn(h, x, f_gate[i], f_up[i], f_down[i], ffn_conv_w[i], ffn_conv_b[i], norm_ple[i], seq)
        last = i == depth - 1
        x, h = _ple(h, p[i], x, pl_gate[i], pl_proj[i], norm_final if last else norm_mix[i + 1],
                    F32 if last else MXU_DTYPE)
    return h.reshape(batch, seq, d_model)
```

```python
import functools
import math

import numpy as np
import jax
import jax.numpy as jnp
from jax import lax
from jax.experimental import pallas as pl
from jax.experimental.pallas import tpu as pltpu

F32 = jnp.float32
MXU_DTYPE = jnp.bfloat16

HEAD_DIM = 128
NSA_HEADS = 8
NSA_GROUPS = 2
NSA_REP = NSA_HEADS // NSA_GROUPS
CMP_BLOCK = 32
CMP_STRIDE = 16
SLC_BLOCK = 64
MAX_SLC_BLOCKS = 64
SLC_TOPN = 16
NSA_WINDOW = 512
SB_HEADS = 4
DSA_HEADS = 4
DSA_KV_RANK = 256
IDX_HEADS = 8
IDX_DIM = 64
DSA_TOPK = 256
REL_BUCKETS = 32
REL_MAX_DIST = 128
CONV_WIDTH = 3
EPS = 1e-6

MASKED = -1e30
UNSELECTED = -1e30
INT_MIN = -2 ** 31
LOG2E = math.log2(math.e)
VMEM_LIMIT = 56 * 1024 * 1024

TQ = 256
DSA_TQ = 256
KT = 512
SB_T = 256
SB_HEAD_GROUPS = ((0, 1), (2, 3))
TMPL_C0 = 896
TMPL_W = TMPL_C0 + KT
PROJ_TN = 512
FFN_TF = 512
CMP_BAND = 32

HM_NSA_Q, HM_K_SLC, HM_V_SLC, HM_K_WIN, HM_V_WIN = 0, 8, 10, 12, 14
HM_SB_Q, HM_SB_K, HM_SB_V, HM_DSA_Q = 0, 4, 8, 12
MISC_IDXQ, MISC_CKV, MISC_IDXK, MISC_GATE, MISC_COLS = 0, 4, 6, 7, 9 * 128


def _cparams(sem):
    return pltpu.CompilerParams(dimension_semantics=sem, vmem_limit_bytes=VMEM_LIMIT)


def _dot(a, b):
    return jnp.dot(a, b, preferred_element_type=F32)


def _dot_nt(a, b):
    return lax.dot_general(a, b, (((1,), (1,)), ((), ())), preferred_element_type=F32)


def _rmsnorm_kernel(x_ref, g_ref, o_ref):
    x = x_ref[...]
    y = x * lax.rsqrt(jnp.mean(x * x, axis=-1, keepdims=True) + EPS)
    o_ref[...] = (y * g_ref[...]).astype(o_ref.dtype)


def _rmsnorm(x, g, out_dtype, tm=512):
    m, d = x.shape
    return pl.pallas_call(
        _rmsnorm_kernel,
        grid=(m // tm,),
        in_specs=[pl.BlockSpec((tm, d), lambda i: (i, 0)), pl.BlockSpec((1, d), lambda i: (0, 0))],
        out_specs=pl.BlockSpec((tm, d), lambda i: (i, 0)),
        out_shape=jax.ShapeDtypeStruct((m, d), out_dtype),
        name="rmsnorm",
        compiler_params=_cparams(("parallel",)),
    )(x, g.reshape(1, d))


def _mm_kernel(a_ref, w_ref, o_ref):
    o_ref[...] = _dot(a_ref[...], w_ref[...]).astype(o_ref.dtype)


def _matmul(a, w, out_dtype, tm, tn, name):
    m, k = a.shape
    n = w.shape[1]
    return pl.pallas_call(
        _mm_kernel,
        name=name,
        grid=(m // tm, n // tn),
        in_specs=[pl.BlockSpec((tm, k), lambda i, j: (i, 0)), pl.BlockSpec((k, tn), lambda i, j: (0, j))],
        out_specs=pl.BlockSpec((tm, tn), lambda i, j: (i, j)),
        out_shape=jax.ShapeDtypeStruct((m, n), out_dtype),
        compiler_params=_cparams(("parallel", "parallel")),
    )(a, w)


def _mm_heads_kernel(a_ref, w_ref, o_ref, *, hb):
    r = _dot(a_ref[...], w_ref[0])
    for j in range(hb):
        o_ref[0, j] = r[:, j * HEAD_DIM:(j + 1) * HEAD_DIM].astype(o_ref.dtype)


def _matmul_heads(a, w, layer, col_block, nb, batch, out_dtype, name, tm=1024):
    m, k = a.shape
    s = m // batch
    tn = PROJ_TN
    hb = tn // HEAD_DIM
    spb = s // tm
    return pl.pallas_call(
        functools.partial(_mm_heads_kernel, hb=hb),
        name=name,
        grid=(batch, spb, nb),
        in_specs=[pl.BlockSpec((tm, k), lambda b, i, j: (b * spb + i, 0)),
                  pl.BlockSpec((1, k, tn), lambda b, i, j: (layer, 0, col_block(j)))],
        out_specs=pl.BlockSpec((1, hb, tm, HEAD_DIM), lambda b, i, j: (b, j, i, 0)),
        out_shape=jax.ShapeDtypeStruct((batch, nb * hb, s, HEAD_DIM), out_dtype),
        compiler_params=_cparams(("parallel", "parallel", "parallel")),
    )(a, w)


def _norm_rows(x, g):
    return x * lax.rsqrt(jnp.mean(x * x, axis=-1, keepdims=True) + EPS) * g


def _mm_res_kernel(a_ref, w_ref, x_ref, g_ref, o_ref, h_ref):
    x = x_ref[...] + _dot(a_ref[...], w_ref[...])
    o_ref[...] = x
    h_ref[...] = _norm_rows(x, g_ref[...]).astype(h_ref.dtype)


def _matmul_residual(a, w, x, g, tm=512):
    m, k = a.shape
    n = w.shape[1]
    row = lambda i: (i, 0)
    fixed = lambda i: (0, 0)
    return pl.pallas_call(
        _mm_res_kernel,
        name="out_proj_residual",
        grid=(m // tm,),
        in_specs=[pl.BlockSpec((tm, k), row),
                  pl.BlockSpec((k, n), fixed, pipeline_mode=pl.Buffered(1)),
                  pl.BlockSpec((tm, n), row),
                  pl.BlockSpec((1, n), fixed)],
        out_specs=[pl.BlockSpec((tm, n), row), pl.BlockSpec((tm, n), row)],
        out_shape=[jax.ShapeDtypeStruct((m, n), F32), jax.ShapeDtypeStruct((m, n), MXU_DTYPE)],
        compiler_params=_cparams(("parallel",)),
    )(a, w, x, g.reshape(1, n))


def _merge_kernel(h_ref, oa_ref, ob_ref, oc_ref, ga_ref, gb_ref, gc_ref, wa_ref, wb_ref, wc_ref, y_ref):
    h = h_ref[...]
    y = jax.nn.sigmoid(_dot(h, ga_ref[0])) * _dot(oa_ref[...], wa_ref[...])
    y += jax.nn.sigmoid(_dot(h, gb_ref[0])) * _dot(ob_ref[...], wb_ref[...])
    y += jax.nn.sigmoid(_dot(h, gc_ref[0])) * _dot(oc_ref[...], wc_ref[...])
    y_ref[...] = y.astype(y_ref.dtype)


def _merge(h, o_a, o_b, o_c, w_in, layer, w_a, w_b, w_c, tm=1024, tn=PROJ_TN):
    m, k = h.shape
    d = w_a.shape[1]
    nb = d // tn
    row = lambda i, j: (i, 0)
    col = lambda i, j: (0, j)
    gate = lambda branch: pl.BlockSpec((1, k, tn), lambda i, j: (layer, 0, branch * nb + j))
    return pl.pallas_call(
        _merge_kernel,
        name="branch_merge",
        grid=(m // tm, nb),
        in_specs=[pl.BlockSpec((tm, k), row),
                  pl.BlockSpec((tm, o_a.shape[1]), row), pl.BlockSpec((tm, o_b.shape[1]), row),
                  pl.BlockSpec((tm, o_c.shape[1]), row),
                  gate(0), gate(1), gate(2),
                  pl.BlockSpec((w_a.shape[0], tn), col), pl.BlockSpec((w_b.shape[0], tn), col),
                  pl.BlockSpec((w_c.shape[0], tn), col)],
        out_specs=pl.BlockSpec((tm, tn), lambda i, j: (i, j)),
        out_shape=jax.ShapeDtypeStruct((m, d), MXU_DTYPE),
        compiler_params=_cparams(("parallel", "parallel")),
    )(h, o_a, o_b, o_c, w_in, w_in, w_in, w_a, w_b, w_c)


FFN_HALO = 16


def _ffn_kernel(h_ref, hp_ref, x_ref, wg_ref, wu_ref, wd_ref, cw_ref, cb_ref, g_ref, o_ref, hn_ref, hext_ref,
                *, tiles_per_seq):
    i = pl.program_id(0)
    f = pl.program_id(1)
    tm = h_ref.shape[0]

    @pl.when(f == 0)
    def _():
        first = (i % tiles_per_seq) == 0
        hext_ref[0:FFN_HALO, :] = jnp.where(first, jnp.zeros_like(hp_ref[...]), hp_ref[...])
        hext_ref[FFN_HALO:, :] = h_ref[...]
        o_ref[...] = x_ref[...]

    tf = wg_ref.shape[1]
    halves = [slice(0, tf // 2), slice(tf // 2, tf)]
    a = [_dot(hext_ref[...], wg_ref[:, s]) for s in halves]
    u = [_dot(h_ref[...], wu_ref[:, s]) for s in halves]
    cw = cw_ref[...]
    cb = cb_ref[...]
    act = []
    for a_j, u_j, s in zip(a, u, halves):
        c = (cw[0:1, s] * a_j[FFN_HALO - 2:FFN_HALO - 2 + tm] + cw[1:2, s] * a_j[FFN_HALO - 1:FFN_HALO - 1 + tm]
             + cw[2:3, s] * a_j[FFN_HALO:FFN_HALO + tm]) + cb[:, s]
        act.append((jax.nn.gelu(c) * u_j).astype(MXU_DTYPE))
    for act_j, s in zip(act, halves):
        o_ref[...] += _dot(act_j, wd_ref[s, :])

    @pl.when(f == pl.num_programs(1) - 1)
    def _():
        hn_ref[...] = _norm_rows(o_ref[...], g_ref[...]).astype(hn_ref.dtype)


def _conv_ffn(h, x, w_gate, w_up, w_down, conv_w, conv_b, g, seq, tm=512):
    m, d = h.shape
    ff = w_gate.shape[1]
    tf = FFN_TF
    assert seq % tm == 0
    hb = tm // FFN_HALO
    return pl.pallas_call(
        functools.partial(_ffn_kernel, tiles_per_seq=seq // tm),
        name="conv_ffn",
        grid=(m // tm, ff // tf),
        in_specs=[pl.BlockSpec((tm, d), lambda i, f: (i, 0)),
                  pl.BlockSpec((FFN_HALO, d), lambda i, f: (jnp.maximum(i * hb - 1, 0), 0)),
                  pl.BlockSpec((tm, d), lambda i, f: (i, 0)),
                  pl.BlockSpec((d, tf), lambda i, f: (0, f)),
                  pl.BlockSpec((d, tf), lambda i, f: (0, f)),
                  pl.BlockSpec((tf, d), lambda i, f: (f, 0)),
                  pl.BlockSpec((CONV_WIDTH, tf), lambda i, f: (0, f)),
                  pl.BlockSpec((1, tf), lambda i, f: (0, f)),
                  pl.BlockSpec((1, d), lambda i, f: (0, 0))],
        out_specs=[pl.BlockSpec((tm, d), lambda i, f: (i, 0)), pl.BlockSpec((tm, d), lambda i, f: (i, 0))],
        out_shape=[jax.ShapeDtypeStruct((m, d), F32), jax.ShapeDtypeStruct((m, d), MXU_DTYPE)],
        scratch_shapes=[pltpu.VMEM((tm + FFN_HALO, d), MXU_DTYPE)],
        compiler_params=_cparams(("parallel", "arbitrary")),
    )(h, h, x, w_gate, w_up, w_down, conv_w, conv_b.reshape(1, ff), g.reshape(1, d))


def _ple_kernel(h_ref, p_ref, x_ref, wg_ref, wp_ref, g_ref, o_ref, hn_ref):
    gate = jax.nn.sigmoid(_dot(h_ref[...], wg_ref[...]))
    x = x_ref[...] + gate * _dot(p_ref[...].astype(MXU_DTYPE), wp_ref[...])
    o_ref[...] = x
    hn_ref[...] = _norm_rows(x, g_ref[...]).astype(hn_ref.dtype)


def _ple(h, p, x, w_gate, w_proj, g, norm_dtype, tm=512):
    m, d = h.shape
    row = lambda i: (i, 0)
    fixed = lambda i: (0, 0)
    once = pl.Buffered(1)
    return pl.pallas_call(
        _ple_kernel,
        name="ple",
        grid=(m // tm,),
        in_specs=[pl.BlockSpec((tm, d), row),
                  pl.BlockSpec((tm, p.shape[1]), row),
                  pl.BlockSpec((tm, d), row),
                  pl.BlockSpec((d, d), fixed, pipeline_mode=once),
                  pl.BlockSpec((p.shape[1], d), fixed, pipeline_mode=once),
                  pl.BlockSpec((1, d), fixed)],
        out_specs=[pl.BlockSpec((tm, d), row), pl.BlockSpec((tm, d), row)],
        out_shape=[jax.ShapeDtypeStruct((m, d), F32), jax.ShapeDtypeStruct((m, d), norm_dtype)],
        compiler_params=_cparams(("parallel",)),
    )(h, p, x, w_gate, w_proj, g.reshape(1, d))


def _compress_kernel(x_ref, w1_ref, w2_ref, pe_ref, o_ref):
    x = x_ref[0, 0]
    pe = pe_ref[0]
    half = x.shape[1]
    lo = _dot((x + pe[0:1]).astype(MXU_DTYPE), w1_ref[0, :half, :])
    hi = _dot((x + pe[1:2]).astype(MXU_DTYPE), w1_ref[0, half:, :])
    n = x.shape[0]
    hid = lo + pltpu.roll(hi, n - 1, 0)
    o_ref[0, 0, 0] = _dot(jax.nn.gelu(hid).astype(MXU_DTYPE), w2_ref[0]).astype(o_ref.dtype)


def _compress(kv_hm, w1, w2, pe):
    b, _, s, d = kv_hm.shape
    nrow = s // CMP_STRIDE
    x = kv_hm.reshape(b, 2 * NSA_GROUPS, nrow, CMP_STRIDE * d)
    return pl.pallas_call(
        _compress_kernel,
        name="nsa_compress",
        grid=(b, 2, NSA_GROUPS),
        in_specs=[pl.BlockSpec((1, 1, nrow, CMP_STRIDE * d), lambda bi, kv, g: (bi, kv * NSA_GROUPS + g, 0, 0)),
                  pl.BlockSpec((1,) + w1.shape[1:], lambda bi, kv, g: (kv, 0, 0)),
                  pl.BlockSpec((1,) + w2.shape[1:], lambda bi, kv, g: (kv, 0, 0)),
                  pl.BlockSpec((1, 2, CMP_STRIDE * d), lambda bi, kv, g: (kv, 0, 0))],
        out_specs=pl.BlockSpec((1, 1, 1, nrow, d), lambda bi, kv, g: (bi, kv, g, 0, 0)),
        out_shape=jax.ShapeDtypeStruct((b, 2, NSA_GROUPS, nrow, d), MXU_DTYPE),
        compiler_params=_cparams(("parallel", "parallel", "parallel")),
    )(x, w1, w2, pe)


def _softmax_pv(lg_ref, mx_ref, acc_ref, v_ref, v_index, n_tiles):
    rows = lg_ref.shape[0]
    m = jnp.max(mx_ref[...], axis=1, keepdims=True)
    mx_ref[...] = jnp.zeros_like(mx_ref)
    acc_ref[...] = jnp.zeros_like(acc_ref)

    def chunk(k0, width):
        p = jnp.exp2(lg_ref[:, pl.ds(k0, width)] - m)
        part = p[:, 0:128]
        for c in range(1, width // 128):
            part = part + p[:, c * 128:(c + 1) * 128]
        mx_ref[...] += part
        acc_ref[...] += _dot(p.astype(MXU_DTYPE), v_ref[v_index + (pl.ds(k0, width), slice(None))])

    def body(j, carry):
        chunk(pl.multiple_of(j * (2 * KT), 2 * KT), 2 * KT)
        return carry

    lax.fori_loop(0, n_tiles // 2, body, 0)

    @pl.when(n_tiles % 2 == 1)
    def _():
        chunk(pl.multiple_of((n_tiles - 1) * KT, KT), KT)

    l = jnp.sum(mx_ref[...], axis=1, keepdims=True)
    return acc_ref[...] / l


def _lane_max(s):
    part = s[:, 0:128]
    for c in range(1, s.shape[1] // 128):
        part = jnp.maximum(part, s[:, c * 128:(c + 1) * 128])
    return part


def _nsa_kernel(q_ref, kc_ref, vc_ref, kslc_ref, vslc_ref, kwin_ref, vwin_ref, gate_ref,
                wslc_ref, wwin_ref, tcmp_ref, ovl_ref, o_ref,
                kslc_aug, kwin_aug, vwin_pad, qaug, lg_ref, mx_ref, acc_ref, score_ref):
    i = pl.program_id(2)
    seq = kslc_ref.shape[2]
    d = HEAD_DIM
    rows = NSA_REP * TQ
    scale = d ** -0.5 * LOG2E
    t0 = i * TQ

    @pl.when(i == 0)
    def _():
        kslc_aug[:, 0:d] = kslc_ref[0, 0]
        srow = lax.broadcasted_iota(jnp.int32, (seq, d), 0)
        lane = lax.broadcasted_iota(jnp.int32, (seq, d), 1)
        kslc_aug[:, d:2 * d] = jnp.where((srow >> 6) == lane, 1.0, 0.0).astype(kslc_aug.dtype)
        kwin_aug[0:NSA_WINDOW, 0:d] = jnp.zeros((NSA_WINDOW, d), kwin_aug.dtype)
        kwin_aug[NSA_WINDOW:, 0:d] = kwin_ref[0, 0]
        prow = lax.broadcasted_iota(jnp.int32, (seq + NSA_WINDOW, d), 0)
        plane = lax.broadcasted_iota(jnp.int32, (seq + NSA_WINDOW, d), 1)
        flag = jnp.where(prow < NSA_WINDOW, jnp.where(plane == MAX_SLC_BLOCKS, UNSELECTED, 0.0), 0.0)
        kwin_aug[:, d:2 * d] = flag.astype(kwin_aug.dtype)
        vwin_pad[0:NSA_WINDOW, :] = jnp.zeros((NSA_WINDOW, d), vwin_pad.dtype)
        vwin_pad[NSA_WINDOW:, :] = vwin_ref[0, 0]

    q4 = q_ref[0].reshape(rows, d)

    ncmp = kc_ref.shape[3]
    kc = kc_ref[0, 0, 0]
    vc = vc_ref[0, 0, 0]
    trow = t0 + lax.broadcasted_iota(jnp.int32, (TQ, ncmp), 0)
    cend = lax.broadcasted_iota(jnp.int32, (TQ, ncmp), 1) * CMP_STRIDE + (CMP_BLOCK - 1)
    valid_c = cend <= trow
    sc_all = _dot_nt(q4, kc) * scale
    c0 = i * (TQ // CMP_STRIDE)
    bias_start = pl.multiple_of(ncmp - 128 - 128 * (c0 // 128), 128)
    pb = []
    for r in range(NSA_REP):
        bias = tcmp_ref[0, r, :, pl.ds(bias_start, ncmp)]
        l = jnp.where(valid_c, sc_all[r * TQ:(r + 1) * TQ] + bias, MASKED)
        m = jnp.max(l, axis=1, keepdims=True)
        e = jnp.where(valid_c, jnp.exp2(l - m), 0.0)
        p = e / jnp.maximum(jnp.sum(e, axis=1, keepdims=True), 1e-30)
        pb.append(p.astype(MXU_DTYPE))
    o_cmp = [_dot(pb[r], vc) for r in range(NSA_REP)]
    imp_t = _dot_nt(ovl_ref[...], pb[0])
    for r in range(1, NSA_REP):
        imp_t = imp_t + _dot_nt(ovl_ref[...], pb[r])

    span = NSA_WINDOW + TQ
    w0 = pl.multiple_of(t0, TQ)
    one_lane = lax.broadcasted_iota(jnp.int32, (rows, d), 1) == MAX_SLC_BLOCKS
    q_win = jnp.concatenate([q4, jnp.where(one_lane, 1.0, 0.0).astype(q4.dtype)], axis=1)
    s = _dot_nt(q_win, kwin_aug[pl.ds(w0, span), :]) * scale + wwin_ref[...].reshape(rows, span)
    m = jnp.max(s, axis=1, keepdims=True)
    p = jnp.exp2(s - m)
    l = jnp.sum(p, axis=1, keepdims=True)
    o_win = _dot(p.astype(MXU_DTYPE), vwin_pad[pl.ds(w0, span), :]) / l

    nblk = MAX_SLC_BLOCKS
    jt = lax.broadcasted_iota(jnp.int32, (nblk, TQ), 0)
    tt = t0 + lax.broadcasted_iota(jnp.int32, (nblk, TQ), 1)
    cur = tt >> 6
    imp_t = imp_t[0:nblk]
    score = jnp.where(jt == 0, 1e9, jnp.where(jt == cur, 1e9, jnp.where(jt == cur - 1, 1e9, imp_t)))
    score = jnp.where(jt * SLC_BLOCK <= tt, score, -jnp.inf)
    score_ref[...] = score
    sub = lax.broadcasted_iota(jnp.int32, (8, TQ), 0)
    sv = [score[8 * v:8 * v + 8] for v in range(nblk // 8)]
    beaten = [jnp.zeros((8, TQ), F32) for _ in sv]
    for jp in range(nblk):
        other = score_ref[jp:jp + 1, :]
        for v in range(nblk // 8):
            if 8 * v > jp:
                hit = other >= sv[v]
            elif 8 * v + 7 <= jp:
                hit = other > sv[v]
            else:
                tie_loses = jnp.where(sub > jp - 8 * v, 1.0, 0.0)
                beaten[v] = beaten[v] + jnp.where(other == sv[v], tie_loses, 0.0)
                hit = other > sv[v]
            beaten[v] = beaten[v] + jnp.where(hit, 1.0, 0.0)
    aug_t = jnp.concatenate([jnp.where(b < SLC_TOPN, 0.0, UNSELECTED) for b in beaten], axis=0)
    row = lax.broadcasted_iota(jnp.int32, (d - nblk, TQ), 0)
    aug_t = jnp.concatenate([aug_t, jnp.where(row == 0, 1.0, 0.0)], axis=0)
    aug = aug_t.T
    qaug[:, 0:d] = q4
    for r in range(NSA_REP):
        qaug[r * TQ:(r + 1) * TQ, d:2 * d] = aug.astype(qaug.dtype)
    qa = qaug[...]

    n_tiles = (t0 + TQ + KT - 1) // KT
    off = t0 - (n_tiles - 1) * KT
    mx_ref[...] = jnp.full(mx_ref.shape, MASKED, F32)

    def slc_tile(kt, tmpl_start, width=KT):
        k0 = pl.multiple_of(kt * KT, KT)
        s = _dot_nt(qa, kslc_aug[pl.ds(k0, width), :]) * scale
        if tmpl_start is not None:
            ts = pl.multiple_of(tmpl_start, 128)
            s = s + wslc_ref[:, :, pl.ds(ts, width)].reshape(rows, width)
        lg_ref[:, pl.ds(k0, width)] = s
        mx_ref[...] = jnp.maximum(mx_ref[...], _lane_max(s))

    n_far = jnp.maximum(n_tiles - 2, 0)

    def far_body(j, carry):
        slc_tile(2 * j, None, 2 * KT)
        return carry

    lax.fori_loop(0, n_far // 2, far_body, 0)

    @pl.when(n_far % 2 == 1)
    def _():
        slc_tile(n_far - 1, None)

    @pl.when(n_tiles >= 2)
    def _():
        slc_tile(n_tiles - 2, TMPL_C0 - off - KT)

    slc_tile(n_tiles - 1, TMPL_C0 - off)
    o_slc = _softmax_pv(lg_ref, mx_ref, acc_ref, vslc_ref, (0, 0), n_tiles)

    g = jax.nn.sigmoid(gate_ref[...])
    for r in range(NSA_REP):
        rs = slice(r * TQ, (r + 1) * TQ)
        o = (g[:, 3 * r:3 * r + 1] * o_cmp[r] + g[:, 3 * r + 1:3 * r + 2] * o_slc[rs]
             + g[:, 3 * r + 2:3 * r + 3] * o_win[rs])
        o_ref[:, r * d:(r + 1) * d] = o.astype(o_ref.dtype)


def _nsa(hm, cmp_kv, misc, wslc, wwin, tcmp, ovl):
    b, _, seq, d = hm.shape
    nq = seq // TQ
    rows = NSA_REP * TQ
    ncmp = cmp_kv.shape[3]
    kv_spec = lambda head: pl.BlockSpec((1, 1, seq, d), lambda bi, g, i: (bi, head + g, 0, 0))
    once = pl.Buffered(1)
    return pl.pallas_call(
        _nsa_kernel,
        name="nsa",
        grid=(b, NSA_GROUPS, nq),
        in_specs=[pl.BlockSpec((1, NSA_REP, TQ, d), lambda bi, g, i: (bi, g, i, 0)),
                  pl.BlockSpec((1, 1, 1, ncmp, d), lambda bi, g, i: (bi, 0, g, 0, 0)),
                  pl.BlockSpec((1, 1, 1, ncmp, d), lambda bi, g, i: (bi, 1, g, 0, 0)),
                  kv_spec(HM_K_SLC), kv_spec(HM_V_SLC), kv_spec(HM_K_WIN), kv_spec(HM_V_WIN),
                  pl.BlockSpec((TQ, 128), lambda bi, g, i: (bi * nq + i, MISC_GATE + g)),
                  pl.BlockSpec((NSA_REP, TQ, TMPL_W), lambda bi, g, i: (g, 0, 0), pipeline_mode=once),
                  pl.BlockSpec((NSA_REP, TQ, NSA_WINDOW + TQ), lambda bi, g, i: (g, 0, 0), pipeline_mode=once),
                  pl.BlockSpec((1, NSA_REP) + tcmp.shape[2:], lambda bi, g, i: (i % tcmp.shape[0], g, 0, 0)),
                  pl.BlockSpec(ovl.shape, lambda bi, g, i: (0, 0), pipeline_mode=once)],
        out_specs=pl.BlockSpec((TQ, NSA_REP * d), lambda bi, g, i: (bi * nq + i, g)),
        out_shape=jax.ShapeDtypeStruct((b * seq, NSA_HEADS * d), MXU_DTYPE),
        scratch_shapes=[pltpu.VMEM((seq, 2 * d), MXU_DTYPE),
                        pltpu.VMEM((seq + NSA_WINDOW, 2 * d), MXU_DTYPE),
                        pltpu.VMEM((seq + NSA_WINDOW, d), MXU_DTYPE),
                        pltpu.VMEM((rows, 2 * d), MXU_DTYPE),
                        pltpu.VMEM((rows, seq), F32),
                        pltpu.VMEM((rows, 128), F32),
                        pltpu.VMEM((rows, d), F32),
                        pltpu.VMEM((MAX_SLC_BLOCKS, TQ), F32)],
        compiler_params=_cparams(("parallel", "parallel", "arbitrary")),
    )(hm, cmp_kv, cmp_kv, hm, hm, hm, hm, misc, wslc, wwin, tcmp, ovl)


def _sb_kernel(q_ref, k_ref, v_ref, upper_ref, o_ref, acc_ref, carry_ref):
    i = pl.program_id(1)
    t = SB_T
    d = HEAD_DIM
    scale = d ** -0.5 * LOG2E
    upper = upper_ref[...]
    acc_ref[...] = jnp.zeros_like(acc_ref)
    carry_ref[...] = jnp.zeros_like(carry_ref)

    heads = range(SB_HEADS)

    def tiles(kts, mask, heads=heads):
        k0 = [pl.multiple_of(kt * t, t) for kt in kts]
        chains = [(j, h) for j in range(len(kts)) for h in heads]
        z = {c: _dot_nt(q_ref[0, c[1]], k_ref[0, c[1], pl.ds(k0[c[0]], t), :]) * scale for c in chains}
        neg_abs = {c: lax.bitcast_convert_type(lax.bitcast_convert_type(z[c], jnp.int32) | INT_MIN, F32)
                   for c in chains}
        ls = {c: jnp.minimum(z[c], 0.0) - jnp.log2(1.0 + jnp.exp2(neg_abs[c])) for c in chains}
        lk = {c: ls[c] - z[c] for c in chains}
        if mask is not None:
            lk = {c: jnp.where(mask, lk[c], 0.0) for c in chains}
        carry = {}
        for h in heads:
            run = carry_ref[h]
            for j in range(len(kts)):
                carry[(j, h)] = run
                run = run + jnp.sum(lk[(j, h)], axis=1, keepdims=True)
            carry_ref[h] = run
        hi = {c: lk[c].astype(MXU_DTYPE) for c in chains}
        lo = {c: (lk[c] - hi[c].astype(F32)).astype(MXU_DTYPE) for c in chains}
        after = {c: carry[c] + (_dot(hi[c], upper) + _dot(lo[c], upper)) for c in chains}
        w = {c: jnp.exp2(ls[c] + after[c]) for c in chains}
        if mask is not None:
            w = {c: jnp.where(mask, w[c], 0.0) for c in chains}
        for j, h in chains:
            acc_ref[h] += _dot(w[(j, h)].astype(MXU_DTYPE), v_ref[0, h, pl.ds(k0[j], t), :])

    strict = lax.broadcasted_iota(jnp.int32, (t, t), 1) < lax.broadcasted_iota(jnp.int32, (t, t), 0)
    tiles([i], strict)

    def body(n, carry):
        for hs in SB_HEAD_GROUPS:
            tiles([i - 1 - 2 * n, i - 2 - 2 * n], None, hs)
        return carry

    lax.fori_loop(0, i // 2, body, 0)

    @pl.when(i % 2 == 1)
    def _():
        tiles([0], None)

    for h in range(SB_HEADS):
        o_ref[:, h * d:(h + 1) * d] = acc_ref[h].astype(o_ref.dtype)


def _stick_breaking(hm, upper):
    b, _, seq, d = hm.shape
    nq = seq // SB_T
    return pl.pallas_call(
        _sb_kernel,
        grid=(b, nq),
        in_specs=[pl.BlockSpec((1, SB_HEADS, SB_T, d), lambda bi, i: (bi, HM_SB_Q // SB_HEADS, i, 0)),
                  pl.BlockSpec((1, SB_HEADS, seq, d), lambda bi, i: (bi, HM_SB_K // SB_HEADS, 0, 0)),
                  pl.BlockSpec((1, SB_HEADS, seq, d), lambda bi, i: (bi, HM_SB_V // SB_HEADS, 0, 0)),
                  pl.BlockSpec(upper.shape, lambda bi, i: (0, 0))],
        out_specs=pl.BlockSpec((SB_T, SB_HEADS * d), lambda bi, i: (bi * nq + i, 0)),
        out_shape=jax.ShapeDtypeStruct((b * seq, SB_HEADS * d), MXU_DTYPE),
        scratch_shapes=[pltpu.VMEM((SB_HEADS, SB_T, d), F32), pltpu.VMEM((SB_HEADS, SB_T, 1), F32)],
        name="stick_breaking",
        compiler_params=_cparams(("parallel", "parallel")),
    )(hm, hm, hm, upper)


def _dsa_kv_kernel(c_ref, g_ref, wk_ref, wv_ref, k_ref, v_ref):
    c = c_ref[...]
    y = c * lax.rsqrt(jnp.mean(c * c, axis=-1, keepdims=True) + EPS)
    y = (y * g_ref[...]).astype(MXU_DTYPE)
    k_ref[...] = _dot(y, wk_ref[...]).astype(k_ref.dtype)
    v_ref[...] = _dot(y, wv_ref[...]).astype(v_ref.dtype)


def _dsa_kv(misc, kv_norm, w_uk, w_uv, tm=512):
    m = misc.shape[0]
    r = DSA_KV_RANK
    out = jax.ShapeDtypeStruct((m, HEAD_DIM), MXU_DTYPE)
    return pl.pallas_call(
        _dsa_kv_kernel,
        name="dsa_kv",
        grid=(m // tm,),
        in_specs=[pl.BlockSpec((tm, r), lambda i: (i, MISC_CKV * 128 // r)),
                  pl.BlockSpec((1, r), lambda i: (0, 0)),
                  pl.BlockSpec((r, HEAD_DIM), lambda i: (0, 0)),
                  pl.BlockSpec((r, HEAD_DIM), lambda i: (0, 0))],
        out_specs=[pl.BlockSpec((tm, HEAD_DIM), lambda i: (i, 0)), pl.BlockSpec((tm, HEAD_DIM), lambda i: (i, 0))],
        out_shape=[out, out],
        compiler_params=_cparams(("parallel",)),
    )(misc, kv_norm.reshape(1, r), w_uk, w_uv)


PLANE_GROUPS_PER_TILE = KT // 256


def _bit_planes(words):
    a = list(words)
    j, m = 16, 0x0000FFFF
    while j:
        sh = jnp.full(a[0].shape, j, jnp.int32)
        for k in range(32):
            if not k & j:
                t = (a[k] ^ lax.shift_right_logical(a[k + j], sh)) & m
                a[k] = a[k] ^ t
                a[k + j] = a[k + j] ^ (t << j)
        j >>= 1
        m = (m ^ (m << j)) & 0xFFFFFFFF
    return a


def _dsa_kernel(q_ref, k_ref, v_ref, iq_ref, ik_ref, iw_ref, wd_ref, low_ref, o_ref,
                key_ref, plane_ref, alive_ref, add_ref, lg_ref, mx_ref, acc_ref, *, n_keep):
    i = pl.program_id(1)
    d = HEAD_DIM
    tq = DSA_TQ
    rows = DSA_HEADS * tq
    scale = d ** -0.5 * LOG2E
    t0 = i * tq
    n_tiles = (t0 + tq + KT - 1) // KT
    off = t0 - (n_tiles - 1) * KT

    iq = iq_ref[...].astype(MXU_DTYPE)
    qh = [iq[:, h * IDX_DIM:(h + 1) * IDX_DIM] for h in range(IDX_HEADS)]
    wi_t = iw_ref[...].T * (IDX_HEADS ** -0.5) * (IDX_DIM ** -0.5)
    wh = [wi_t[IDX_DIM + h:IDX_DIM + h + 1, :] for h in range(IDX_HEADS)]
    tq_row = t0 + lax.broadcasted_iota(jnp.int32, (KT, tq), 1)
    krow = lax.broadcasted_iota(jnp.int32, (KT, tq), 0)

    def column_sum(a):
        return jnp.sum(a.reshape(KT // 8, 8, tq), axis=0)

    def score_tile(kt, causal):
        k0 = pl.multiple_of(kt * KT, KT)
        ki = ik_ref[pl.ds(k0, KT), 0:IDX_DIM].astype(MXU_DTYPE)
        dots = [_dot_nt(ki, qh[h]) for h in range(IDX_HEADS)]
        sc = wh[0] * jnp.maximum(dots[0], 0.0)
        for h in range(1, IDX_HEADS):
            sc = sc + wh[h] * jnp.maximum(dots[h], 0.0)
        if causal:
            sc = jnp.where(k0 + krow <= tq_row, sc, -jnp.inf)
        bits = lax.bitcast_convert_type(sc, jnp.int32)
        sign = bits >> 31
        key = (bits ^ (sign & 0x7FFFFFFF)) - sign
        key_ref[pl.ds(k0, KT), :] = key
        ukey = key ^ INT_MIN
        for g in range(PLANE_GROUPS_PER_TILE):
            words = [ukey[(32 * g + w) * 8:(32 * g + w + 1) * 8] for w in range(32)]
            for x, plane in enumerate(_bit_planes(words)):
                plane_ref[x, PLANE_GROUPS_PER_TILE * kt + g] = plane

    @pl.when((pl.program_id(0) == 0) & (i == 0))
    def _():
        plane_ref[...] = jnp.zeros_like(plane_ref)

    def score_body(kt, carry):
        score_tile(kt, False)
        return carry

    lax.fori_loop(0, n_tiles - 1, score_body, 0)
    score_tile(n_tiles - 1, True)

    ngrp = alive_ref.shape[0]
    for g in range(ngrp):
        alive_ref[g] = jnp.where(g < PLANE_GROUPS_PER_TILE * n_tiles, -1, 0) + jnp.zeros((8, tq), jnp.int32)

    def bit_body(x, carry):
        thr_u, remaining = carry
        ones = [alive_ref[g] & plane_ref[x, g] for g in range(ngrp)]
        c = lax.population_count(ones[0])
        for g in range(1, ngrp):
            c = c + lax.population_count(ones[g])
        c = jnp.sum(c, axis=0, keepdims=True)
        take = c >= remaining
        for g in range(ngrp):
            alive_ref[g] = jnp.where(take, ones[g], alive_ref[g] ^ ones[g])
        bit = jnp.int32(1) << (31 - x)
        return jnp.where(take, thr_u | bit, thr_u), jnp.where(take, remaining, remaining - c)

    thr_u, need = lax.fori_loop(0, 32, bit_body,
                                (jnp.zeros((1, tq), jnp.int32), jnp.full((1, tq), n_keep, jnp.int32)))
    thr = thr_u ^ INT_MIN
    n_equal = lax.population_count(alive_ref[0])
    for g in range(1, ngrp):
        n_equal = n_equal + lax.population_count(alive_ref[g])
    n_equal = jnp.sum(n_equal, axis=0, keepdims=True)
    need = need.astype(F32)
    surplus = jnp.max(n_equal.astype(F32) - need)

    @pl.when(surplus <= 0)
    def _():
        def mask_body(kt, carry):
            k0 = pl.multiple_of(kt * KT, KT)
            add_ref[:, pl.ds(k0, KT)] = jnp.where(key_ref[pl.ds(k0, KT), :] >= thr, 0.0, MASKED).T
            return carry

        lax.fori_loop(0, n_tiles, mask_body, 0)

    @pl.when(surplus > 0)
    def _():
        def mask_body(kt, seen):
            k0 = pl.multiple_of(kt * KT, KT)
            kk = key_ref[pl.ds(k0, KT), :]
            eq = jnp.where(kk == thr, 1.0, 0.0)
            before = seen + _dot(low_ref[...], eq.astype(MXU_DTYPE))
            tie = jnp.where(before < need, 0.0, MASKED)
            add_t = jnp.where(kk > thr, 0.0, jnp.where(kk == thr, tie, MASKED))
            add_ref[:, pl.ds(k0, KT)] = add_t.T
            return seen + jnp.sum(column_sum(eq), axis=0, keepdims=True)

        lax.fori_loop(0, n_tiles, mask_body, jnp.zeros((1, tq), F32))

    q4 = q_ref[0].reshape(rows, d)
    mx_ref[...] = jnp.full(mx_ref.shape, MASKED, F32)

    def att_tile(kt, tmpl_start, width=KT):
        k0 = pl.multiple_of(kt * KT, KT)
        s = _dot_nt(q4, k_ref[0, pl.ds(k0, width), :]) * scale
        addm = add_ref[:, pl.ds(k0, width)]
        s = s + jnp.concatenate([addm] * DSA_HEADS, axis=0)
        if tmpl_start is not None:
            ts = pl.multiple_of(tmpl_start, 128)
            s = s + wd_ref[:, :, pl.ds(ts, width)].reshape(rows, width)
        lg_ref[:, pl.ds(k0, width)] = s
        mx_ref[...] = jnp.maximum(mx_ref[...], _lane_max(s))

    n_far = jnp.maximum(n_tiles - 2, 0)

    def far_body(j, carry):
        att_tile(2 * j, None, 2 * KT)
        return carry

    lax.fori_loop(0, n_far // 2, far_body, 0)

    @pl.when(n_far % 2 == 1)
    def _():
        att_tile(n_far - 1, None)

    @pl.when(n_tiles >= 2)
    def _():
        att_tile(n_tiles - 2, TMPL_C0 - off - KT)

    att_tile(n_tiles - 1, TMPL_C0 - off)
    o = _softmax_pv(lg_ref, mx_ref, acc_ref, v_ref, (0,), n_tiles)
    for r in range(DSA_HEADS):
        o_ref[:, r * d:(r + 1) * d] = o[r * tq:(r + 1) * tq].astype(o_ref.dtype)


def _dsa(hm, k, v, misc, wdsa, low):
    b, _, seq, d = hm.shape
    tq = DSA_TQ
    nq = seq // tq
    rows = DSA_HEADS * tq
    k = k.reshape(b, seq, d)
    v = v.reshape(b, seq, d)
    n_keep = min(DSA_TOPK, seq // 4)
    ngrp = PLANE_GROUPS_PER_TILE * (seq // KT)
    return pl.pallas_call(
        functools.partial(_dsa_kernel, n_keep=n_keep),
        name="dsa",
        grid=(b, nq),
        in_specs=[pl.BlockSpec((1, DSA_HEADS, tq, d), lambda bi, i: (bi, HM_DSA_Q // DSA_HEADS, i, 0)),
                  pl.BlockSpec((1, seq, d), lambda bi, i: (bi, 0, 0)),
                  pl.BlockSpec((1, seq, d), lambda bi, i: (bi, 0, 0)),
                  pl.BlockSpec((tq, IDX_HEADS * IDX_DIM), lambda bi, i: (bi * nq + i, MISC_IDXQ)),
                  pl.BlockSpec((seq, 128), lambda bi, i: (bi, MISC_IDXK)),
                  pl.BlockSpec((tq, 128), lambda bi, i: (bi * nq + i, MISC_IDXK)),
                  pl.BlockSpec((DSA_HEADS, tq, TMPL_W), lambda bi, i: (0, 0, 0), pipeline_mode=pl.Buffered(1)),
                  pl.BlockSpec(low.shape, lambda bi, i: (0, 0), pipeline_mode=pl.Buffered(1))],
        out_specs=pl.BlockSpec((tq, DSA_HEADS * d), lambda bi, i: (bi * nq + i, 0)),
        out_shape=jax.ShapeDtypeStruct((b * seq, DSA_HEADS * d), MXU_DTYPE),
        scratch_shapes=[pltpu.VMEM((seq, tq), jnp.int32),
                        pltpu.VMEM((32, ngrp, 8, tq), jnp.int32),
                        pltpu.VMEM((ngrp, 8, tq), jnp.int32),
                        pltpu.VMEM((tq, seq), F32),
                        pltpu.VMEM((rows, seq), F32),
                        pltpu.VMEM((rows, 128), F32),
                        pltpu.VMEM((rows, d), F32)],
        compiler_params=_cparams(("arbitrary", "arbitrary")),
    )(hm, k, v, misc, misc, misc, wdsa, low)


def _t5_bucket(dist):
    n = jnp.maximum(dist, 0)
    max_exact = REL_BUCKETS // 2
    nf = jnp.maximum(n, 1).astype(F32)
    large = max_exact + (jnp.log(nf / max_exact) / math.log(REL_MAX_DIST / max_exact)
                         * (REL_BUCKETS - max_exact)).astype(jnp.int32)
    large = jnp.minimum(large, REL_BUCKETS - 1)
    return jnp.where(n < max_exact, n, large)


def _bias_templates(rel_tab, ncmp):
    far = REL_MAX_DIST
    by_dist = rel_tab[_t5_bucket(jnp.arange(far + 1))] - rel_tab[REL_BUCKETS - 1][None, :]
    by_dist = by_dist.T * LOG2E

    def build(dist, valid, fill=MASKED):
        t = by_dist[:, np.clip(dist, 0, far)]
        return jnp.where(valid[None], t, fill).astype(F32)

    def toeplitz(u, nrows, width):
        nh, l = u.shape
        return jnp.tile(u, (1, nrows))[:, :nrows * (l - 1)].reshape(nh, nrows, l - 1)[:, :, :width]

    def diagonals(nrows, width):
        l = width + nrows
        k = np.arange(l)
        return np.where(k < width, k, k - l)

    def causal(heads, nrows):
        dist = TMPL_C0 - diagonals(nrows, TMPL_W)
        return toeplitz(build(dist, dist >= 0)[heads], nrows, TMPL_W)

    nsa = slice(0, NSA_HEADS)
    span = NSA_WINDOW + TQ
    dist = NSA_WINDOW - diagonals(TQ, span)
    window = toeplitz(build(dist, (dist >= 0) & (dist < NSA_WINDOW))[nsa], TQ, span)
    cc = np.arange(CMP_BAND)[None, :] - CMP_BAND // 2
    dist = np.arange(TQ)[:, None] - CMP_STRIDE * cc - (CMP_BLOCK - 1)
    band = build(dist, dist >= 0, 0.0)[nsa]
    width = 2 * ncmp - 128
    step = TQ // CMP_STRIDE
    cmp = []
    for v in range(128 // step):
        left = ncmp - 128 + step * v
        canvas = jnp.pad(band, ((0, 0), (0, 0), (left, width - left)))
        cmp.append(canvas[:, :, CMP_BAND // 2:CMP_BAND // 2 + width])
    return causal(nsa, TQ), causal(slice(NSA_HEADS, None), DSA_TQ), window, jnp.stack(cmp)


def _pack_w_in(w_in):
    d3 = 3 * w_in.shape[1]
    kv = NSA_GROUPS * HEAD_DIM
    o_q = d3
    o_kc = o_q + NSA_HEADS * HEAD_DIM
    o_vc, o_ks, o_vs, o_kw, o_vw = (o_kc + j * kv for j in range(1, 6))
    o_g = o_vw + kv
    o_sbq = o_g + 3 * NSA_HEADS
    o_sbk = o_sbq + SB_HEADS * HEAD_DIM
    o_sbv = o_sbk + SB_HEADS * HEAD_DIM
    o_dq = o_sbv + SB_HEADS * HEAD_DIM
    o_ckv = o_dq + DSA_HEADS * HEAD_DIM
    o_iq = o_ckv + DSA_KV_RANK
    o_ik = o_iq + IDX_HEADS * IDX_DIM
    o_iw = o_ik + IDX_DIM
    w = w_in.astype(MXU_DTYPE)
    c = lambda a, n: w[:, :, a:a + n]
    zeros = lambda n: jnp.zeros(w.shape[:2] + (n,), w.dtype)
    tn = PROJ_TN
    assert all(o % tn == 0 for o in (d3, o_q, o_kc, o_ks, o_kw)) and 2 * kv == tn and NSA_HEADS * HEAD_DIM == 2 * tn
    blocks_a = (o_q // tn, o_q // tn + 1, o_ks // tn, o_kw // tn)
    w_b = c(o_sbq, (3 * SB_HEADS + DSA_HEADS) * HEAD_DIM)
    gw = 3 * NSA_REP
    w_misc = jnp.concatenate([c(o_iq, IDX_HEADS * IDX_DIM), c(o_ckv, DSA_KV_RANK),
                              c(o_ik, IDX_DIM), c(o_iw, IDX_HEADS), zeros(128 - IDX_DIM - IDX_HEADS),
                              c(o_g, gw), zeros(128 - gw), c(o_g + gw, gw), zeros(128 - gw)], axis=2)
    return w, blocks_a, o_kc // tn, w_b, w_misc


def kernel(x, p, w_in, norm_mix, norm_ffn, norm_ple, norm_final, w_proj_a, w_proj_b, w_proj_c, w_out,
           cmp_k_w1, cmp_k_w2, cmp_k_pe, cmp_v_w1, cmp_v_w2, cmp_v_pe, dsa_kv_norm, dsa_w_uk, dsa_w_uv,
           rel_bias_table, ffn_w_gate, ffn_w_up, ffn_w_down, ffn_conv_w, ffn_conv_b, ple_w_gate, ple_w_proj):
    batch, seq, d_model = x.shape
    depth = w_in.shape[0]
    m = batch * seq
    assert seq % KT == 0 and seq % SB_T == 0 and seq // SLC_BLOCK <= MAX_SLC_BLOCKS
    ncmp = seq // CMP_STRIDE
    bf = lambda w: w.astype(MXU_DTYPE)

    w_all, blocks_a, block_cmp, w_hm_b, w_misc = _pack_w_in(w_in)
    a0, a1, a2, a3 = blocks_a
    nsa_block = lambda j: jnp.where(j == 0, a0, jnp.where(j == 1, a1, jnp.where(j == 2, a2, a3)))
    w_a, w_b, w_c, w_o = bf(w_proj_a), bf(w_proj_b), bf(w_proj_c), bf(w_out)
    cmp_w1 = bf(jnp.stack([cmp_k_w1, cmp_v_w1], axis=1))
    cmp_w2 = bf(jnp.stack([cmp_k_w2, cmp_v_w2], axis=1))
    cmp_pe = jnp.stack([cmp_k_pe, cmp_v_pe], axis=1).reshape(depth, 2, 2, CMP_STRIDE * HEAD_DIM)
    w_uk, w_uv = bf(dsa_w_uk), bf(dsa_w_uv)
    f_gate, f_up, f_down = bf(ffn_w_gate), bf(ffn_w_up), bf(ffn_w_down)
    pl_gate, pl_proj = bf(ple_w_gate), bf(ple_w_proj)

    wslc, wdsa, wwin, tcmp = _bias_templates(rel_bias_table, ncmp)
    cc = np.arange(ncmp)[None, :]
    jj = np.arange(128)[:, None]
    per = SLC_BLOCK // CMP_STRIDE
    ovl = ((cc >= per * jj - (CMP_BLOCK // CMP_STRIDE - 1)) & (cc <= per * jj + per - 1)
           & (cc < ncmp - 1) & (jj < seq // SLC_BLOCK))
    ovl = jnp.asarray(ovl, MXU_DTYPE)
    low = jnp.asarray(np.arange(KT)[:, None] > np.arange(KT)[None, :], MXU_DTYPE)
    upper = jnp.asarray(np.arange(SB_T)[:, None] > np.arange(SB_T)[None, :], MXU_DTYPE)

    x = x.reshape(m, d_model)
    p = p.reshape(depth, m, p.shape[-1])
    h = _rmsnorm(x, norm_mix[0], MXU_DTYPE)
    for i in range(depth):
        hm_a = _matmul_heads(h, w_all, i, nsa_block, len(blocks_a), batch, MXU_DTYPE, "in_proj_nsa")
        hm_b = _matmul_heads(h, w_hm_b, i, lambda j: j, w_hm_b.shape[2] // PROJ_TN, batch, MXU_DTYPE, "in_proj_sb_dsa")
        cmp_in = _matmul_heads(h, w_all, i, lambda j: block_cmp, 1, batch, F32, "in_proj_cmp")
        misc = _matmul(h, w_misc[i], F32, 1024, MISC_COLS, "in_proj_misc")
        cmp_kv = _compress(cmp_in, cmp_w1[i], cmp_w2[i], cmp_pe[i])
        o_a = _nsa(hm_a, cmp_kv, misc, wslc, wwin, tcmp, ovl)
        o_b = _stick_breaking(hm_b, upper)
        dk, dv = _dsa_kv(misc, dsa_kv_norm[i], w_uk[i], w_uv[i])
        o_c = _dsa(hm_b, dk, dv, misc, wdsa, low)
        y = _merge(h, o_a, o_b, o_c, w_all, i, w_a[i], w_b[i], w_c[i])
        x, h = _matmul_residual(y, w_o[i], x, norm_ffn[i])
        x, h = _conv_ffn(h, x, f_gate[i], f_up[i], f_down[i], ffn_conv_w[i], ffn_conv_b[i], norm_ple[i], seq)
        last = i == depth - 1
        x, h = _ple(h, p[i], x, pl_gate[i], pl_proj[i], norm_final if last else norm_mix[i + 1],
                    F32 if last else MXU_DTYPE)
    return h.reshape(batch, seq, d_model)
```

```python
import functools
import math

import numpy as np
import jax
import jax.numpy as jnp
from jax import lax
from jax.experimental import pallas as pl
from jax.experimental.pallas import tpu as pltpu

F32 = jnp.float32
MXU_DTYPE = jnp.bfloat16

HEAD_DIM = 128
NSA_HEADS = 8
NSA_GROUPS = 2
NSA_REP = NSA_HEADS // NSA_GROUPS
CMP_BLOCK = 32
CMP_STRIDE = 16
SLC_BLOCK = 64
MAX_SLC_BLOCKS = 64
SLC_TOPN = 16
NSA_WINDOW = 512
SB_HEADS = 4
DSA_HEADS = 4
DSA_KV_RANK = 256
IDX_HEADS = 8
IDX_DIM = 64
DSA_TOPK = 256
REL_BUCKETS = 32
REL_MAX_DIST = 128
CONV_WIDTH = 3
EPS = 1e-6

MASKED = -1e30
UNSELECTED = -1e30
INT_MIN = -2 ** 31
LOG2E = math.log2(math.e)
VMEM_LIMIT = 56 * 1024 * 1024

TQ = 256
DSA_TQ = 256
KT = 512
SB_T = 256
SB_HEAD_GROUPS = ((0, 1), (2, 3))
TMPL_C0 = 896
TMPL_W = TMPL_C0 + KT
PROJ_TN = 512
FFN_TF = 512
CMP_BAND = 32

HM_NSA_Q, HM_K_SLC, HM_V_SLC, HM_K_WIN, HM_V_WIN = 0, 8, 10, 12, 14
HM_SB_Q, HM_SB_K, HM_SB_V, HM_DSA_Q = 0, 4, 8, 12
MISC_IDXQ, MISC_CKV, MISC_IDXK, MISC_GATE, MISC_COLS = 0, 4, 6, 7, 9 * 128


def _cparams(sem):
    return pltpu.CompilerParams(dimension_semantics=sem, vmem_limit_bytes=VMEM_LIMIT)


def _dot(a, b):
    return jnp.dot(a, b, preferred_element_type=F32)


def _dot_nt(a, b):
    return lax.dot_general(a, b, (((1,), (1,)), ((), ())), preferred_element_type=F32)


def _rmsnorm_kernel(x_ref, g_ref, o_ref):
    x = x_ref[...]
    y = x * lax.rsqrt(jnp.mean(x * x, axis=-1, keepdims=True) + EPS)
    o_ref[...] = (y * g_ref[...]).astype(o_ref.dtype)


def _rmsnorm(x, g, out_dtype, tm=512):
    m, d = x.shape
    return pl.pallas_call(
        _rmsnorm_kernel,
        grid=(m // tm,),
        in_specs=[pl.BlockSpec((tm, d), lambda i: (i, 0)), pl.BlockSpec((1, d), lambda i: (0, 0))],
        out_specs=pl.BlockSpec((tm, d), lambda i: (i, 0)),
        out_shape=jax.ShapeDtypeStruct((m, d), out_dtype),
        name="rmsnorm",
        compiler_params=_cparams(("parallel",)),
    )(x, g.reshape(1, d))


def _mm_kernel(a_ref, w_ref, o_ref):
    o_ref[...] = _dot(a_ref[...], w_ref[...]).astype(o_ref.dtype)


def _matmul(a, w, out_dtype, tm, tn, name):
    m, k = a.shape
    n = w.shape[1]
    return pl.pallas_call(
        _mm_kernel,
        name=name,
        grid=(m // tm, n // tn),
        in_specs=[pl.BlockSpec((tm, k), lambda i, j: (i, 0)), pl.BlockSpec((k, tn), lambda i, j: (0, j))],
        out_specs=pl.BlockSpec((tm, tn), lambda i, j: (i, j)),
        out_shape=jax.ShapeDtypeStruct((m, n), out_dtype),
        compiler_params=_cparams(("parallel", "parallel")),
    )(a, w)


def _mm_heads_kernel(a_ref, w_ref, o_ref, *, hb):
    r = _dot(a_ref[...], w_ref[0])
    for j in range(hb):
        o_ref[0, j] = r[:, j * HEAD_DIM:(j + 1) * HEAD_DIM].astype(o_ref.dtype)


def _matmul_heads(a, w, layer, col_block, nb, batch, out_dtype, name, tm=1024):
    m, k = a.shape
    s = m // batch
    tn = PROJ_TN
    hb = tn // HEAD_DIM
    spb = s // tm
    return pl.pallas_call(
        functools.partial(_mm_heads_kernel, hb=hb),
        name=name,
        grid=(batch, spb, nb),
        in_specs=[pl.BlockSpec((tm, k), lambda b, i, j: (b * spb + i, 0)),
                  pl.BlockSpec((1, k, tn), lambda b, i, j: (layer, 0, col_block(j)))],
        out_specs=pl.BlockSpec((1, hb, tm, HEAD_DIM), lambda b, i, j: (b, j, i, 0)),
        out_shape=jax.ShapeDtypeStruct((batch, nb * hb, s, HEAD_DIM), out_dtype),
        compiler_params=_cparams(("parallel", "parallel", "parallel")),
    )(a, w)


def _norm_rows(x, g):
    return x * lax.rsqrt(jnp.mean(x * x, axis=-1, keepdims=True) + EPS) * g


def _mm_res_kernel(a_ref, w_ref, x_ref, g_ref, o_ref, h_ref):
    x = x_ref[...] + _dot(a_ref[...], w_ref[0])
    o_ref[...] = x
    h_ref[...] = _norm_rows(x, g_ref[...]).astype(h_ref.dtype)


def _matmul_residual(a, w, layer, x, g, tm=512):
    m, k = a.shape
    n = w.shape[2]
    row = lambda i: (i, 0)
    fixed = lambda i: (0, 0)
    return pl.pallas_call(
        _mm_res_kernel,
        name="out_proj_residual",
        grid=(m // tm,),
        in_specs=[pl.BlockSpec((tm, k), row),
                  pl.BlockSpec((1, k, n), lambda i: (layer, 0, 0), pipeline_mode=pl.Buffered(1)),
                  pl.BlockSpec((tm, n), row),
                  pl.BlockSpec((1, n), fixed)],
        out_specs=[pl.BlockSpec((tm, n), row), pl.BlockSpec((tm, n), row)],
        out_shape=[jax.ShapeDtypeStruct((m, n), F32), jax.ShapeDtypeStruct((m, n), MXU_DTYPE)],
        compiler_params=_cparams(("parallel",)),
    )(a, w, x, g.reshape(1, n))


def _merge_kernel(h_ref, oa_ref, ob_ref, oc_ref, ga_ref, gb_ref, gc_ref, wa_ref, wb_ref, wc_ref, y_ref):
    h = h_ref[...]
    y = jax.nn.sigmoid(_dot(h, ga_ref[0])) * _dot(oa_ref[...], wa_ref[0])
    y += jax.nn.sigmoid(_dot(h, gb_ref[0])) * _dot(ob_ref[...], wb_ref[0])
    y += jax.nn.sigmoid(_dot(h, gc_ref[0])) * _dot(oc_ref[...], wc_ref[0])
    y_ref[...] = y.astype(y_ref.dtype)


def _merge(h, o_a, o_b, o_c, w_in, layer, w_a, w_b, w_c, tm=1024, tn=PROJ_TN):
    m, k = h.shape
    d = w_a.shape[2]
    nb = d // tn
    row = lambda i, j: (i, 0)
    gate = lambda branch: pl.BlockSpec((1, k, tn), lambda i, j: (layer, 0, branch * nb + j))
    proj = lambda w: pl.BlockSpec((1, w.shape[1], tn), lambda i, j: (layer, 0, j))
    return pl.pallas_call(
        _merge_kernel,
        name="branch_merge",
        grid=(m // tm, nb),
        in_specs=[pl.BlockSpec((tm, k), row),
                  pl.BlockSpec((tm, o_a.shape[1]), row), pl.BlockSpec((tm, o_b.shape[1]), row),
                  pl.BlockSpec((tm, o_c.shape[1]), row),
                  gate(0), gate(1), gate(2),
                  proj(w_a), proj(w_b), proj(w_c)],
        out_specs=pl.BlockSpec((tm, tn), lambda i, j: (i, j)),
        out_shape=jax.ShapeDtypeStruct((m, d), MXU_DTYPE),
        compiler_params=_cparams(("parallel", "parallel")),
    )(h, o_a, o_b, o_c, w_in, w_in, w_in, w_a, w_b, w_c)


FFN_HALO = 16


def _ffn_kernel(h_ref, hp_ref, x_ref, wg_ref, wu_ref, wd_ref, cw_ref, cb_ref, g_ref, o_ref, hn_ref, hext_ref,
                *, tiles_per_seq):
    i = pl.program_id(0)
    f = pl.program_id(1)
    tm = h_ref.shape[0]

    @pl.when(f == 0)
    def _():
        first = (i % tiles_per_seq) == 0
        hext_ref[0:FFN_HALO, :] = jnp.where(first, jnp.zeros_like(hp_ref[...]), hp_ref[...])
        hext_ref[FFN_HALO:, :] = h_ref[...]
        o_ref[...] = x_ref[...]

    tf = wg_ref.shape[2]
    halves = [slice(0, tf // 2), slice(tf // 2, tf)]
    a = [_dot(hext_ref[...], wg_ref[0, :, s]) for s in halves]
    u = [_dot(h_ref[...], wu_ref[0, :, s]) for s in halves]
    cw = cw_ref[...]
    cb = cb_ref[...]
    act = []
    for a_j, u_j, s in zip(a, u, halves):
        c = (cw[0:1, s] * a_j[FFN_HALO - 2:FFN_HALO - 2 + tm] + cw[1:2, s] * a_j[FFN_HALO - 1:FFN_HALO - 1 + tm]
             + cw[2:3, s] * a_j[FFN_HALO:FFN_HALO + tm]) + cb[:, s]
        act.append((jax.nn.gelu(c) * u_j).astype(MXU_DTYPE))
    for act_j, s in zip(act, halves):
        o_ref[...] += _dot(act_j, wd_ref[0, s, :])

    @pl.when(f == pl.num_programs(1) - 1)
    def _():
        hn_ref[...] = _norm_rows(o_ref[...], g_ref[...]).astype(hn_ref.dtype)


def _conv_ffn(h, x, w_gate, w_up, w_down, layer, conv_w, conv_b, g, seq, tm=512):
    m, d = h.shape
    ff = w_gate.shape[2]
    tf = FFN_TF
    assert seq % tm == 0
    hb = tm // FFN_HALO
    return pl.pallas_call(
        functools.partial(_ffn_kernel, tiles_per_seq=seq // tm),
        name="conv_ffn",
        grid=(m // tm, ff // tf),
        in_specs=[pl.BlockSpec((tm, d), lambda i, f: (i, 0)),
                  pl.BlockSpec((FFN_HALO, d), lambda i, f: (jnp.maximum(i * hb - 1, 0), 0)),
                  pl.BlockSpec((tm, d), lambda i, f: (i, 0)),
                  pl.BlockSpec((1, d, tf), lambda i, f: (layer, 0, f)),
                  pl.BlockSpec((1, d, tf), lambda i, f: (layer, 0, f)),
                  pl.BlockSpec((1, tf, d), lambda i, f: (layer, f, 0)),
                  pl.BlockSpec((CONV_WIDTH, tf), lambda i, f: (0, f)),
                  pl.BlockSpec((1, tf), lambda i, f: (0, f)),
                  pl.BlockSpec((1, d), lambda i, f: (0, 0))],
        out_specs=[pl.BlockSpec((tm, d), lambda i, f: (i, 0)), pl.BlockSpec((tm, d), lambda i, f: (i, 0))],
        out_shape=[jax.ShapeDtypeStruct((m, d), F32), jax.ShapeDtypeStruct((m, d), MXU_DTYPE)],
        scratch_shapes=[pltpu.VMEM((tm + FFN_HALO, d), MXU_DTYPE)],
        compiler_params=_cparams(("parallel", "arbitrary")),
    )(h, h, x, w_gate, w_up, w_down, conv_w, conv_b.reshape(1, ff), g.reshape(1, d))


def _ple_kernel(h_ref, p_ref, x_ref, wg_ref, wp_ref, g_ref, o_ref, hn_ref):
    gate = jax.nn.sigmoid(_dot(h_ref[...], wg_ref[0]))
    x = x_ref[...] + gate * _dot(p_ref[0].astype(MXU_DTYPE), wp_ref[0])
    o_ref[...] = x
    hn_ref[...] = _norm_rows(x, g_ref[...]).astype(hn_ref.dtype)


def _ple(h, p, x, w_gate, w_proj, layer, g, norm_dtype, tm=512):
    m, d = h.shape
    pd = p.shape[2]
    row = lambda i: (i, 0)
    fixed = lambda i: (0, 0)
    once = pl.Buffered(1)
    return pl.pallas_call(
        _ple_kernel,
        name="ple",
        grid=(m // tm,),
        in_specs=[pl.BlockSpec((tm, d), row),
                  pl.BlockSpec((1, tm, pd), lambda i: (layer, i, 0)),
                  pl.BlockSpec((tm, d), row),
                  pl.BlockSpec((1, d, d), lambda i: (layer, 0, 0), pipeline_mode=once),
                  pl.BlockSpec((1, pd, d), lambda i: (layer, 0, 0), pipeline_mode=once),
                  pl.BlockSpec((1, d), fixed)],
        out_specs=[pl.BlockSpec((tm, d), row), pl.BlockSpec((tm, d), row)],
        out_shape=[jax.ShapeDtypeStruct((m, d), F32), jax.ShapeDtypeStruct((m, d), norm_dtype)],
        compiler_params=_cparams(("parallel",)),
    )(h, p, x, w_gate, w_proj, g.reshape(1, d))


def _compress_kernel(x_ref, w1_ref, w2_ref, pe_ref, o_ref):
    x = x_ref[0, 0]
    pe = pe_ref[0]
    half = x.shape[1]
    lo = _dot((x + pe[0:1]).astype(MXU_DTYPE), w1_ref[0, :half, :])
    hi = _dot((x + pe[1:2]).astype(MXU_DTYPE), w1_ref[0, half:, :])
    n = x.shape[0]
    hid = lo + pltpu.roll(hi, n - 1, 0)
    o_ref[0, 0, 0] = _dot(jax.nn.gelu(hid).astype(MXU_DTYPE), w2_ref[0]).astype(o_ref.dtype)


def _compress(kv_hm, w1, w2, pe):
    b, _, s, d = kv_hm.shape
    nrow = s // CMP_STRIDE
    x = kv_hm.reshape(b, 2 * NSA_GROUPS, nrow, CMP_STRIDE * d)
    return pl.pallas_call(
        _compress_kernel,
        name="nsa_compress",
        grid=(b, 2, NSA_GROUPS),
        in_specs=[pl.BlockSpec((1, 1, nrow, CMP_STRIDE * d), lambda bi, kv, g: (bi, kv * NSA_GROUPS + g, 0, 0)),
                  pl.BlockSpec((1,) + w1.shape[1:], lambda bi, kv, g: (kv, 0, 0)),
                  pl.BlockSpec((1,) + w2.shape[1:], lambda bi, kv, g: (kv, 0, 0)),
                  pl.BlockSpec((1, 2, CMP_STRIDE * d), lambda bi, kv, g: (kv, 0, 0))],
        out_specs=pl.BlockSpec((1, 1, 1, nrow, d), lambda bi, kv, g: (bi, kv, g, 0, 0)),
        out_shape=jax.ShapeDtypeStruct((b, 2, NSA_GROUPS, nrow, d), MXU_DTYPE),
        compiler_params=_cparams(("parallel", "parallel", "parallel")),
    )(x, w1, w2, pe)


def _softmax_pv(lg_ref, mx_ref, acc_ref, v_ref, v_index, n_tiles):
    rows = lg_ref.shape[0]
    m = jnp.max(mx_ref[...], axis=1, keepdims=True)
    mx_ref[...] = jnp.zeros_like(mx_ref)
    acc_ref[...] = jnp.zeros_like(acc_ref)

    def chunk(k0, width):
        p = jnp.exp2(lg_ref[:, pl.ds(k0, width)] - m)
        part = p[:, 0:128]
        for c in range(1, width // 128):
            part = part + p[:, c * 128:(c + 1) * 128]
        mx_ref[...] += part
        acc_ref[...] += _dot(p.astype(MXU_DTYPE), v_ref[v_index + (pl.ds(k0, width), slice(None))])

    def body(j, carry):
        chunk(pl.multiple_of(j * (2 * KT), 2 * KT), 2 * KT)
        return carry

    lax.fori_loop(0, n_tiles // 2, body, 0)

    @pl.when(n_tiles % 2 == 1)
    def _():
        chunk(pl.multiple_of((n_tiles - 1) * KT, KT), KT)

    l = jnp.sum(mx_ref[...], axis=1, keepdims=True)
    return acc_ref[...] / l


def _lane_max(s):
    part = s[:, 0:128]
    for c in range(1, s.shape[1] // 128):
        part = jnp.maximum(part, s[:, c * 128:(c + 1) * 128])
    return part


def _nsa_kernel(q_ref, kc_ref, vc_ref, kslc_ref, vslc_ref, kwin_ref, vwin_ref, gate_ref,
                wslc_ref, wwin_ref, tcmp_ref, ovl_ref, o_ref,
                kslc_aug, kwin_aug, vwin_pad, qaug, lg_ref, mx_ref, acc_ref, score_ref):
    i = pl.program_id(2)
    seq = kslc_ref.shape[2]
    d = HEAD_DIM
    rows = NSA_REP * TQ
    scale = d ** -0.5 * LOG2E
    t0 = i * TQ

    @pl.when(i == 0)
    def _():
        kslc_aug[:, 0:d] = kslc_ref[0, 0]
        srow = lax.broadcasted_iota(jnp.int32, (seq, d), 0)
        lane = lax.broadcasted_iota(jnp.int32, (seq, d), 1)
        kslc_aug[:, d:2 * d] = jnp.where((srow >> 6) == lane, 1.0, 0.0).astype(kslc_aug.dtype)
        kwin_aug[0:NSA_WINDOW, 0:d] = jnp.zeros((NSA_WINDOW, d), kwin_aug.dtype)
        kwin_aug[NSA_WINDOW:, 0:d] = kwin_ref[0, 0]
        prow = lax.broadcasted_iota(jnp.int32, (seq + NSA_WINDOW, d), 0)
        plane = lax.broadcasted_iota(jnp.int32, (seq + NSA_WINDOW, d), 1)
        flag = jnp.where(prow < NSA_WINDOW, jnp.where(plane == MAX_SLC_BLOCKS, UNSELECTED, 0.0), 0.0)
        kwin_aug[:, d:2 * d] = flag.astype(kwin_aug.dtype)
        vwin_pad[0:NSA_WINDOW, :] = jnp.zeros((NSA_WINDOW, d), vwin_pad.dtype)
        vwin_pad[NSA_WINDOW:, :] = vwin_ref[0, 0]

    q4 = q_ref[0].reshape(rows, d)

    ncmp = kc_ref.shape[3]
    kc = kc_ref[0, 0, 0]
    vc = vc_ref[0, 0, 0]
    trow = t0 + lax.broadcasted_iota(jnp.int32, (TQ, ncmp), 0)
    cend = lax.broadcasted_iota(jnp.int32, (TQ, ncmp), 1) * CMP_STRIDE + (CMP_BLOCK - 1)
    valid_c = cend <= trow
    sc_all = _dot_nt(q4, kc) * scale
    c0 = i * (TQ // CMP_STRIDE)
    bias_start = pl.multiple_of(ncmp - 128 - 128 * (c0 // 128), 128)
    pb = []
    for r in range(NSA_REP):
        bias = tcmp_ref[0, r, :, pl.ds(bias_start, ncmp)]
        l = jnp.where(valid_c, sc_all[r * TQ:(r + 1) * TQ] + bias, MASKED)
        m = jnp.max(l, axis=1, keepdims=True)
        e = jnp.where(valid_c, jnp.exp2(l - m), 0.0)
        p = e / jnp.maximum(jnp.sum(e, axis=1, keepdims=True), 1e-30)
        pb.append(p.astype(MXU_DTYPE))
    o_cmp = [_dot(pb[r], vc) for r in range(NSA_REP)]
    imp_t = _dot_nt(ovl_ref[...], pb[0])
    for r in range(1, NSA_REP):
        imp_t = imp_t + _dot_nt(ovl_ref[...], pb[r])

    span = NSA_WINDOW + TQ
    w0 = pl.multiple_of(t0, TQ)
    one_lane = lax.broadcasted_iota(jnp.int32, (rows, d), 1) == MAX_SLC_BLOCKS
    q_win = jnp.concatenate([q4, jnp.where(one_lane, 1.0, 0.0).astype(q4.dtype)], axis=1)
    s = _dot_nt(q_win, kwin_aug[pl.ds(w0, span), :]) * scale + wwin_ref[...].reshape(rows, span)
    m = jnp.max(s, axis=1, keepdims=True)
    p = jnp.exp2(s - m)
    l = jnp.sum(p, axis=1, keepdims=True)
    o_win = _dot(p.astype(MXU_DTYPE), vwin_pad[pl.ds(w0, span), :]) / l

    nblk = MAX_SLC_BLOCKS
    jt = lax.broadcasted_iota(jnp.int32, (nblk, TQ), 0)
    tt = t0 + lax.broadcasted_iota(jnp.int32, (nblk, TQ), 1)
    cur = tt >> 6
    imp_t = imp_t[0:nblk]
    score = jnp.where(jt == 0, 1e9, jnp.where(jt == cur, 1e9, jnp.where(jt == cur - 1, 1e9, imp_t)))
    score = jnp.where(jt * SLC_BLOCK <= tt, score, -jnp.inf)
    score_ref[...] = score
    sub = lax.broadcasted_iota(jnp.int32, (8, TQ), 0)
    sv = [score[8 * v:8 * v + 8] for v in range(nblk // 8)]
    beaten = [jnp.zeros((8, TQ), F32) for _ in sv]
    for jp in range(nblk):
        other = score_ref[jp:jp + 1, :]
        for v in range(nblk // 8):
            if 8 * v > jp:
                hit = other >= sv[v]
            elif 8 * v + 7 <= jp:
                hit = other > sv[v]
            else:
                tie_loses = jnp.where(sub > jp - 8 * v, 1.0, 0.0)
                beaten[v] = beaten[v] + jnp.where(other == sv[v], tie_loses, 0.0)
                hit = other > sv[v]
            beaten[v] = beaten[v] + jnp.where(hit, 1.0, 0.0)
    aug_t = jnp.concatenate([jnp.where(b < SLC_TOPN, 0.0, UNSELECTED) for b in beaten], axis=0)
    row = lax.broadcasted_iota(jnp.int32, (d - nblk, TQ), 0)
    aug_t = jnp.concatenate([aug_t, jnp.where(row == 0, 1.0, 0.0)], axis=0)
    aug = aug_t.T
    qaug[:, 0:d] = q4
    for r in range(NSA_REP):
        qaug[r * TQ:(r + 1) * TQ, d:2 * d] = aug.astype(qaug.dtype)
    qa = qaug[...]

    n_tiles = (t0 + TQ + KT - 1) // KT
    off = t0 - (n_tiles - 1) * KT
    mx_ref[...] = jnp.full(mx_ref.shape, MASKED, F32)

    def slc_tile(kt, tmpl_start, width=KT):
        k0 = pl.multiple_of(kt * KT, KT)
        s = _dot_nt(qa, kslc_aug[pl.ds(k0, width), :]) * scale
        if tmpl_start is not None:
            ts = pl.multiple_of(tmpl_start, 128)
            s = s + wslc_ref[:, :, pl.ds(ts, width)].reshape(rows, width)
        lg_ref[:, pl.ds(k0, width)] = s
        mx_ref[...] = jnp.maximum(mx_ref[...], _lane_max(s))

    n_far = jnp.maximum(n_tiles - 2, 0)

    def far_body(j, carry):
        slc_tile(2 * j, None, 2 * KT)
        return carry

    lax.fori_loop(0, n_far // 2, far_body, 0)

    @pl.when(n_far % 2 == 1)
    def _():
        slc_tile(n_far - 1, None)

    @pl.when(n_tiles >= 2)
    def _():
        slc_tile(n_tiles - 2, TMPL_C0 - off - KT)

    slc_tile(n_tiles - 1, TMPL_C0 - off)
    o_slc = _softmax_pv(lg_ref, mx_ref, acc_ref, vslc_ref, (0, 0), n_tiles)

    g = jax.nn.sigmoid(gate_ref[...])
    for r in range(NSA_REP):
        rs = slice(r * TQ, (r + 1) * TQ)
        o = (g[:, 3 * r:3 * r + 1] * o_cmp[r] + g[:, 3 * r + 1:3 * r + 2] * o_slc[rs]
             + g[:, 3 * r + 2:3 * r + 3] * o_win[rs])
        o_ref[:, r * d:(r + 1) * d] = o.astype(o_ref.dtype)


def _nsa(hm, cmp_kv, misc, wslc, wwin, tcmp, ovl):
    b, _, seq, d = hm.shape
    nq = seq // TQ
    rows = NSA_REP * TQ
    ncmp = cmp_kv.shape[3]
    kv_spec = lambda head: pl.BlockSpec((1, 1, seq, d), lambda bi, g, i: (bi, head + g, 0, 0))
    once = pl.Buffered(1)
    return pl.pallas_call(
        _nsa_kernel,
        name="nsa",
        grid=(b, NSA_GROUPS, nq),
        in_specs=[pl.BlockSpec((1, NSA_REP, TQ, d), lambda bi, g, i: (bi, g, i, 0)),
                  pl.BlockSpec((1, 1, 1, ncmp, d), lambda bi, g, i: (bi, 0, g, 0, 0)),
                  pl.BlockSpec((1, 1, 1, ncmp, d), lambda bi, g, i: (bi, 1, g, 0, 0)),
                  kv_spec(HM_K_SLC), kv_spec(HM_V_SLC), kv_spec(HM_K_WIN), kv_spec(HM_V_WIN),
                  pl.BlockSpec((TQ, 128), lambda bi, g, i: (bi * nq + i, MISC_GATE + g)),
                  pl.BlockSpec((NSA_REP, TQ, TMPL_W), lambda bi, g, i: (g, 0, 0), pipeline_mode=once),
                  pl.BlockSpec((NSA_REP, TQ, NSA_WINDOW + TQ), lambda bi, g, i: (g, 0, 0), pipeline_mode=once),
                  pl.BlockSpec((1, NSA_REP) + tcmp.shape[2:], lambda bi, g, i: (i % tcmp.shape[0], g, 0, 0)),
                  pl.BlockSpec(ovl.shape, lambda bi, g, i: (0, 0), pipeline_mode=once)],
        out_specs=pl.BlockSpec((TQ, NSA_REP * d), lambda bi, g, i: (bi * nq + i, g)),
        out_shape=jax.ShapeDtypeStruct((b * seq, NSA_HEADS * d), MXU_DTYPE),
        scratch_shapes=[pltpu.VMEM((seq, 2 * d), MXU_DTYPE),
                        pltpu.VMEM((seq + NSA_WINDOW, 2 * d), MXU_DTYPE),
                        pltpu.VMEM((seq + NSA_WINDOW, d), MXU_DTYPE),
                        pltpu.VMEM((rows, 2 * d), MXU_DTYPE),
                        pltpu.VMEM((rows, seq), F32),
                        pltpu.VMEM((rows, 128), F32),
                        pltpu.VMEM((rows, d), F32),
                        pltpu.VMEM((MAX_SLC_BLOCKS, TQ), F32)],
        compiler_params=_cparams(("parallel", "parallel", "arbitrary")),
    )(hm, cmp_kv, cmp_kv, hm, hm, hm, hm, misc, wslc, wwin, tcmp, ovl)


def _sb_kernel(q_ref, k_ref, v_ref, upper_ref, o_ref, acc_ref, carry_ref):
    i = pl.program_id(1)
    t = SB_T
    d = HEAD_DIM
    scale = d ** -0.5 * LOG2E
    upper = upper_ref[...]
    acc_ref[...] = jnp.zeros_like(acc_ref)
    carry_ref[...] = jnp.zeros_like(carry_ref)

    heads = range(SB_HEADS)

    def tiles(kts, mask, heads=heads):
        k0 = [pl.multiple_of(kt * t, t) for kt in kts]
        chains = [(j, h) for j in range(len(kts)) for h in heads]
        z = {c: _dot_nt(q_ref[0, c[1]], k_ref[0, c[1], pl.ds(k0[c[0]], t), :]) * scale for c in chains}
        neg_abs = {c: lax.bitcast_convert_type(lax.bitcast_convert_type(z[c], jnp.int32) | INT_MIN, F32)
                   for c in chains}
        ls = {c: jnp.minimum(z[c], 0.0) - jnp.log2(1.0 + jnp.exp2(neg_abs[c])) for c in chains}
        lk = {c: ls[c] - z[c] for c in chains}
        if mask is not None:
            lk = {c: jnp.where(mask, lk[c], 0.0) for c in chains}
        carry = {}
        for h in heads:
            run = carry_ref[h]
            for j in range(len(kts)):
                carry[(j, h)] = run
                run = run + jnp.sum(lk[(j, h)], axis=1, keepdims=True)
            carry_ref[h] = run
        hi = {c: lk[c].astype(MXU_DTYPE) for c in chains}
        lo = {c: (lk[c] - hi[c].astype(F32)).astype(MXU_DTYPE) for c in chains}
        after = {c: carry[c] + (_dot(hi[c], upper) + _dot(lo[c], upper)) for c in chains}
        w = {c: jnp.exp2(ls[c] + after[c]) for c in chains}
        if mask is not None:
            w = {c: jnp.where(mask, w[c], 0.0) for c in chains}
        for j, h in chains:
            acc_ref[h] += _dot(w[(j, h)].astype(MXU_DTYPE), v_ref[0, h, pl.ds(k0[j], t), :])

    strict = lax.broadcasted_iota(jnp.int32, (t, t), 1) < lax.broadcasted_iota(jnp.int32, (t, t), 0)
    tiles([i], strict)

    def body(n, carry):
        for hs in SB_HEAD_GROUPS:
            tiles([i - 1 - 2 * n, i - 2 - 2 * n], None, hs)
        return carry

    lax.fori_loop(0, i // 2, body, 0)

    @pl.when(i % 2 == 1)
    def _():
        tiles([0], None)

    for h in range(SB_HEADS):
        o_ref[:, h * d:(h + 1) * d] = acc_ref[h].astype(o_ref.dtype)


def _stick_breaking(hm, upper):
    b, _, seq, d = hm.shape
    nq = seq // SB_T
    return pl.pallas_call(
        _sb_kernel,
        grid=(b, nq),
        in_specs=[pl.BlockSpec((1, SB_HEADS, SB_T, d), lambda bi, i: (bi, HM_SB_Q // SB_HEADS, i, 0)),
                  pl.BlockSpec((1, SB_HEADS, seq, d), lambda bi, i: (bi, HM_SB_K // SB_HEADS, 0, 0)),
                  pl.BlockSpec((1, SB_HEADS, seq, d), lambda bi, i: (bi, HM_SB_V // SB_HEADS, 0, 0)),
                  pl.BlockSpec(upper.shape, lambda bi, i: (0, 0))],
        out_specs=pl.BlockSpec((SB_T, SB_HEADS * d), lambda bi, i: (bi * nq + i, 0)),
        out_shape=jax.ShapeDtypeStruct((b * seq, SB_HEADS * d), MXU_DTYPE),
        scratch_shapes=[pltpu.VMEM((SB_HEADS, SB_T, d), F32), pltpu.VMEM((SB_HEADS, SB_T, 1), F32)],
        name="stick_breaking",
        compiler_params=_cparams(("parallel", "parallel")),
    )(hm, hm, hm, upper)


def _dsa_kv_kernel(c_ref, g_ref, wk_ref, wv_ref, k_ref, v_ref):
    c = c_ref[...]
    y = c * lax.rsqrt(jnp.mean(c * c, axis=-1, keepdims=True) + EPS)
    y = (y * g_ref[...]).astype(MXU_DTYPE)
    k_ref[...] = _dot(y, wk_ref[...]).astype(k_ref.dtype)
    v_ref[...] = _dot(y, wv_ref[...]).astype(v_ref.dtype)


def _dsa_kv(misc, kv_norm, w_uk, w_uv, tm=512):
    m = misc.shape[0]
    r = DSA_KV_RANK
    out = jax.ShapeDtypeStruct((m, HEAD_DIM), MXU_DTYPE)
    return pl.pallas_call(
        _dsa_kv_kernel,
        name="dsa_kv",
        grid=(m // tm,),
        in_specs=[pl.BlockSpec((tm, r), lambda i: (i, MISC_CKV * 128 // r)),
                  pl.BlockSpec((1, r), lambda i: (0, 0)),
                  pl.BlockSpec((r, HEAD_DIM), lambda i: (0, 0)),
                  pl.BlockSpec((r, HEAD_DIM), lambda i: (0, 0))],
        out_specs=[pl.BlockSpec((tm, HEAD_DIM), lambda i: (i, 0)), pl.BlockSpec((tm, HEAD_DIM), lambda i: (i, 0))],
        out_shape=[out, out],
        compiler_params=_cparams(("parallel",)),
    )(misc, kv_norm.reshape(1, r), w_uk, w_uv)


PLANE_GROUPS_PER_TILE = KT // 256


def _bit_planes(words):
    a = list(words)
    j, m = 16, 0x0000FFFF
    while j:
        sh = jnp.full(a[0].shape, j, jnp.int32)
        for k in range(32):
            if not k & j:
                t = (a[k] ^ lax.shift_right_logical(a[k + j], sh)) & m
                a[k] = a[k] ^ t
                a[k + j] = a[k + j] ^ (t << j)
        j >>= 1
        m = (m ^ (m << j)) & 0xFFFFFFFF
    return a


def _dsa_kernel(q_ref, k_ref, v_ref, iq_ref, ik_ref, iw_ref, wd_ref, low_ref, o_ref,
                key_ref, plane_ref, alive_ref, add_ref, lg_ref, mx_ref, acc_ref, *, n_keep):
    i = pl.program_id(1)
    d = HEAD_DIM
    tq = DSA_TQ
    rows = DSA_HEADS * tq
    scale = d ** -0.5 * LOG2E
    t0 = i * tq
    n_tiles = (t0 + tq + KT - 1) // KT
    off = t0 - (n_tiles - 1) * KT

    iq = iq_ref[...].astype(MXU_DTYPE)
    qh = [iq[:, h * IDX_DIM:(h + 1) * IDX_DIM] for h in range(IDX_HEADS)]
    wi_t = iw_ref[...].T * (IDX_HEADS ** -0.5) * (IDX_DIM ** -0.5)
    wh = [wi_t[IDX_DIM + h:IDX_DIM + h + 1, :] for h in range(IDX_HEADS)]
    tq_row = t0 + lax.broadcasted_iota(jnp.int32, (KT, tq), 1)
    krow = lax.broadcasted_iota(jnp.int32, (KT, tq), 0)

    def column_sum(a):
        return jnp.sum(a.reshape(KT // 8, 8, tq), axis=0)

    def score_tile(kt, causal):
        k0 = pl.multiple_of(kt * KT, KT)
        ki = ik_ref[pl.ds(k0, KT), 0:IDX_DIM].astype(MXU_DTYPE)
        dots = [_dot_nt(ki, qh[h]) for h in range(IDX_HEADS)]
        sc = wh[0] * jnp.maximum(dots[0], 0.0)
        for h in range(1, IDX_HEADS):
            sc = sc + wh[h] * jnp.maximum(dots[h], 0.0)
        if causal:
            sc = jnp.where(k0 + krow <= tq_row, sc, -jnp.inf)
        bits = lax.bitcast_convert_type(sc, jnp.int32)
        sign = bits >> 31
        key = (bits ^ (sign & 0x7FFFFFFF)) - sign
        key_ref[pl.ds(k0, KT), :] = key
        ukey = key ^ INT_MIN
        for g in range(PLANE_GROUPS_PER_TILE):
            words = [ukey[(32 * g + w) * 8:(32 * g + w + 1) * 8] for w in range(32)]
            for x, plane in enumerate(_bit_planes(words)):
                plane_ref[x, PLANE_GROUPS_PER_TILE * kt + g] = plane

    @pl.when((pl.program_id(0) == 0) & (i == 0))
    def _():
        plane_ref[...] = jnp.zeros_like(plane_ref)

    def score_body(kt, carry):
        score_tile(kt, False)
        return carry

    lax.fori_loop(0, n_tiles - 1, score_body, 0)
    score_tile(n_tiles - 1, True)

    ngrp = alive_ref.shape[0]
    for g in range(ngrp):
        alive_ref[g] = jnp.where(g < PLANE_GROUPS_PER_TILE * n_tiles, -1, 0) + jnp.zeros((8, tq), jnp.int32)

    def bit_body(x, carry):
        thr_u, remaining = carry
        ones = [alive_ref[g] & plane_ref[x, g] for g in range(ngrp)]
        c = lax.population_count(ones[0])
        for g in range(1, ngrp):
            c = c + lax.population_count(ones[g])
        c = jnp.sum(c, axis=0, keepdims=True)
        take = c >= remaining
        for g in range(ngrp):
            alive_ref[g] = jnp.where(take, ones[g], alive_ref[g] ^ ones[g])
        bit = jnp.int32(1) << (31 - x)
        return jnp.where(take, thr_u | bit, thr_u), jnp.where(take, remaining, remaining - c)

    thr_u, need = lax.fori_loop(0, 32, bit_body,
                                (jnp.zeros((1, tq), jnp.int32), jnp.full((1, tq), n_keep, jnp.int32)))
    thr = thr_u ^ INT_MIN
    n_equal = lax.population_count(alive_ref[0])
    for g in range(1, ngrp):
        n_equal = n_equal + lax.population_count(alive_ref[g])
    n_equal = jnp.sum(n_equal, axis=0, keepdims=True)
    need = need.astype(F32)
    surplus = jnp.max(n_equal.astype(F32) - need)

    @pl.when(surplus <= 0)
    def _():
        def mask_body(kt, carry):
            k0 = pl.multiple_of(kt * KT, KT)
            add_ref[:, pl.ds(k0, KT)] = jnp.where(key_ref[pl.ds(k0, KT), :] >= thr, 0.0, MASKED).T
            return carry

        lax.fori_loop(0, n_tiles, mask_body, 0)

    @pl.when(surplus > 0)
    def _():
        def mask_body(kt, seen):
            k0 = pl.multiple_of(kt * KT, KT)
            kk = key_ref[pl.ds(k0, KT), :]
            eq = jnp.where(kk == thr, 1.0, 0.0)
            before = seen + _dot(low_ref[...], eq.astype(MXU_DTYPE))
            tie = jnp.where(before < need, 0.0, MASKED)
            add_t = jnp.where(kk > thr, 0.0, jnp.where(kk == thr, tie, MASKED))
            add_ref[:, pl.ds(k0, KT)] = add_t.T
            return seen + jnp.sum(column_sum(eq), axis=0, keepdims=True)

        lax.fori_loop(0, n_tiles, mask_body, jnp.zeros((1, tq), F32))

    q4 = q_ref[0].reshape(rows, d)
    mx_ref[...] = jnp.full(mx_ref.shape, MASKED, F32)

    def att_tile(kt, tmpl_start, width=KT):
        k0 = pl.multiple_of(kt * KT, KT)
        s = _dot_nt(q4, k_ref[0, pl.ds(k0, width), :]) * scale
        addm = add_ref[:, pl.ds(k0, width)]
        s = s + jnp.concatenate([addm] * DSA_HEADS, axis=0)
        if tmpl_start is not None:
            ts = pl.multiple_of(tmpl_start, 128)
            s = s + wd_ref[:, :, pl.ds(ts, width)].reshape(rows, width)
        lg_ref[:, pl.ds(k0, width)] = s
        mx_ref[...] = jnp.maximum(mx_ref[...], _lane_max(s))

    n_far = jnp.maximum(n_tiles - 2, 0)

    def far_body(j, carry):
        att_tile(2 * j, None, 2 * KT)
        return carry

    lax.fori_loop(0, n_far // 2, far_body, 0)

    @pl.when(n_far % 2 == 1)
    def _():
        att_tile(n_far - 1, None)

    @pl.when(n_tiles >= 2)
    def _():
        att_tile(n_tiles - 2, TMPL_C0 - off - KT)

    att_tile(n_tiles - 1, TMPL_C0 - off)
    o = _softmax_pv(lg_ref, mx_ref, acc_ref, v_ref, (0,), n_tiles)
    for r in range(DSA_HEADS):
        o_ref[:, r * d:(r + 1) * d] = o[r * tq:(r + 1) * tq].astype(o_ref.dtype)


def _dsa(hm, k, v, misc, wdsa, low):
    b, _, seq, d = hm.shape
    tq = DSA_TQ
    nq = seq // tq
    rows = DSA_HEADS * tq
    k = k.reshape(b, seq, d)
    v = v.reshape(b, seq, d)
    n_keep = min(DSA_TOPK, seq // 4)
    ngrp = PLANE_GROUPS_PER_TILE * (seq // KT)
    return pl.pallas_call(
        functools.partial(_dsa_kernel, n_keep=n_keep),
        name="dsa",
        grid=(b, nq),
        in_specs=[pl.BlockSpec((1, DSA_HEADS, tq, d), lambda bi, i: (bi, HM_DSA_Q // DSA_HEADS, i, 0)),
                  pl.BlockSpec((1, seq, d), lambda bi, i: (bi, 0, 0)),
                  pl.BlockSpec((1, seq, d), lambda bi, i: (bi, 0, 0)),
                  pl.BlockSpec((tq, IDX_HEADS * IDX_DIM), lambda bi, i: (bi * nq + i, MISC_IDXQ)),
                  pl.BlockSpec((seq, 128), lambda bi, i: (bi, MISC_IDXK)),
                  pl.BlockSpec((tq, 128), lambda bi, i: (bi * nq + i, MISC_IDXK)),
                  pl.BlockSpec((DSA_HEADS, tq, TMPL_W), lambda bi, i: (0, 0, 0), pipeline_mode=pl.Buffered(1)),
                  pl.BlockSpec(low.shape, lambda bi, i: (0, 0), pipeline_mode=pl.Buffered(1))],
        out_specs=pl.BlockSpec((tq, DSA_HEADS * d), lambda bi, i: (bi * nq + i, 0)),
        out_shape=jax.ShapeDtypeStruct((b * seq, DSA_HEADS * d), MXU_DTYPE),
        scratch_shapes=[pltpu.VMEM((seq, tq), jnp.int32),
                        pltpu.VMEM((32, ngrp, 8, tq), jnp.int32),
                        pltpu.VMEM((ngrp, 8, tq), jnp.int32),
                        pltpu.VMEM((tq, seq), F32),
                        pltpu.VMEM((rows, seq), F32),
                        pltpu.VMEM((rows, 128), F32),
                        pltpu.VMEM((rows, d), F32)],
        compiler_params=_cparams(("arbitrary", "arbitrary")),
    )(hm, k, v, misc, misc, misc, wdsa, low)


def _t5_bucket(dist):
    n = jnp.maximum(dist, 0)
    max_exact = REL_BUCKETS // 2
    nf = jnp.maximum(n, 1).astype(F32)
    large = max_exact + (jnp.log(nf / max_exact) / math.log(REL_MAX_DIST / max_exact)
                         * (REL_BUCKETS - max_exact)).astype(jnp.int32)
    large = jnp.minimum(large, REL_BUCKETS - 1)
    return jnp.where(n < max_exact, n, large)


def _bias_templates(rel_tab, ncmp):
    far = REL_MAX_DIST
    by_dist = rel_tab[_t5_bucket(jnp.arange(far + 1))] - rel_tab[REL_BUCKETS - 1][None, :]
    by_dist = by_dist.T * LOG2E

    def build(dist, valid, fill=MASKED):
        t = by_dist[:, np.clip(dist, 0, far)]
        return jnp.where(valid[None], t, fill).astype(F32)

    def toeplitz(u, nrows, width):
        nh, l = u.shape
        return jnp.tile(u, (1, nrows))[:, :nrows * (l - 1)].reshape(nh, nrows, l - 1)[:, :, :width]

    def diagonals(nrows, width):
        l = width + nrows
        k = np.arange(l)
        return np.where(k < width, k, k - l)

    def causal(heads, nrows):
        dist = TMPL_C0 - diagonals(nrows, TMPL_W)
        return toeplitz(build(dist, dist >= 0)[heads], nrows, TMPL_W)

    nsa = slice(0, NSA_HEADS)
    span = NSA_WINDOW + TQ
    dist = NSA_WINDOW - diagonals(TQ, span)
    window = toeplitz(build(dist, (dist >= 0) & (dist < NSA_WINDOW))[nsa], TQ, span)
    cc = np.arange(CMP_BAND)[None, :] - CMP_BAND // 2
    dist = np.arange(TQ)[:, None] - CMP_STRIDE * cc - (CMP_BLOCK - 1)
    band = build(dist, dist >= 0, 0.0)[nsa]
    width = 2 * ncmp - 128
    step = TQ // CMP_STRIDE
    cmp = []
    for v in range(128 // step):
        left = ncmp - 128 + step * v
        canvas = jnp.pad(band, ((0, 0), (0, 0), (left, width - left)))
        cmp.append(canvas[:, :, CMP_BAND // 2:CMP_BAND // 2 + width])
    return causal(nsa, TQ), causal(slice(NSA_HEADS, None), DSA_TQ), window, jnp.stack(cmp)


def _pack_w_in(w_in):
    d3 = 3 * w_in.shape[1]
    kv = NSA_GROUPS * HEAD_DIM
    o_q = d3
    o_kc = o_q + NSA_HEADS * HEAD_DIM
    o_vc, o_ks, o_vs, o_kw, o_vw = (o_kc + j * kv for j in range(1, 6))
    o_g = o_vw + kv
    o_sbq = o_g + 3 * NSA_HEADS
    o_sbk = o_sbq + SB_HEADS * HEAD_DIM
    o_sbv = o_sbk + SB_HEADS * HEAD_DIM
    o_dq = o_sbv + SB_HEADS * HEAD_DIM
    o_ckv = o_dq + DSA_HEADS * HEAD_DIM
    o_iq = o_ckv + DSA_KV_RANK
    o_ik = o_iq + IDX_HEADS * IDX_DIM
    o_iw = o_ik + IDX_DIM
    w = w_in.astype(MXU_DTYPE)
    c = lambda a, n: w[:, :, a:a + n]
    zeros = lambda n: jnp.zeros(w.shape[:2] + (n,), w.dtype)
    tn = PROJ_TN
    assert all(o % tn == 0 for o in (d3, o_q, o_kc, o_ks, o_kw)) and 2 * kv == tn and NSA_HEADS * HEAD_DIM == 2 * tn
    blocks_a = (o_q // tn, o_q // tn + 1, o_ks // tn, o_kw // tn)
    w_b = c(o_sbq, (3 * SB_HEADS + DSA_HEADS) * HEAD_DIM)
    gw = 3 * NSA_REP
    w_misc = jnp.concatenate([c(o_iq, IDX_HEADS * IDX_DIM), c(o_ckv, DSA_KV_RANK),
                              c(o_ik, IDX_DIM), c(o_iw, IDX_HEADS), zeros(128 - IDX_DIM - IDX_HEADS),
                              c(o_g, gw), zeros(128 - gw), c(o_g + gw, gw), zeros(128 - gw)], axis=2)
    return w, blocks_a, o_kc // tn, w_b, w_misc


def kernel(x, p, w_in, norm_mix, norm_ffn, norm_ple, norm_final, w_proj_a, w_proj_b, w_proj_c, w_out,
           cmp_k_w1, cmp_k_w2, cmp_k_pe, cmp_v_w1, cmp_v_w2, cmp_v_pe, dsa_kv_norm, dsa_w_uk, dsa_w_uv,
           rel_bias_table, ffn_w_gate, ffn_w_up, ffn_w_down, ffn_conv_w, ffn_conv_b, ple_w_gate, ple_w_proj):
    batch, seq, d_model = x.shape
    depth = w_in.shape[0]
    m = batch * seq
    assert seq % KT == 0 and seq % SB_T == 0 and seq // SLC_BLOCK <= MAX_SLC_BLOCKS
    ncmp = seq // CMP_STRIDE
    bf = lambda w: w.astype(MXU_DTYPE)

    w_all, blocks_a, block_cmp, w_hm_b, w_misc = _pack_w_in(w_in)
    a0, a1, a2, a3 = blocks_a
    nsa_block = lambda j: jnp.where(j == 0, a0, jnp.where(j == 1, a1, jnp.where(j == 2, a2, a3)))
    w_a, w_b, w_c, w_o = bf(w_proj_a), bf(w_proj_b), bf(w_proj_c), bf(w_out)
    cmp_w1 = bf(jnp.stack([cmp_k_w1, cmp_v_w1], axis=1))
    cmp_w2 = bf(jnp.stack([cmp_k_w2, cmp_v_w2], axis=1))
    cmp_pe = jnp.stack([cmp_k_pe, cmp_v_pe], axis=1).reshape(depth, 2, 2, CMP_STRIDE * HEAD_DIM)
    w_uk, w_uv = bf(dsa_w_uk), bf(dsa_w_uv)
    f_gate, f_up, f_down = bf(ffn_w_gate), bf(ffn_w_up), bf(ffn_w_down)
    pl_gate, pl_proj = bf(ple_w_gate), bf(ple_w_proj)

    wslc, wdsa, wwin, tcmp = _bias_templates(rel_bias_table, ncmp)
    cc = np.arange(ncmp)[None, :]
    jj = np.arange(128)[:, None]
    per = SLC_BLOCK // CMP_STRIDE
    ovl = ((cc >= per * jj - (CMP_BLOCK // CMP_STRIDE - 1)) & (cc <= per * jj + per - 1)
           & (cc < ncmp - 1) & (jj < seq // SLC_BLOCK))
    ovl = jnp.asarray(ovl, MXU_DTYPE)
    low = jnp.asarray(np.arange(KT)[:, None] > np.arange(KT)[None, :], MXU_DTYPE)
    upper = jnp.asarray(np.arange(SB_T)[:, None] > np.arange(SB_T)[None, :], MXU_DTYPE)

    x = x.reshape(m, d_model)
    p = p.reshape(depth, m, p.shape[-1])
    h = _rmsnorm(x, norm_mix[0], MXU_DTYPE)
    for i in range(depth):
        hm_a = _matmul_heads(h, w_all, i, nsa_block, len(blocks_a), batch, MXU_DTYPE, "in_proj_nsa")
        hm_b = _matmul_heads(h, w_hm_b, i, lambda j: j, w_hm_b.shape[2] // PROJ_TN, batch, MXU_DTYPE, "in_proj_sb_dsa")
        cmp_in = _matmul_heads(h, w_all, i, lambda j: block_cmp, 1, batch, F32, "in_proj_cmp")
        misc = _matmul(h, w_misc[i], F32, 1024, MISC_COLS, "in_proj_misc")
        cmp_kv = _compress(cmp_in, cmp_w1[i], cmp_w2[i], cmp_pe[i])
        o_a = _nsa(hm_a, cmp_kv, misc, wslc, wwin, tcmp, ovl)
        o_b = _stick_breaking(hm_b, upper)
        dk, dv = _dsa_kv(misc, dsa_kv_norm[i], w_uk[i], w_uv[i])
        o_c = _dsa(hm_b, dk, dv, misc, wdsa, low)
        y = _merge(h, o_a, o_b, o_c, w_all, i, w_a, w_b, w_c)
        x, h = _matmul_residual(y, w_o, i, x, norm_ffn[i])
        x, h = _conv_ffn(h, x, f_gate, f_up, f_down, i, ffn_conv_w[i], ffn_conv_b[i], norm_ple[i], seq)
        last = i == depth - 1
        x, h = _ple(h, p, x, pl_gate, pl_proj, i, norm_final if last else norm_mix[i + 1],
                    F32 if last else MXU_DTYPE)
    return h.reshape(batch, seq, d_model)
```
